```python
import math
import jax, jax.numpy as jnp
from jax import lax
import numpy as np

D_MODEL = 1024
BATCH = 2
SEQ = 8192
DEPTH = 2

CTX_LEN = 256
GRID_W = 64
EPS = 1e-6
BRANCH = D_MODEL // 4
D_MIX = 4 * BRANCH
CONV_W = 4
GLA_HEADS = 4
GLA_DK = BRANCH // 8
GLA_DV = BRANCH // GLA_HEADS
GLA_GATE_RANK = 16
GLA_GATE_TAU = 16.0
GLA_CHUNK = 64
LRU_BLOCKS = 4
LRU_BLOCK_W = BRANCH // LRU_BLOCKS
LRU_C = 8.0
DIFF_HEADS = 4
DIFF_DV = BRANCH // DIFF_HEADS
DIFF_D = DIFF_DV // 2
Q_BLOCK = 128
ROPE_BASE = 10000.0
SSD_HEADS = 4
SSD_P = BRANCH // SSD_HEADS
SSD_GROUPS = 2
SSD_N = 64
SSD_CHUNK = 64
SSD_CONV_DIM = SSD_HEADS * SSD_P + 2 * SSD_GROUPS * SSD_N
GLA_SIZES = (GLA_HEADS * GLA_DK, GLA_HEADS * GLA_DK, GLA_HEADS * GLA_DV, GLA_GATE_RANK, GLA_GATE_RANK, BRANCH)
LRU_SIZES = (BRANCH, BRANCH)
DIFF_SIZES = (DIFF_HEADS * 2 * DIFF_D, DIFF_HEADS * 2 * DIFF_D, DIFF_HEADS * DIFF_DV, BRANCH)
SSD_SIZES = (SSD_CONV_DIM, SSD_HEADS, SSD_HEADS, BRANCH)
GROUP_SIZES = (sum(GLA_SIZES), sum(LRU_SIZES), sum(DIFF_SIZES), sum(SSD_SIZES))
D_IN = sum(GROUP_SIZES)

kernel_name = 'hybrid_gla_rglru_diffattn_ssd_block'


def split_sizes(u, sizes):
    idx = []
    acc = 0
    for s in sizes[:-1]:
        acc += s
        idx.append(acc)
    return jnp.split(u, idx, axis=-1)


def rmsnorm(x, w):
    xf = x.astype(jnp.float32)
    y = xf * lax.rsqrt(jnp.mean(xf * xf, axis=-1, keepdims=True) + EPS)
    return (y * w.astype(jnp.float32)).astype(x.dtype)


def dwconv(x, w, b):
    t = x.shape[1]
    left = CONV_W // 2
    xp = jnp.pad(x, ((0, 0), (left, CONV_W - 1 - left), (0, 0)))
    y = b + xp[:, 0:t] * w[0]
    for j in range(1, CONV_W):
        y = y + xp[:, j:j + t] * w[j]
    return y


def chunk_state_scan(decay, upd, s0):
    def step(s, inp):
        d, u = inp
        return d * s + u, s
    s_fin, s_in = lax.scan(step, s0, (decay, upd))
    return s_in, s_fin


def gla_chunked(q, k, v, log_a, s0):
    b, t, h, _ = q.shape
    n = t // GLA_CHUNK
    r = lambda a: a.reshape(b, n, GLA_CHUNK, h, a.shape[-1])
    q, k, v, log_a = r(q), r(k), r(v), r(log_a)
    g = jnp.cumsum(log_a.astype(jnp.float32), axis=2)
    g_last = g[:, :, -1:]
    qg = q * jnp.exp(g)
    kg = k * jnp.exp(-g)
    kd = k * jnp.exp(g_last - g)
    mask = jnp.tril(jnp.ones((GLA_CHUNK, GLA_CHUNK), dtype=bool))
    att = jnp.where(mask, jnp.einsum('bnihk,bnjhk->bnhij', qg, kg), 0.0)
    o = jnp.einsum('bnhij,bnjhv->bnihv', att, v)
    upd = jnp.einsum('bnjhk,bnjhv->nbhkv', kd, v)
    decay = jnp.exp(g_last[:, :, 0]).transpose(1, 0, 2, 3)[..., None]
    s_in, s_fin = chunk_state_scan(decay, upd, s0)
    o = o + jnp.einsum('bnihk,nbhkv->bnihv', qg, s_in)
    return o.reshape(b, t, h, -1), s_fin


def gla_dir(q, k, v, lr, w2, b2, s0, reverse):
    log_a = jax.nn.log_sigmoid((lr @ w2 + b2).astype(jnp.float32)) / GLA_GATE_TAU
    log_a = log_a.reshape(k.shape)
    if reverse:
        q, k, v, log_a = (jnp.flip(a, axis=1) for a in (q, k, v, log_a))
    o, s = gla_chunked(q, k, v, log_a, s0)
    if reverse:
        o = jnp.flip(o, axis=1)
    return o, s


def gla_mixer(u_ctx, u_lat, w2, b2, norm_w, with_ctx_out):
    def prep(u):
        b, t, _ = u.shape
        q, k, v, lr_f, lr_b, g = split_sizes(u, GLA_SIZES)
        q = q.reshape(b, t, GLA_HEADS, GLA_DK) * (GLA_DK ** -0.5)
        k = k.reshape(b, t, GLA_HEADS, GLA_DK)
        v = v.reshape(b, t, GLA_HEADS, GLA_DV)
        return q, k, v, lr_f, lr_b, g
    qc, kc, vc, lcf, lcb, gc = prep(u_ctx)
    ql, kl, vl, llf, llb, gl = prep(u_lat)
    s0 = jnp.zeros((u_ctx.shape[0], GLA_HEADS, GLA_DK, GLA_DV), jnp.float32)
    ocf, scf = gla_dir(qc, kc, vc, lcf, w2[0], b2[0], s0, False)
    ocb, scb = gla_dir(qc, kc, vc, lcb, w2[1], b2[1], s0, True)
    olf, _ = gla_dir(ql, kl, vl, llf, w2[0], b2[0], scf, False)
    olb, _ = gla_dir(ql, kl, vl, llb, w2[1], b2[1], scb, True)

    def finish(o, g):
        b, t = g.shape[:2]
        y = rmsnorm(o, norm_w).reshape(b, t, BRANCH) * jax.nn.silu(g)
        return y.astype(g.dtype)
    yc = finish(ocf + ocb, gc) if with_ctx_out else None
    return yc, finish(olf + olb, gl)


def lru_combine(e1, e2):
    a1, u1 = e1
    a2, u2 = e2
    return a1 * a2, a2 * u1 + u2


def rglru_dir(x, w_a, b_a, w_x, b_x, lam, h0, reverse):
    b, t, _ = x.shape
    xb = x.reshape(b, t, LRU_BLOCKS, LRU_BLOCK_W)
    r = jax.nn.sigmoid(jnp.einsum('btgi,gij->btgj', xb, w_a).reshape(b, t, BRANCH) + b_a)
    i = jax.nn.sigmoid(jnp.einsum('btgi,gij->btgj', xb, w_x).reshape(b, t, BRANCH) + b_x)
    log_a = -LRU_C * r.astype(jnp.float32) * jax.nn.softplus(-lam.astype(jnp.float32))
    a = jnp.exp(log_a)
    u = jnp.sqrt(-jnp.expm1(2.0 * log_a)) * (i * x).astype(jnp.float32)
    if reverse:
        a, u = jnp.flip(a, axis=1), jnp.flip(u, axis=1)
    u = u.at[:, 0].add(a[:, 0] * h0)
    _, h = lax.associative_scan(lru_combine, (a, u), axis=1)
    h_last = h[:, -1]
    if reverse:
        h = jnp.flip(h, axis=1)
    return h, h_last


def rglru_mixer(u_ctx, u_lat, conv_w, conv_b, w_a, b_a, w_x, b_x, lam, with_ctx_out):
    xc, gc = split_sizes(u_ctx, LRU_SIZES)
    xl, gl = split_sizes(u_lat, LRU_SIZES)
    xc = dwconv(xc, conv_w, conv_b)
    xl = dwconv(xl, conv_w, conv_b)
    h0 = jnp.zeros((xc.shape[0], BRANCH), jnp.float32)
    hcf, scf = rglru_dir(xc, w_a[0], b_a[0], w_x[0], b_x[0], lam[0], h0, False)
    hcb, scb = rglru_dir(xc, w_a[1], b_a[1], w_x[1], b_x[1], lam[1], h0, True)
    hlf, _ = rglru_dir(xl, w_a[0], b_a[0], w_x[0], b_x[0], lam[0], scf, False)
    hlb, _ = rglru_dir(xl, w_a[1], b_a[1], w_x[1], b_x[1], lam[1], scb, True)
    yl = ((hlf + hlb) * jax.nn.silu(gl)).astype(gl.dtype)
    yc = ((hcf + hcb) * jax.nn.silu(gc)).astype(gc.dtype) if with_ctx_out else None
    return yc, yl


def axial_rope_tables(rows):
    n_freq = DIFF_D // 4
    inv = ROPE_BASE ** (-jnp.arange(n_freq, dtype=jnp.float32) / n_freq)
    tpos = jnp.arange(rows * GRID_W)
    pos_r = (tpos // GRID_W).astype(jnp.float32)
    pos_c = (tpos % GRID_W).astype(jnp.float32)
    ang_r = pos_r[:, None] * inv
    ang_c = pos_c[:, None] * inv
    ang = jnp.concatenate([ang_r, ang_r, ang_c, ang_c], axis=-1)
    return jnp.cos(ang), jnp.sin(ang)


def rotate_half_axial(x):
    h = DIFF_D // 4
    xr = x.reshape(x.shape[:-1] + (2, 2, h))
    return jnp.concatenate([-xr[..., 1:2, :], xr[..., 0:1, :]], axis=-2).reshape(x.shape)


def apply_rope(x, cos, sin):
    cs = cos[:, None, None, :]
    sn = sin[:, None, None, :]
    return (x * cs + rotate_half_axial(x) * sn).astype(x.dtype)


def diff_attend(q, k, v, lam):
    s = jnp.einsum('bqhcd,bkhcd->bhcqk', q, k).astype(jnp.float32) * (DIFF_D ** -0.5)
    p = jax.nn.softmax(s, axis=-1)
    w = p[:, :, 0] - lam * p[:, :, 1]
    return jnp.einsum('bhqk,bkhv->bqhv', w.astype(v.dtype), v)


def diff_attn_mixer(u_ctx, u_lat, lam_vecs, subln_w, cos, sin, lam_init, with_ctx_out):
    def prep(u):
        b, t, _ = u.shape
        q, k, v, g = split_sizes(u, DIFF_SIZES)
        return (q.reshape(b, t, DIFF_HEADS, 2, DIFF_D), k.reshape(b, t, DIFF_HEADS, 2, DIFF_D),
                v.reshape(b, t, DIFF_HEADS, DIFF_DV), g)
    qc, kc, vc, gc = prep(u_ctx)
    ql, kl, vl, gl = prep(u_lat)
    lv = lam_vecs.astype(jnp.float32)
    lam = jnp.exp(jnp.sum(lv[0] * lv[1])) - jnp.exp(jnp.sum(lv[2] * lv[3])) + lam_init
    ql = apply_rope(ql, cos, sin)
    kl = apply_rope(kl, cos, sin)
    k_all = jnp.concatenate([kl, kc.astype(kl.dtype)], axis=1)
    v_all = jnp.concatenate([vl, vc.astype(vl.dtype)], axis=1)
    b, t = u_lat.shape[:2]
    nb = t // Q_BLOCK
    qb = jnp.moveaxis(ql.reshape(b, nb, Q_BLOCK, DIFF_HEADS, 2, DIFF_D), 1, 0)
    ol = lax.map(lambda qq: diff_attend(qq, k_all, v_all, lam), qb)
    ol = jnp.moveaxis(ol, 0, 1).reshape(b, t, DIFF_HEADS, DIFF_DV)

    def finish(o, g):
        bb, tt = g.shape[:2]
        y = (rmsnorm(o, subln_w) * (1.0 - lam_init)).reshape(bb, tt, BRANCH) * jax.nn.silu(g)
        return y.astype(g.dtype)
    yc = finish(diff_attend(qc, kc, vc, lam), gc) if with_ctx_out else None
    return yc, finish(ol, gl)


def ssd_chunked(x, dt, a, bm, cm, s0):
    b, t, h, p = x.shape
    n = t // SSD_CHUNK
    x = x.reshape(b, n, SSD_CHUNK, h, p)
    dt = dt.reshape(b, n, SSD_CHUNK, h)
    bm = bm.reshape(b, n, SSD_CHUNK, h, -1)
    cm = cm.reshape(b, n, SSD_CHUNK, h, -1)
    cum = jnp.cumsum(dt * a, axis=2)
    seg = cum[:, :, :, None, :] - cum[:, :, None, :, :]
    mask = jnp.tril(jnp.ones((SSD_CHUNK, SSD_CHUNK), dtype=bool))[:, :, None]
    decay_ls = jnp.exp(jnp.where(mask, seg, -jnp.inf))
    cb = jnp.einsum('bclhn,bcshn->bclsh', cm, bm)
    y = jnp.einsum('bclsh,bcsh,bcshp->bclhp', cb * decay_ls, dt, x)
    w_state = jnp.exp(cum[:, :, -1:] - cum) * dt
    upd = jnp.einsum('bcshn,bcsh,bcshp->cbhpn', bm, w_state, x)
    decay = jnp.exp(cum[:, :, -1]).transpose(1, 0, 2)[..., None, None]
    s_in, s_fin = chunk_state_scan(decay, upd, s0)
    y = y + jnp.einsum('bclhn,cbhpn,bclh->bclhp', cm, s_in, jnp.exp(cum))
    return y.reshape(b, t, h, p), s_fin


def ssd_dir(x, dt_raw, dt_bias, a_log, bm, cm, s0, reverse):
    dt = jax.nn.softplus(dt_raw.astype(jnp.float32) + dt_bias.astype(jnp.float32))
    a = -jnp.exp(a_log.astype(jnp.float32))
    if reverse:
        x, dt, bm, cm = (jnp.flip(z, axis=1) for z in (x, dt, bm, cm))
    y, s = ssd_chunked(x, dt, a, bm, cm, s0)
    if reverse:
        y = jnp.flip(y, axis=1)
    return y, s


def ssd_mixer(u_ctx, u_lat, conv_w, conv_b, dt_bias, a_log, d_skip, norm_w, with_ctx_out):
    rep = SSD_HEADS // SSD_GROUPS

    def prep(u):
        b, t, _ = u.shape
        xbc, dt_f, dt_b, z = split_sizes(u, SSD_SIZES)
        xbc = jax.nn.silu(dwconv(xbc, conv_w, conv_b))
        xs, bm, cm = split_sizes(xbc, (SSD_HEADS * SSD_P, SSD_GROUPS * SSD_N, SSD_GROUPS * SSD_N))
        xs = xs.reshape(b, t, SSD_HEADS, SSD_P)
        bm = jnp.repeat(bm.reshape(b, t, SSD_GROUPS, SSD_N), rep, axis=2)
        cm = jnp.repeat(cm.reshape(b, t, SSD_GROUPS, SSD_N), rep, axis=2)
        return xs, bm, cm, dt_f, dt_b, z
    xc, bc, cc, dcf, dcb, zc = prep(u_ctx)
    xl, bl, cl, dlf, dlb, zl = prep(u_lat)
    s0 = jnp.zeros((u_ctx.shape[0], SSD_HEADS, SSD_P, SSD_N), jnp.float32)
    ycf, scf = ssd_dir(xc, dcf, dt_bias[0], a_log[0], bc, cc, s0, False)
    ycb, scb = ssd_dir(xc, dcb, dt_bias[1], a_log[1], bc, cc, s0, True)
    ylf, _ = ssd_dir(xl, dlf, dt_bias[0], a_log[0], bl, cl, scf, False)
    ylb, _ = ssd_dir(xl, dlb, dt_bias[1], a_log[1], bl, cl, scb, True)

    def finish(yf, yb, xs, z):
        b, t = z.shape[:2]
        y = (yf + yb + d_skip[:, None] * xs).reshape(b, t, BRANCH)
        return rmsnorm(y.astype(z.dtype) * jax.nn.silu(z), norm_w)
    yc = finish(ycf, ycb, xc, zc) if with_ctx_out else None
    return yc, finish(ylf, ylb, xl, zl)


def setup_inputs(seed: int = 0) -> dict:
    key = jax.random.key(seed)
    ks = iter(jax.random.split(key, 40))
    f32 = jnp.float32

    def nrm(shape, scale):
        return jax.random.normal(next(ks), shape, f32) * scale

    L = DEPTH
    D = D_MODEL
    x = nrm((BATCH, SEQ, D), 1.0)
    c = nrm((BATCH, D), 1.0)
    ctx = nrm((BATCH, CTX_LEN, D), 1.0)
    c_ctx = nrm((D,), 1.0)
    w_mod = nrm((L, D, 3 * D), 0.3 * D ** -0.5)
    b_mod = nrm((L, 3 * D), 0.02)
    norm_w = 1.0 + nrm((L, D), 0.02)
    w_in = nrm((L, D, D_IN), D ** -0.5)
    w_out = nrm((L, D_MIX, D), D_MIX ** -0.5)
    gla_w2 = nrm((L, 2, GLA_GATE_RANK, GLA_HEADS * GLA_DK), GLA_GATE_RANK ** -0.5)
    gla_b2 = nrm((L, 2, GLA_HEADS * GLA_DK), 0.1)
    gla_norm_w = 1.0 + nrm((L, GLA_DV), 0.02)
    lru_conv_w = nrm((L, CONV_W, BRANCH), 0.5)
    lru_conv_b = nrm((L, BRANCH), 0.02)
    lru_wa = nrm((L, 2, LRU_BLOCKS, LRU_BLOCK_W, LRU_BLOCK_W), LRU_BLOCK_W ** -0.5)
    lru_ba = nrm((L, 2, BRANCH), 0.02)
    lru_wx = nrm((L, 2, LRU_BLOCKS, LRU_BLOCK_W, LRU_BLOCK_W), LRU_BLOCK_W ** -0.5)
    lru_bx = nrm((L, 2, BRANCH), 0.02)
    a_init = jax.random.uniform(next(ks), (L, 2, BRANCH), f32, 0.9, 0.999)
    lru_lam = jnp.log(a_init) - jnp.log1p(-a_init)
    diff_lam = nrm((L, 4, DIFF_D), 0.1)
    diff_subln_w = 1.0 + nrm((L, DIFF_DV), 0.02)
    ssd_conv_w = nrm((L, CONV_W, SSD_CONV_DIM), 0.5)
    ssd_conv_b = nrm((L, SSD_CONV_DIM), 0.02)
    dt0 = jnp.exp(jax.random.uniform(next(ks), (L, 2, SSD_HEADS), f32, math.log(1e-3), math.log(1e-1)))
    ssd_dt_bias = dt0 + jnp.log(-jnp.expm1(-dt0))
    ssd_a_log = jnp.log(jax.random.uniform(next(ks), (L, 2, SSD_HEADS), f32, 1.0, 16.0))
    ssd_d = 1.0 + nrm((L, SSD_HEADS), 0.1)
    ssd_norm_w = 1.0 + nrm((L, BRANCH), 0.02)
    final_norm_w = 1.0 + nrm((D,), 0.02)
    return {'x': x, 'c': c, 'ctx': ctx, 'c_ctx': c_ctx, 'w_mod': w_mod, 'b_mod': b_mod, 'norm_w': norm_w,
            'w_in': w_in, 'w_out': w_out, 'gla_w2': gla_w2, 'gla_b2': gla_b2, 'gla_norm_w': gla_norm_w,
            'lru_conv_w': lru_conv_w, 'lru_conv_b': lru_conv_b, 'lru_wa': lru_wa, 'lru_ba': lru_ba,
            'lru_wx': lru_wx, 'lru_bx': lru_bx, 'lru_lam': lru_lam, 'diff_lam': diff_lam,
            'diff_subln_w': diff_subln_w, 'ssd_conv_w': ssd_conv_w, 'ssd_conv_b': ssd_conv_b,
            'ssd_dt_bias': ssd_dt_bias, 'ssd_a_log': ssd_a_log, 'ssd_d': ssd_d, 'ssd_norm_w': ssd_norm_w,
            'final_norm_w': final_norm_w}


def reference(x, c, ctx, c_ctx, w_mod, b_mod, norm_w, w_in, w_out, gla_w2, gla_b2, gla_norm_w,
              lru_conv_w, lru_conv_b, lru_wa, lru_ba, lru_wx, lru_bx, lru_lam, diff_lam, diff_subln_w,
              ssd_conv_w, ssd_conv_b, ssd_dt_bias, ssd_a_log, ssd_d, ssd_norm_w, final_norm_w):
    rows = x.shape[1] // GRID_W
    cos, sin = axial_rope_tables(rows)
    h_lat, h_ctx = x, ctx
    silu_c = jax.nn.silu(c)
    silu_cc = jax.nn.silu(c_ctx)
    for l in range(DEPTH):
        last = l == DEPTH - 1
        lam_init = 0.8 - 0.6 * math.exp(-0.3 * l)
        mod_lat = (silu_c @ w_mod[l] + b_mod[l])[:, None, :]
        mod_ctx = silu_cc @ w_mod[l] + b_mod[l]
        shift_l, scale_l, gate_l = jnp.split(mod_lat, 3, axis=-1)
        shift_c, scale_c, gate_c = jnp.split(mod_ctx, 3, axis=-1)
        u_lat = (rmsnorm(h_lat, norm_w[l]) * (1.0 + scale_l) + shift_l) @ w_in[l]
        u_ctx = (rmsnorm(h_ctx, norm_w[l]) * (1.0 + scale_c) + shift_c) @ w_in[l]
        ga_c, gb_c, gc_c, gd_c = split_sizes(u_ctx, GROUP_SIZES)
        ga_l, gb_l, gc_l, gd_l = split_sizes(u_lat, GROUP_SIZES)
        ya_c, ya_l = gla_mixer(ga_c, ga_l, gla_w2[l], gla_b2[l], gla_norm_w[l], not last)
        yb_c, yb_l = rglru_mixer(gb_c, gb_l, lru_conv_w[l], lru_conv_b[l], lru_wa[l], lru_ba[l],
                                 lru_wx[l], lru_bx[l], lru_lam[l], not last)
        yc_c, yc_l = diff_attn_mixer(gc_c, gc_l, diff_lam[l], diff_subln_w[l], cos, sin, lam_init, not last)
        yd_c, yd_l = ssd_mixer(gd_c, gd_l, ssd_conv_w[l], ssd_conv_b[l], ssd_dt_bias[l], ssd_a_log[l],
                               ssd_d[l], ssd_norm_w[l], not last)
        y_lat = jnp.concatenate([ya_l, yb_l, yc_l, yd_l], axis=-1) @ w_out[l]
        h_lat = h_lat + gate_l * y_lat
        if not last:
            y_ctx = jnp.concatenate([ya_c, yb_c, yc_c, yd_c], axis=-1) @ w_out[l]
            h_ctx = h_ctx + gate_c * y_ctx
    return rmsnorm(h_lat, final_norm_w)
```

```python
import functools
import math

import jax
import jax.numpy as jnp
from jax import lax
from jax.experimental import pallas as pl
from jax.experimental.pallas import tpu as pltpu

F32 = jnp.float32
BF16 = jnp.bfloat16

D_MODEL = 1024
BRANCH = D_MODEL // 4
GRID_W = 64
EPS = 1e-6
CONV_W = 4
BLK = 256
HALO = 8
GLA_HEADS, GLA_DK, GLA_DV = 4, 32, 64
GLA_RANK = 16
GLA_TAU = 16.0
GLA_CHUNK = 64
LRU_C = 8.0
DIFF_HEADS, DIFF_D, DIFF_DV = 4, 32, 64
ROPE_BASE = 10000.0
SSD_HEADS, SSD_P, SSD_N, SSD_GROUPS = 4, 64, 64, 2
SSD_CHUNK = 128
ATT_TK = 768
W_GLA, W_LRU, W_DIFF, W_SSD = 896, 512, 1024, 896
W_ALL = W_GLA + W_LRU + W_DIFF + W_SSD
VMEM_LIMIT = 56 * 1024 * 1024

_NT = (((1,), (1,)), ((), ()))
_TN = (((0,), (0,)), ((), ()))


def _dot(a, b):
    return jnp.dot(a, b, preferred_element_type=F32)


def _dot_nt(a, b):
    return lax.dot_general(a, b, _NT, preferred_element_type=F32)


def _dot_tn(a, b):
    return lax.dot_general(a, b, _TN, preferred_element_type=F32)


def _split(x, terms):
    out = []
    r = x
    for _ in range(terms):
        p = r.astype(BF16)
        out.append(p)
        r = r - p.astype(F32)
    return out


def _dot_exact_left(m, x, terms=3):
    return sum(_dot(m, p) for p in _split(x, terms))


def _dot_exact_right(x, m, terms=3):
    return sum(_dot(p, m) for p in _split(x, terms))


def _iota(shape, dim):
    return lax.broadcasted_iota(jnp.int32, shape, dim)


def _silu(x):
    return x * jax.nn.sigmoid(x)


def _params(sem):
    return pltpu.CompilerParams(dimension_semantics=sem, vmem_limit_bytes=VMEM_LIMIT)


def _bwd_block(i, nblk):
    return jnp.where(i == 0, 0, nblk - i)


def _mod_body(c_ref, w_ref, b_ref, o_ref):
    s = _silu(c_ref[...])
    o_ref[0] = _dot(s.astype(BF16), w_ref[0].astype(BF16)) + b_ref[0]


def _mod_call(cvec, w_mod, b_mod):
    depth, d, n = w_mod.shape
    tn = 1024
    return pl.pallas_call(
        _mod_body,
        grid=(depth, n // tn),
        in_specs=[pl.BlockSpec((8, d), lambda l, j: (0, 0)),
                  pl.BlockSpec((1, d, tn), lambda l, j: (l, 0, j)),
                  pl.BlockSpec((1, 1, tn), lambda l, j: (l, 0, j))],
        out_specs=pl.BlockSpec((1, 8, tn), lambda l, j: (l, 0, j)),
        out_shape=jax.ShapeDtypeStruct((depth, 8, n), F32),
        compiler_params=_params(("arbitrary", "arbitrary")),
        name="adaln_mod",
    )(cvec, w_mod, b_mod.reshape(depth, 1, n))


def _inproj_body(h_ref, nw_ref, sc_ref, sh_ref, w_ref, oa_ref, ob_ref, oc_ref, od_ref):
    x = h_ref[0]
    ms = jnp.mean(x * x, axis=-1, keepdims=True)
    xn = x * lax.rsqrt(ms + EPS) * nw_ref[...]
    xm = xn * (1.0 + sc_ref[0, 0]) + sh_ref[0, 0]
    u = _dot(xm.astype(BF16), w_ref[...])
    oa_ref[0] = u[:, 0:W_GLA]
    ob_ref[0] = u[:, W_GLA:W_GLA + W_LRU]
    oc_ref[0] = u[:, W_GLA + W_LRU:W_GLA + W_LRU + W_DIFF]
    od_ref[0] = u[:, W_GLA + W_LRU + W_DIFF:W_ALL]


def _inproj_call(h, nw, scale, shift, w):
    b, rows, d = h.shape
    nblk = rows // BLK
    row_spec = pl.BlockSpec((1, 1, 1, d), lambda bb, i: (bb, jnp.minimum(i, 1), 0, 0))
    outs = [jax.ShapeDtypeStruct((b, rows, n), F32) for n in (W_GLA, W_LRU, W_DIFF, W_SSD)]
    return pl.pallas_call(
        _inproj_body,
        grid=(b, nblk),
        in_specs=[pl.BlockSpec((1, BLK, d), lambda bb, i: (bb, i, 0)),
                  pl.BlockSpec((1, d), lambda bb, i: (0, 0)),
                  row_spec, row_spec,
                  pl.BlockSpec((d, W_ALL), lambda bb, i: (0, 0))],
        out_specs=[pl.BlockSpec((1, BLK, n), lambda bb, i: (bb, i, 0)) for n in (W_GLA, W_LRU, W_DIFF, W_SSD)],
        out_shape=outs,
        compiler_params=_params(("arbitrary", "arbitrary")),
        name="norm_inproj",
    )(h, nw, scale, shift, w)


def _outproj_body(ya_ref, yb_ref, yc_ref, yd_ref, h_ref, g_ref, w_ref, fw_ref, o_ref, *, final):
    y = jnp.concatenate([ya_ref[0], yb_ref[0], yc_ref[0], yd_ref[0]], axis=-1)
    z = _dot(y.astype(BF16), w_ref[...])
    hn = h_ref[0] + g_ref[0, 0] * z
    if final:
        ms = jnp.mean(hn * hn, axis=-1, keepdims=True)
        hn = hn * lax.rsqrt(ms + EPS) * fw_ref[...]
    o_ref[0] = hn


def _outproj_call(ys, h, gate, w, fw, final):
    b, rows, d = h.shape
    nblk = rows // BLK
    off = 1 if final else 0
    nout = nblk - off
    y_spec = pl.BlockSpec((1, BLK, BRANCH), lambda bb, i: (bb, i + off, 0))
    return pl.pallas_call(
        functools.partial(_outproj_body, final=final),
        grid=(b, nout),
        in_specs=[y_spec, y_spec, y_spec, y_spec,
                  pl.BlockSpec((1, BLK, d), lambda bb, i: (bb, i + off, 0)),
                  pl.BlockSpec((1, 1, 1, d), lambda bb, i: (bb, jnp.minimum(i + off, 1), 0, 0)),
                  pl.BlockSpec((d, d), lambda bb, i: (0, 0)),
                  pl.BlockSpec((1, d), lambda bb, i: (0, 0))],
        out_specs=pl.BlockSpec((1, BLK, d), lambda bb, i: (bb, i, 0)),
        out_shape=jax.ShapeDtypeStruct((b, nout * BLK, d), F32),
        compiler_params=_params(("arbitrary", "arbitrary")),
        name="outproj_final" if final else "outproj",
    )(*ys, h, gate, w, fw)


def _pass_specs(nblk, reverse, width):
    def blk_of(i):
        return _bwd_block(i, nblk) if reverse else i
    per_blk = BLK // HALO
    main = lambda bb, i: (bb, blk_of(i), 0)
    prev = pl.BlockSpec((1, HALO, width), lambda bb, i: (bb, jnp.maximum(blk_of(i) * per_blk - 1, 0), 0))
    nxt = pl.BlockSpec((1, HALO, width),
                       lambda bb, i: (bb, jnp.minimum((blk_of(i) + 1) * per_blk, nblk * per_blk - 1), 0))
    return blk_of, main, prev, nxt


def _short_conv(xe_ref, x, prev, nxt, blk, nblk, cw_ref, cb_ref):
    prev_ok = blk >= 2
    next_ok = jnp.logical_and(blk >= 1, blk <= nblk - 2)
    xe_ref[0:HALO, :] = jnp.where(prev_ok, prev, 0.0)
    xe_ref[HALO:HALO + BLK, :] = x
    xe_ref[HALO + BLK:2 * HALO + BLK, :] = jnp.where(next_ok, nxt, 0.0)
    left = CONV_W // 2
    y = cb_ref[...] + xe_ref[HALO - left:HALO - left + BLK, :] * cw_ref[0:1, :]
    for j in range(1, CONV_W):
        y = y + xe_ref[HALO - left + j:HALO - left + j + BLK, :] * cw_ref[j:j + 1, :]
    return y


def _gla_body(*refs, reverse, nblk):
    if reverse:
        u_ref, w2_ref, b2_ref, of_ref, nw_ref, o_ref, st_ref = refs
    else:
        u_ref, w2_ref, b2_ref, o_ref, st_ref = refs
    i = pl.program_id(1)

    @pl.when(i == 0)
    def _():
        st_ref[...] = jnp.zeros_like(st_ref)

    ub = u_ref[0]
    q = ub[:, 0:128] * (GLA_DK ** -0.5)
    k = ub[:, 128:256]
    v = ub[:, 256:512].astype(BF16)
    lr = ub[:, 768:896]
    la = jax.nn.log_sigmoid(_dot(lr.astype(BF16), w2_ref[...]) + b2_ref[...]) * (1.0 / GLA_TAU)

    r = _iota((BLK, BLK), 0)
    c = _iota((BLK, BLK), 1)
    same = (r >> 6) == (c >> 6)
    tri = jnp.logical_and(same, (c >= r) if reverse else (c <= r))
    sel = jnp.concatenate([tri, same], axis=0).astype(BF16)
    gg = _dot_exact_left(sel, la)
    g = gg[0:BLK]
    gl = gg[BLK:2 * BLK]
    qg = q * jnp.exp(g)
    kg = (k * jnp.exp(-g)).astype(BF16)
    kd = (k * jnp.exp(gl - g)).astype(BF16)
    egl = jnp.exp(gl)

    lane_k = _iota((GLA_CHUNK, 128), 1) >> 5
    rr = _iota((GLA_HEADS * GLA_CHUNK, GLA_CHUNK), 0) & (GLA_CHUNK - 1)
    cc = _iota((GLA_HEADS * GLA_CHUNK, GLA_CHUNK), 1)
    amask = (cc >= rr) if reverse else (cc <= rr)
    lane_v = _iota((GLA_CHUNK, BRANCH), 1) >> 6
    bd = (_iota((BRANCH, 128), 0) >> 6) == (_iota((BRANCH, 128), 1) >> 5)

    n_chunks = BLK // GLA_CHUNK
    outs = [None] * n_chunks
    for ci in (reversed(range(n_chunks)) if reverse else range(n_chunks)):
        lo = ci * GLA_CHUNK
        qg_c = qg[lo:lo + GLA_CHUNK]
        v_c = v[lo:lo + GLA_CHUNK]
        qst = jnp.concatenate([jnp.where(lane_k == h, qg_c, 0.0) for h in range(GLA_HEADS)], axis=0)
        att = _dot_nt(qst.astype(BF16), kg[lo:lo + GLA_CHUNK])
        att = jnp.where(amask, att, 0.0)
        res = _dot(att.astype(BF16), v_c)
        o_c = jnp.zeros((GLA_CHUNK, BRANCH), F32)
        for h in range(GLA_HEADS):
            o_c = o_c + jnp.where(lane_v == h, res[h * GLA_CHUNK:(h + 1) * GLA_CHUNK], 0.0)
        st = st_ref[...]
        o_c = o_c + _dot_nt(qg_c.astype(BF16), st.astype(BF16))
        upd = _dot_tn(v_c, kd[lo:lo + GLA_CHUNK])
        st_ref[...] = st * egl[lo:lo + 1, :] + jnp.where(bd, upd, 0.0)
        outs[ci] = o_c
    o = jnp.concatenate(outs, axis=0)

    if reverse:
        o = o + of_ref[0]
        rv = _iota((BRANCH, BRANCH), 0) >> 6
        cv = _iota((BRANCH, BRANCH), 1) >> 6
        grp = (rv == cv).astype(BF16)
        ms = _dot_exact_right(o * o, grp, terms=2) * (1.0 / GLA_DV)
        y = o * lax.rsqrt(ms + EPS) * nw_ref[...]
        o_ref[0] = y * _silu(ub[:, 512:768])
    else:
        o_ref[0] = o


def _gla_call(u, w2, b2, reverse, o_fwd=None, nw=None):
    b, rows, _ = u.shape
    nblk = rows // BLK
    _, main, _, _ = _pass_specs(nblk, reverse, W_GLA)
    in_specs = [pl.BlockSpec((1, BLK, W_GLA), main),
                pl.BlockSpec((128, 128), lambda bb, i: (0, 0)),
                pl.BlockSpec((1, 128), lambda bb, i: (0, 0))]
    args = [u, w2, b2]
    if reverse:
        in_specs += [pl.BlockSpec((1, BLK, BRANCH), main), pl.BlockSpec((1, BRANCH), lambda bb, i: (0, 0))]
        args += [o_fwd, nw]
    return pl.pallas_call(
        functools.partial(_gla_body, reverse=reverse, nblk=nblk),
        grid=(b, nblk),
        in_specs=in_specs,
        out_specs=pl.BlockSpec((1, BLK, BRANCH), main),
        out_shape=jax.ShapeDtypeStruct((b, rows, BRANCH), F32),
        scratch_shapes=[pltpu.VMEM((BRANCH, 128), F32)],
        compiler_params=_params(("arbitrary", "arbitrary")),
        name="gla_bwd" if reverse else "gla_fwd",
    )(*args)


def _lru_body(*refs, reverse, nblk):
    if reverse:
        (u_ref, prev_ref, next_ref, cw_ref, cb_ref, wg_ref, bg_ref, lam_ref, hf_ref,
         o_ref, xe_ref, a_ref, s_ref, hb_ref, h_ref) = refs
    else:
        (u_ref, prev_ref, next_ref, cw_ref, cb_ref, wg_ref, bg_ref, lam_ref,
         o_ref, xe_ref, a_ref, s_ref, h_ref) = refs
    i = pl.program_id(1)
    blk = _bwd_block(i, nblk) if reverse else i

    @pl.when(i == 0)
    def _():
        h_ref[...] = jnp.zeros_like(h_ref)

    ub = u_ref[0]
    xc = _short_conv(xe_ref, ub[:, 0:BRANCH], prev_ref[0][:, 0:BRANCH], next_ref[0][:, 0:BRANCH],
                     blk, nblk, cw_ref, cb_ref)
    z = _dot(xc.astype(BF16), wg_ref[...]) + bg_ref[...]
    rg = jax.nn.sigmoid(z[:, 0:BRANCH])
    ig = jax.nn.sigmoid(z[:, BRANCH:2 * BRANCH])
    log_a = -LRU_C * rg * jax.nn.softplus(-lam_ref[...])
    a = jnp.exp(log_a)
    a_ref[...] = a
    s_ref[...] = jnp.sqrt(jnp.tanh(-log_a) * (1.0 + a * a)) * (ig * xc)

    dst = hb_ref if reverse else o_ref

    def group(gi, h):
        base = pl.multiple_of((BLK // 8 - 1 - gi if reverse else gi) * 8, 8)
        for rr in range(8):
            t = base + (7 - rr if reverse else rr)
            h = a_ref[pl.ds(t, 1), :] * h + s_ref[pl.ds(t, 1), :]
            if reverse:
                dst[pl.ds(t, 1), :] = h
            else:
                dst[0, pl.ds(t, 1), :] = h
        return h

    h_ref[...] = lax.fori_loop(0, BLK // 8, group, h_ref[...])
    if reverse:
        o_ref[0] = (hf_ref[0] + hb_ref[...]) * _silu(ub[:, BRANCH:2 * BRANCH])


def _lru_call(u, cw, cb, wg, bg, lam, reverse, h_fwd=None):
    b, rows, _ = u.shape
    nblk = rows // BLK
    _, main, prev, nxt = _pass_specs(nblk, reverse, W_LRU)
    const = lambda shape: pl.BlockSpec(shape, lambda bb, i: tuple(0 for _ in shape))
    in_specs = [pl.BlockSpec((1, BLK, W_LRU), main), prev, nxt,
                const((CONV_W, BRANCH)), const((1, BRANCH)), const((BRANCH, 2 * BRANCH)),
                const((1, 2 * BRANCH)), const((1, BRANCH))]
    args = [u, u, u, cw, cb, wg, bg, lam]
    scratch = [pltpu.VMEM((BLK + 2 * HALO, BRANCH), F32), pltpu.VMEM((BLK, BRANCH), F32),
               pltpu.VMEM((BLK, BRANCH), F32)]
    if reverse:
        in_specs.append(pl.BlockSpec((1, BLK, BRANCH), main))
        args.append(h_fwd)
        scratch.append(pltpu.VMEM((BLK, BRANCH), F32))
    scratch.append(pltpu.VMEM((1, BRANCH), F32))
    return pl.pallas_call(
        functools.partial(_lru_body, reverse=reverse, nblk=nblk),
        grid=(b, nblk),
        in_specs=in_specs,
        out_specs=pl.BlockSpec((1, BLK, BRANCH), main),
        out_shape=jax.ShapeDtypeStruct((b, rows, BRANCH), F32),
        scratch_shapes=scratch,
        compiler_params=_params(("arbitrary", "arbitrary")),
        name="lru_bwd" if reverse else "lru_fwd",
    )(*args)


def _ssd_body(*refs, reverse, nblk):
    if reverse:
        (u_ref, prev_ref, next_ref, cw_ref, cb_ref, dtb_ref, an_ref, yf_ref, dsk_ref, nw_ref,
         o_ref, xe_ref, st_ref) = refs
    else:
        u_ref, prev_ref, next_ref, cw_ref, cb_ref, dtb_ref, an_ref, o_ref, xe_ref, st_ref = refs
    i = pl.program_id(1)
    blk = _bwd_block(i, nblk) if reverse else i

    @pl.when(i == 0)
    def _():
        st_ref[...] = jnp.zeros_like(st_ref)

    ub = u_ref[0]
    xc = _silu(_short_conv(xe_ref, ub[:, 0:512], prev_ref[0][:, 0:512], next_ref[0][:, 0:512],
                           blk, nblk, cw_ref, cb_ref))
    xs = xc[:, 0:256]
    bm = xc[:, 256:384].astype(BF16)
    cm = xc[:, 384:512]
    dt = jax.nn.softplus(ub[:, 768:896] + dtb_ref[...])
    da = dt * an_ref[...]
    off = SSD_HEADS if reverse else 0
    ch = SSD_CHUNK

    r = _iota((ch, ch), 0)
    c = _iota((ch, ch), 1)
    keep = (c >= r) if reverse else (c <= r)
    tri = keep.astype(BF16)
    expand = (_iota((128, BRANCH), 0) == off + (_iota((128, BRANCH), 1) >> 6)).astype(BF16)
    lane_n = _iota((ch, 128), 1) >> 6
    lane_p = _iota((ch, BRANCH), 1) >> 6
    bd = (_iota((128, BRANCH), 0) >> 6) == (_iota((128, BRANCH), 1) >> 7)
    edge = 0 if reverse else ch - 1

    n_chunks = BLK // ch
    outs = [None] * n_chunks
    for ci in (reversed(range(n_chunks)) if reverse else range(n_chunks)):
        lo = ci * ch
        xs_c = xs[lo:lo + ch]
        bm_c = bm[lo:lo + ch]
        cm_c = cm[lo:lo + ch]
        dt_c = dt[lo:lo + ch]
        cum = _dot_exact_left(tri, da[lo:lo + ch])
        cum_t = cum.T
        dt_t = dt_c.T
        cum_x = _dot_exact_right(cum, expand)
        dt_x = _dot_exact_right(dt_c, expand, terms=2)
        tot_x = cum_x[edge:edge + 1, :]
        y_c = jnp.zeros((ch, BRANCH), F32)
        for g in range(SSD_GROUPS):
            cb = _dot_nt(jnp.where(lane_n == g, cm_c, 0.0).astype(BF16), bm_c)
            for h in range(g * (SSD_HEADS // SSD_GROUPS), (g + 1) * (SSD_HEADS // SSD_GROUPS)):
                seg = cum[:, off + h:off + h + 1] - cum_t[off + h:off + h + 1, :]
                lm = jnp.exp(jnp.where(keep, seg, -1e30))
                m = cb * lm * dt_t[off + h:off + h + 1, :]
                res = _dot(m.astype(BF16), xs_c.astype(BF16))
                y_c = y_c + jnp.where(lane_p == h, res, 0.0)
        st = st_ref[...]
        y_c = y_c + _dot(cm_c.astype(BF16), st.astype(BF16)) * jnp.exp(cum_x)
        xw = xs_c * (jnp.exp(tot_x - cum_x) * dt_x)
        upd = _dot_tn(bm_c, xw.astype(BF16))
        st_ref[...] = st * jnp.exp(tot_x) + jnp.where(bd, upd, 0.0)
        outs[ci] = y_c
    y = jnp.concatenate(outs, axis=0)

    if reverse:
        y = y + yf_ref[0] + dsk_ref[...] * xs
        yz = y * _silu(ub[:, 512:768])
        ms = jnp.mean(yz * yz, axis=-1, keepdims=True)
        o_ref[0] = yz * lax.rsqrt(ms + EPS) * nw_ref[...]
    else:
        o_ref[0] = y


def _ssd_call(u, cw, cb, dtb, aneg, reverse, y_fwd=None, dskip=None, nw=None):
    b, rows, _ = u.shape
    nblk = rows // BLK
    _, main, prev, nxt = _pass_specs(nblk, reverse, W_SSD)
    const = lambda shape: pl.BlockSpec(shape, lambda bb, i: tuple(0 for _ in shape))
    in_specs = [pl.BlockSpec((1, BLK, W_SSD), main), prev, nxt,
                const((CONV_W, 512)), const((1, 512)), const((1, 128)), const((1, 128))]
    args = [u, u, u, cw, cb, dtb, aneg]
    if reverse:
        in_specs += [pl.BlockSpec((1, BLK, BRANCH), main), const((1, BRANCH)), const((1, BRANCH))]
        args += [y_fwd, dskip, nw]
    return pl.pallas_call(
        functools.partial(_ssd_body, reverse=reverse, nblk=nblk),
        grid=(b, nblk),
        in_specs=in_specs,
        out_specs=pl.BlockSpec((1, BLK, BRANCH), main),
        out_shape=jax.ShapeDtypeStruct((b, rows, BRANCH), F32),
        scratch_shapes=[pltpu.VMEM((BLK + 2 * HALO, 512), F32), pltpu.VMEM((128, BRANCH), F32)],
        compiler_params=_params(("arbitrary", "arbitrary")),
        name="ssd_bwd" if reverse else "ssd_fwd",
    )(*args)


def _attn_prep_body(u_ref, cos_ref, sin_ref, q_ref, k_ref, vt_ref):
    i = pl.program_id(1)
    ub = u_ref[0]
    is_ctx = i == 0
    cs = jnp.where(is_ctx, 1.0, cos_ref[...])
    sn = jnp.where(is_ctx, 0.0, sin_ref[...])
    first = (_iota((BLK, BRANCH), 1) & 15) < 8

    def rope(x):
        rot = jnp.where(first, pltpu.roll(x, BRANCH - 8, 1), pltpu.roll(x, 8, 1))
        return x * cs + rot * sn

    q_ref[0] = (rope(ub[:, 0:256]) * (DIFF_D ** -0.5 * math.log2(math.e))).astype(BF16)
    k_ref[0] = rope(ub[:, 256:512]).astype(BF16)
    vt_ref[0, 0] = ub[:, 512:768].T.astype(BF16)


def _attn_prep_call(u, cos, sin):
    b, rows, _ = u.shape
    nblk = rows // BLK
    tab = pl.BlockSpec((BLK, BRANCH), lambda bb, i: (jnp.maximum(i - 1, 0), 0))
    return pl.pallas_call(
        _attn_prep_body,
        grid=(b, nblk),
        in_specs=[pl.BlockSpec((1, BLK, W_DIFF), lambda bb, i: (bb, i, 0)), tab, tab],
        out_specs=[pl.BlockSpec((1, BLK, BRANCH), lambda bb, i: (bb, i, 0)),
                   pl.BlockSpec((1, BLK, BRANCH), lambda bb, i: (bb, i, 0)),
                   pl.BlockSpec((1, 1, BRANCH, BLK), lambda bb, i: (bb, i, 0, 0))],
        out_shape=[jax.ShapeDtypeStruct((b, rows, BRANCH), BF16),
                   jax.ShapeDtypeStruct((b, rows, BRANCH), BF16),
                   jax.ShapeDtypeStruct((b, nblk, BRANCH, BLK), BF16)],
        compiler_params=_params(("arbitrary", "arbitrary")),
        name="attn_prep",
    )(u, cos, sin)


def _attn_body(q_ref, k_ref, vt_ref, g_ref, lam_ref, sw_ref, o_ref, s_ref, *, nk, lam_init):
    tk = min(ATT_TK, nk)
    n_steps = nk // tk
    sub = tk // BLK
    q = q_ref[0]
    lv = lam_ref[...]
    lam = (jnp.exp(jnp.sum(lv[0:1] * lv[1:2], axis=-1, keepdims=True))
           - jnp.exp(jnp.sum(lv[2:3] * lv[3:4], axis=-1, keepdims=True)) + lam_init)
    lane = _iota((BLK, 128), 1)
    heads = []
    for h in range(DIFF_HEADS):
        tile = h // 2
        q_t = q[:, tile * 128:(tile + 1) * 128]
        halves = []
        for half in range(2):
            lo = (h % 2) * 64 + half * DIFF_D
            qm = jnp.where(jnp.logical_and(lane >= lo, lane < lo + DIFF_D), q_t, jnp.zeros_like(q_t))

            def scores(j, m8, tile=tile, qm=qm):
                start = pl.multiple_of(j * tk, BLK)
                s = _dot_nt(k_ref[0, pl.ds(start, tk), tile * 128:(tile + 1) * 128], qm)
                s_ref[pl.ds(start, tk), :] = s
                return jnp.maximum(m8, jnp.max(s.reshape(tk // 8, 8, BLK), axis=0))

            m8 = lax.fori_loop(0, n_steps, scores, jnp.full((8, BLK), -jnp.inf, F32))
            m = jnp.max(m8, axis=0, keepdims=True)

            def weigh(j, carry, h=h, m=m):
                l8, acc = carry
                start = pl.multiple_of(j * tk, BLK)
                p = jnp.exp2(s_ref[pl.ds(start, tk), :] - m)
                l8 = l8 + jnp.sum(p.reshape(tk // 8, 8, BLK), axis=0)
                pb = p.astype(BF16)
                for jj in range(sub):
                    acc = acc + _dot(vt_ref[0, j * sub + jj, h * DIFF_DV:(h + 1) * DIFF_DV, :],
                                     pb[jj * BLK:(jj + 1) * BLK])
                return l8, acc

            l8, acc = lax.fori_loop(0, n_steps, weigh,
                                    (jnp.zeros((8, BLK), F32), jnp.zeros((DIFF_DV, BLK), F32)))
            halves.append(acc / jnp.sum(l8, axis=0, keepdims=True))
        heads.append(halves[0] - lam * halves[1])
    o = jnp.concatenate(heads, axis=0).T
    grp = ((_iota((BRANCH, BRANCH), 0) >> 6) == (_iota((BRANCH, BRANCH), 1) >> 6)).astype(BF16)
    ms = _dot_exact_right(o * o, grp, terms=2) * (1.0 / DIFF_DV)
    y = o * lax.rsqrt(ms + EPS) * sw_ref[...] * (1.0 - lam_init)
    o_ref[0] = y * _silu(g_ref[0])


def _attn_call(q, k, vt, u, lam_vecs, sw, lam_init, ctx_only):
    b, rows, _ = q.shape
    nblk = rows // BLK
    nk = BLK if ctx_only else rows
    nq = 1 if ctx_only else nblk - 1
    qoff = 0 if ctx_only else 1
    const = lambda shape: pl.BlockSpec(shape, lambda bb, i: tuple(0 for _ in shape))
    return pl.pallas_call(
        functools.partial(_attn_body, nk=nk, lam_init=lam_init),
        grid=(b, nq),
        in_specs=[pl.BlockSpec((1, BLK, BRANCH), lambda bb, i: (bb, i + qoff, 0)),
                  pl.BlockSpec((1, nk, BRANCH), lambda bb, i: (bb, 0, 0)),
                  pl.BlockSpec((1, nk // BLK, BRANCH, BLK), lambda bb, i: (bb, 0, 0, 0)),
                  pl.BlockSpec((1, BLK, BRANCH), lambda bb, i: (bb, i + qoff, 3)),
                  const((4, DIFF_D)), const((1, BRANCH))],
        out_specs=pl.BlockSpec((1, BLK, BRANCH), lambda bb, i: (bb, i, 0)),
        out_shape=jax.ShapeDtypeStruct((b, nq * BLK, BRANCH), F32),
        scratch_shapes=[pltpu.VMEM((nk, BLK), F32)],
        compiler_params=_params(("arbitrary", "arbitrary")),
        name="diff_attn_ctx" if ctx_only else "diff_attn",
    )(q, k, vt, u, lam_vecs, sw)


def _pad_cols(parts, width):
    n = sum(p.shape[-1] for p in parts)
    if n < width:
        parts = parts + [jnp.zeros(parts[0].shape[:-1] + (width - n,), parts[0].dtype)]
    return jnp.concatenate(parts, axis=-1)


def _layout_w_in(w):
    a, b_, c_, d_ = 0, 800, 1312, 2336
    gla = _pad_cols([w[:, a:a + 512], w[:, a + 544:a + 800], w[:, a + 512:a + 544]], W_GLA)
    lru = w[:, b_:b_ + 512]
    diff = w[:, c_:c_ + 1024]
    ssd = _pad_cols([w[:, d_:d_ + 512], w[:, d_ + 520:d_ + 776], w[:, d_ + 512:d_ + 520]], W_SSD)
    return jnp.concatenate([gla, lru, diff, ssd], axis=-1).astype(BF16)


def _block_diag(w):
    nb, n, _ = w.shape
    eye = jnp.eye(nb, dtype=w.dtype)
    return (eye[:, None, :, None] * w[:, :, None, :]).reshape(nb * n, nb * n)


def _rope_tables(t_lat):
    n_freq = DIFF_D // 4
    inv = ROPE_BASE ** (-jnp.arange(n_freq, dtype=F32) / n_freq)
    tpos = jnp.arange(t_lat)
    ang_r = (tpos // GRID_W).astype(F32)[:, None] * inv
    ang_c = (tpos % GRID_W).astype(F32)[:, None] * inv
    ang = jnp.concatenate([ang_r, ang_r, ang_c, ang_c], axis=-1)
    reps = BRANCH // DIFF_D
    sign = jnp.where((jnp.arange(DIFF_D) % 16) < 8, -1.0, 1.0).astype(F32)
    return jnp.tile(jnp.cos(ang), (1, reps)), jnp.tile(jnp.sin(ang) * sign, (1, reps))


def kernel(x, c, ctx, c_ctx, w_mod, b_mod, norm_w, w_in, w_out, gla_w2, gla_b2, gla_norm_w, lru_conv_w, lru_conv_b, lru_wa, lru_ba, lru_wx, lru_bx, lru_lam, diff_lam, diff_subln_w, ssd_conv_w, ssd_conv_b, ssd_dt_bias, ssd_a_log, ssd_d, ssd_norm_w, final_norm_w):
    bsz, t_lat, d = x.shape
    depth = w_mod.shape[0]
    assert ctx.shape[1] == BLK and t_lat % BLK == 0 and d == D_MODEL and bsz + 1 <= 8
    assert (t_lat + BLK) % ATT_TK == 0

    h = jnp.concatenate([ctx, x], axis=1)
    cvec = jnp.zeros((8, d), F32).at[0:bsz].set(c).at[bsz].set(c_ctx)
    mod = _mod_call(cvec, w_mod, b_mod)
    cos, sin = _rope_tables(t_lat)

    for l in range(depth):
        last = l == depth - 1
        lam_init = 0.8 - 0.6 * math.exp(-0.3 * l)
        m_lat = mod[l, 0:bsz]
        m_ctx = jnp.broadcast_to(mod[l, bsz], (bsz, 3 * d))
        rows = jnp.stack([m_ctx, m_lat], axis=1)[:, :, None, :]
        shift, scale, gate = rows[..., 0:d], rows[..., d:2 * d], rows[..., 2 * d:3 * d]

        ua, ub, uc, ud = _inproj_call(h, norm_w[l][None], scale, shift, _layout_w_in(w_in[l]))

        w2 = jnp.zeros((2, 128, 128), F32)
        w2 = w2.at[0, 0:GLA_RANK].set(gla_w2[l, 0]).at[1, GLA_RANK:2 * GLA_RANK].set(gla_w2[l, 1]).astype(BF16)
        gnw = jnp.tile(gla_norm_w[l], GLA_HEADS)[None]
        of = _gla_call(ua, w2[0], gla_b2[l, 0][None], False)
        ya = _gla_call(ua, w2[1], gla_b2[l, 1][None], True, of, gnw)

        wg = [jnp.concatenate([_block_diag(lru_wa[l, dr]), _block_diag(lru_wx[l, dr])], axis=1).astype(BF16)
              for dr in range(2)]
        bg = [jnp.concatenate([lru_ba[l, dr], lru_bx[l, dr]])[None] for dr in range(2)]
        hf = _lru_call(ub, lru_conv_w[l], lru_conv_b[l][None], wg[0], bg[0], lru_lam[l, 0][None], False)
        yb = _lru_call(ub, lru_conv_w[l], lru_conv_b[l][None], wg[1], bg[1], lru_lam[l, 1][None], True, hf)

        qp, kp, vt = _attn_prep_call(uc, cos, sin)
        sw = jnp.tile(diff_subln_w[l], DIFF_HEADS)[None]
        yc_lat = _attn_call(qp, kp, vt, uc, diff_lam[l], sw, lam_init, False)
        if last:
            yc = jnp.concatenate([jnp.zeros((bsz, BLK, BRANCH), F32), yc_lat], axis=1)
        else:
            yc = jnp.concatenate([_attn_call(qp, kp, vt, uc, diff_lam[l], sw, lam_init, True), yc_lat], axis=1)

        dtb = jnp.zeros((1, 128), F32).at[0, 0:2 * SSD_HEADS].set(ssd_dt_bias[l].reshape(-1))
        aneg = jnp.zeros((1, 128), F32).at[0, 0:2 * SSD_HEADS].set(-jnp.exp(ssd_a_log[l].reshape(-1)))
        dsk = jnp.repeat(ssd_d[l], SSD_P)[None]
        yf = _ssd_call(ud, ssd_conv_w[l], ssd_conv_b[l][None], dtb, aneg, False)
        yd = _ssd_call(ud, ssd_conv_w[l], ssd_conv_b[l][None], dtb, aneg, True, yf, dsk, ssd_norm_w[l][None])

        h = _outproj_call([ya, yb, yc, yd], h, gate, w_out[l].astype(BF16), final_norm_w[None], last)
    return h
```

```python
import functools
import math

import jax
import jax.numpy as jnp
from jax import lax
from jax.experimental import pallas as pl
from jax.experimental.pallas import tpu as pltpu

F32 = jnp.float32
BF16 = jnp.bfloat16

D_MODEL = 1024
BRANCH = D_MODEL // 4
GRID_W = 64
EPS = 1e-6
CONV_W = 4
BLK = 256
HALO = 8
GLA_HEADS, GLA_DK, GLA_DV = 4, 32, 64
GLA_RANK = 16
GLA_TAU = 16.0
GLA_CHUNK = 64
LRU_C = 8.0
DIFF_HEADS, DIFF_D, DIFF_DV = 4, 32, 64
ROPE_BASE = 10000.0
SSD_HEADS, SSD_P, SSD_N, SSD_GROUPS = 4, 64, 64, 2
SSD_CHUNK = 128
ATT_TK = 2816
W_GLA, W_LRU, W_DIFF, W_SSD = 896, 512, 1024, 896
W_ALL = W_GLA + W_LRU + W_DIFF + W_SSD
VMEM_LIMIT = 56 * 1024 * 1024

_NT = (((1,), (1,)), ((), ()))
_TN = (((0,), (0,)), ((), ()))


def _dot(a, b):
    return jnp.dot(a, b, preferred_element_type=F32)


def _dot_nt(a, b):
    return lax.dot_general(a, b, _NT, preferred_element_type=F32)


def _dot_tn(a, b):
    return lax.dot_general(a, b, _TN, preferred_element_type=F32)


def _split(x, terms):
    out = []
    r = x
    for _ in range(terms):
        p = r.astype(BF16)
        out.append(p)
        r = r - p.astype(F32)
    return out


def _dot_exact_left(m, x, terms=3):
    return sum(_dot(m, p) for p in _split(x, terms))


def _dot_exact_right(x, m, terms=3):
    return sum(_dot(p, m) for p in _split(x, terms))


def _iota(shape, dim):
    return lax.broadcasted_iota(jnp.int32, shape, dim)


def _silu(x):
    return x * jax.nn.sigmoid(x)


def _params(sem):
    return pltpu.CompilerParams(dimension_semantics=sem, vmem_limit_bytes=VMEM_LIMIT)


def _bwd_block(i, nblk):
    return jnp.where(i == 0, 0, nblk - i)


def _mod_body(c_ref, w_ref, b_ref, o_ref):
    s = _silu(c_ref[...])
    o_ref[0] = _dot(s.astype(BF16), w_ref[0].astype(BF16)) + b_ref[0]


def _mod_call(cvec, w_mod, b_mod):
    depth, d, n = w_mod.shape
    tn = 1024
    return pl.pallas_call(
        _mod_body,
        grid=(depth, n // tn),
        in_specs=[pl.BlockSpec((8, d), lambda l, j: (0, 0)),
                  pl.BlockSpec((1, d, tn), lambda l, j: (l, 0, j)),
                  pl.BlockSpec((1, 1, tn), lambda l, j: (l, 0, j))],
        out_specs=pl.BlockSpec((1, 8, tn), lambda l, j: (l, 0, j)),
        out_shape=jax.ShapeDtypeStruct((depth, 8, n), F32),
        compiler_params=_params(("arbitrary", "arbitrary")),
        name="adaln_mod",
    )(cvec, w_mod, b_mod.reshape(depth, 1, n))


def _inproj_body(h_ref, nw_ref, sc_ref, sh_ref, w_ref, oa_ref, ob_ref, oc_ref, od_ref):
    x = h_ref[0]
    ms = jnp.mean(x * x, axis=-1, keepdims=True)
    xn = x * lax.rsqrt(ms + EPS) * nw_ref[...]
    xm = xn * (1.0 + sc_ref[0, 0]) + sh_ref[0, 0]
    u = _dot(xm.astype(BF16), w_ref[...])
    oa_ref[0] = u[:, 0:W_GLA]
    ob_ref[0] = u[:, W_GLA:W_GLA + W_LRU]
    oc_ref[0] = u[:, W_GLA + W_LRU:W_GLA + W_LRU + W_DIFF]
    od_ref[0] = u[:, W_GLA + W_LRU + W_DIFF:W_ALL]


def _inproj_call(h, nw, scale, shift, w):
    b, rows, d = h.shape
    nblk = rows // BLK
    row_spec = pl.BlockSpec((1, 1, 1, d), lambda bb, i: (bb, jnp.minimum(i, 1), 0, 0))
    outs = [jax.ShapeDtypeStruct((b, rows, n), F32) for n in (W_GLA, W_LRU, W_DIFF, W_SSD)]
    return pl.pallas_call(
        _inproj_body,
        grid=(b, nblk),
        in_specs=[pl.BlockSpec((1, BLK, d), lambda bb, i: (bb, i, 0)),
                  pl.BlockSpec((1, d), lambda bb, i: (0, 0)),
                  row_spec, row_spec,
                  pl.BlockSpec((d, W_ALL), lambda bb, i: (0, 0))],
        out_specs=[pl.BlockSpec((1, BLK, n), lambda bb, i: (bb, i, 0)) for n in (W_GLA, W_LRU, W_DIFF, W_SSD)],
        out_shape=outs,
        compiler_params=_params(("arbitrary", "arbitrary")),
        name="norm_inproj",
    )(h, nw, scale, shift, w)


def _outproj_body(ya_ref, yb_ref, yc_ref, yd_ref, h_ref, g_ref, w_ref, fw_ref, o_ref, *, final):
    y = jnp.concatenate([ya_ref[0], yb_ref[0], yc_ref[0], yd_ref[0]], axis=-1)
    z = _dot(y.astype(BF16), w_ref[...])
    hn = h_ref[0] + g_ref[0, 0] * z
    if final:
        ms = jnp.mean(hn * hn, axis=-1, keepdims=True)
        hn = hn * lax.rsqrt(ms + EPS) * fw_ref[...]
    o_ref[0] = hn


def _outproj_call(ys, h, gate, w, fw, final):
    b, rows, d = h.shape
    nblk = rows // BLK
    off = 1 if final else 0
    nout = nblk - off
    y_spec = pl.BlockSpec((1, BLK, BRANCH), lambda bb, i: (bb, i + off, 0))
    return pl.pallas_call(
        functools.partial(_outproj_body, final=final),
        grid=(b, nout),
        in_specs=[y_spec, y_spec, y_spec, y_spec,
                  pl.BlockSpec((1, BLK, d), lambda bb, i: (bb, i + off, 0)),
                  pl.BlockSpec((1, 1, 1, d), lambda bb, i: (bb, jnp.minimum(i + off, 1), 0, 0)),
                  pl.BlockSpec((d, d), lambda bb, i: (0, 0)),
                  pl.BlockSpec((1, d), lambda bb, i: (0, 0))],
        out_specs=pl.BlockSpec((1, BLK, d), lambda bb, i: (bb, i, 0)),
        out_shape=jax.ShapeDtypeStruct((b, nout * BLK, d), F32),
        compiler_params=_params(("arbitrary", "arbitrary")),
        name="outproj_final" if final else "outproj",
    )(*ys, h, gate, w, fw)


def _pass_specs(nblk, reverse, width):
    def blk_of(i):
        return _bwd_block(i, nblk) if reverse else i
    per_blk = BLK // HALO
    main = lambda bb, i: (bb, blk_of(i), 0)
    prev = pl.BlockSpec((1, HALO, width), lambda bb, i: (bb, jnp.maximum(blk_of(i) * per_blk - 1, 0), 0))
    nxt = pl.BlockSpec((1, HALO, width),
                       lambda bb, i: (bb, jnp.minimum((blk_of(i) + 1) * per_blk, nblk * per_blk - 1), 0))
    return blk_of, main, prev, nxt


def _short_conv(xe_ref, x, prev, nxt, blk, nblk, cw_ref, cb_ref):
    prev_ok = blk >= 2
    next_ok = jnp.logical_and(blk >= 1, blk <= nblk - 2)
    xe_ref[0:HALO, :] = jnp.where(prev_ok, prev, 0.0)
    xe_ref[HALO:HALO + BLK, :] = x
    xe_ref[HALO + BLK:2 * HALO + BLK, :] = jnp.where(next_ok, nxt, 0.0)
    left = CONV_W // 2
    y = cb_ref[...] + xe_ref[HALO - left:HALO - left + BLK, :] * cw_ref[0:1, :]
    for j in range(1, CONV_W):
        y = y + xe_ref[HALO - left + j:HALO - left + j + BLK, :] * cw_ref[j:j + 1, :]
    return y


def _gla_body(*refs, reverse, nblk):
    if reverse:
        u_ref, w2_ref, b2_ref, of_ref, nw_ref, o_ref, st_ref = refs
    else:
        u_ref, w2_ref, b2_ref, o_ref, st_ref = refs
    i = pl.program_id(1)

    @pl.when(i == 0)
    def _():
        st_ref[...] = jnp.zeros_like(st_ref)

    ub = u_ref[0]
    q = ub[:, 0:128] * (GLA_DK ** -0.5)
    k = ub[:, 128:256]
    v = ub[:, 256:512].astype(BF16)
    lr = ub[:, 768:896]
    la = jax.nn.log_sigmoid(_dot(lr.astype(BF16), w2_ref[...]) + b2_ref[...]) * (1.0 / GLA_TAU)

    r = _iota((BLK, BLK), 0)
    c = _iota((BLK, BLK), 1)
    same = (r >> 6) == (c >> 6)
    tri = jnp.logical_and(same, (c >= r) if reverse else (c <= r))
    sel = jnp.concatenate([tri, same], axis=0).astype(BF16)
    gg = _dot_exact_left(sel, la)
    g = gg[0:BLK]
    gl = gg[BLK:2 * BLK]
    qg = q * jnp.exp(g)
    kg = (k * jnp.exp(-g)).astype(BF16)
    kd = (k * jnp.exp(gl - g)).astype(BF16)
    egl = jnp.exp(gl)

    lane_k = _iota((GLA_CHUNK, 128), 1) >> 5
    rr = _iota((GLA_HEADS * GLA_CHUNK, GLA_CHUNK), 0) & (GLA_CHUNK - 1)
    cc = _iota((GLA_HEADS * GLA_CHUNK, GLA_CHUNK), 1)
    amask = (cc >= rr) if reverse else (cc <= rr)
    lane_v = _iota((GLA_CHUNK, BRANCH), 1) >> 6
    bd = (_iota((BRANCH, 128), 0) >> 6) == (_iota((BRANCH, 128), 1) >> 5)

    n_chunks = BLK // GLA_CHUNK
    outs = [None] * n_chunks
    for ci in (reversed(range(n_chunks)) if reverse else range(n_chunks)):
        lo = ci * GLA_CHUNK
        qg_c = qg[lo:lo + GLA_CHUNK]
        v_c = v[lo:lo + GLA_CHUNK]
        qst = jnp.concatenate([jnp.where(lane_k == h, qg_c, 0.0) for h in range(GLA_HEADS)], axis=0)
        att = _dot_nt(qst.astype(BF16), kg[lo:lo + GLA_CHUNK])
        att = jnp.where(amask, att, 0.0)
        res = _dot(att.astype(BF16), v_c)
        o_c = jnp.zeros((GLA_CHUNK, BRANCH), F32)
        for h in range(GLA_HEADS):
            o_c = o_c + jnp.where(lane_v == h, res[h * GLA_CHUNK:(h + 1) * GLA_CHUNK], 0.0)
        st = st_ref[...]
        o_c = o_c + _dot_nt(qg_c.astype(BF16), st.astype(BF16))
        upd = _dot_tn(v_c, kd[lo:lo + GLA_CHUNK])
        st_ref[...] = st * egl[lo:lo + 1, :] + jnp.where(bd, upd, 0.0)
        outs[ci] = o_c
    o = jnp.concatenate(outs, axis=0)

    if reverse:
        o = o + of_ref[0]
        rv = _iota((BRANCH, BRANCH), 0) >> 6
        cv = _iota((BRANCH, BRANCH), 1) >> 6
        grp = (rv == cv).astype(BF16)
        ms = _dot_exact_right(o * o, grp, terms=2) * (1.0 / GLA_DV)
        y = o * lax.rsqrt(ms + EPS) * nw_ref[...]
        o_ref[0] = y * _silu(ub[:, 512:768])
    else:
        o_ref[0] = o


def _gla_call(u, w2, b2, reverse, o_fwd=None, nw=None):
    b, rows, _ = u.shape
    nblk = rows // BLK
    _, main, _, _ = _pass_specs(nblk, reverse, W_GLA)
    in_specs = [pl.BlockSpec((1, BLK, W_GLA), main),
                pl.BlockSpec((128, 128), lambda bb, i: (0, 0)),
                pl.BlockSpec((1, 128), lambda bb, i: (0, 0))]
    args = [u, w2, b2]
    if reverse:
        in_specs += [pl.BlockSpec((1, BLK, BRANCH), main), pl.BlockSpec((1, BRANCH), lambda bb, i: (0, 0))]
        args += [o_fwd, nw]
    return pl.pallas_call(
        functools.partial(_gla_body, reverse=reverse, nblk=nblk),
        grid=(b, nblk),
        in_specs=in_specs,
        out_specs=pl.BlockSpec((1, BLK, BRANCH), main),
        out_shape=jax.ShapeDtypeStruct((b, rows, BRANCH), F32),
        scratch_shapes=[pltpu.VMEM((BRANCH, 128), F32)],
        compiler_params=_params(("arbitrary", "arbitrary")),
        name="gla_bwd" if reverse else "gla_fwd",
    )(*args)


def _lru_body(*refs, reverse, nblk):
    if reverse:
        (u_ref, prev_ref, next_ref, cw_ref, cb_ref, wg_ref, bg_ref, lam_ref, hf_ref,
         o_ref, xe_ref, a_ref, s_ref, hb_ref, h_ref) = refs
    else:
        (u_ref, prev_ref, next_ref, cw_ref, cb_ref, wg_ref, bg_ref, lam_ref,
         o_ref, xe_ref, a_ref, s_ref, h_ref) = refs
    i = pl.program_id(1)
    blk = _bwd_block(i, nblk) if reverse else i

    @pl.when(i == 0)
    def _():
        h_ref[...] = jnp.zeros_like(h_ref)

    ub = u_ref[0]
    xc = _short_conv(xe_ref, ub[:, 0:BRANCH], prev_ref[0][:, 0:BRANCH], next_ref[0][:, 0:BRANCH],
                     blk, nblk, cw_ref, cb_ref)
    z = _dot(xc.astype(BF16), wg_ref[...]) + bg_ref[...]
    rg = jax.nn.sigmoid(z[:, 0:BRANCH])
    ig = jax.nn.sigmoid(z[:, BRANCH:2 * BRANCH])
    log_a = -LRU_C * rg * jax.nn.softplus(-lam_ref[...])
    a = jnp.exp(log_a)
    a_ref[...] = a
    s_ref[...] = jnp.sqrt(jnp.tanh(-log_a) * (1.0 + a * a)) * (ig * xc)

    dst = hb_ref if reverse else o_ref

    def group(gi, h):
        base = pl.multiple_of((BLK // 8 - 1 - gi if reverse else gi) * 8, 8)
        for rr in range(8):
            t = base + (7 - rr if reverse else rr)
            h = a_ref[pl.ds(t, 1), :] * h + s_ref[pl.ds(t, 1), :]
            if reverse:
                dst[pl.ds(t, 1), :] = h
            else:
                dst[0, pl.ds(t, 1), :] = h
        return h

    h_ref[...] = lax.fori_loop(0, BLK // 8, group, h_ref[...])
    if reverse:
        o_ref[0] = (hf_ref[0] + hb_ref[...]) * _silu(ub[:, BRANCH:2 * BRANCH])


def _lru_call(u, cw, cb, wg, bg, lam, reverse, h_fwd=None):
    b, rows, _ = u.shape
    nblk = rows // BLK
    _, main, prev, nxt = _pass_specs(nblk, reverse, W_LRU)
    const = lambda shape: pl.BlockSpec(shape, lambda bb, i: tuple(0 for _ in shape))
    in_specs = [pl.BlockSpec((1, BLK, W_LRU), main), prev, nxt,
                const((CONV_W, BRANCH)), const((1, BRANCH)), const((BRANCH, 2 * BRANCH)),
                const((1, 2 * BRANCH)), const((1, BRANCH))]
    args = [u, u, u, cw, cb, wg, bg, lam]
    scratch = [pltpu.VMEM((BLK + 2 * HALO, BRANCH), F32), pltpu.VMEM((BLK, BRANCH), F32),
               pltpu.VMEM((BLK, BRANCH), F32)]
    if reverse:
        in_specs.append(pl.BlockSpec((1, BLK, BRANCH), main))
        args.append(h_fwd)
        scratch.append(pltpu.VMEM((BLK, BRANCH), F32))
    scratch.append(pltpu.VMEM((1, BRANCH), F32))
    return pl.pallas_call(
        functools.partial(_lru_body, reverse=reverse, nblk=nblk),
        grid=(b, nblk),
        in_specs=in_specs,
        out_specs=pl.BlockSpec((1, BLK, BRANCH), main),
        out_shape=jax.ShapeDtypeStruct((b, rows, BRANCH), F32),
        scratch_shapes=scratch,
        compiler_params=_params(("arbitrary", "arbitrary")),
        name="lru_bwd" if reverse else "lru_fwd",
    )(*args)


def _ssd_body(*refs, reverse, nblk):
    if reverse:
        (u_ref, prev_ref, next_ref, cw_ref, cb_ref, dtb_ref, an_ref, yf_ref, dsk_ref, nw_ref,
         o_ref, xe_ref, st_ref) = refs
    else:
        u_ref, prev_ref, next_ref, cw_ref, cb_ref, dtb_ref, an_ref, o_ref, xe_ref, st_ref = refs
    i = pl.program_id(1)
    blk = _bwd_block(i, nblk) if reverse else i

    @pl.when(i == 0)
    def _():
        st_ref[...] = jnp.zeros_like(st_ref)

    ub = u_ref[0]
    xc = _silu(_short_conv(xe_ref, ub[:, 0:512], prev_ref[0][:, 0:512], next_ref[0][:, 0:512],
                           blk, nblk, cw_ref, cb_ref))
    xs = xc[:, 0:256]
    bm = xc[:, 256:384].astype(BF16)
    cm = xc[:, 384:512]
    dt = jax.nn.softplus(ub[:, 768:896] + dtb_ref[...])
    da = dt * an_ref[...]
    off = SSD_HEADS if reverse else 0
    ch = SSD_CHUNK

    r = _iota((ch, ch), 0)
    c = _iota((ch, ch), 1)
    keep = (c >= r) if reverse else (c <= r)
    tri = keep.astype(BF16)
    expand = (_iota((128, BRANCH), 0) == off + (_iota((128, BRANCH), 1) >> 6)).astype(BF16)
    lane_n = _iota((ch, 128), 1) >> 6
    lane_p = _iota((ch, BRANCH), 1) >> 6
    bd = (_iota((128, BRANCH), 0) >> 6) == (_iota((128, BRANCH), 1) >> 7)
    edge = 0 if reverse else ch - 1

    n_chunks = BLK // ch
    outs = [None] * n_chunks
    for ci in (reversed(range(n_chunks)) if reverse else range(n_chunks)):
        lo = ci * ch
        xs_c = xs[lo:lo + ch]
        bm_c = bm[lo:lo + ch]
        cm_c = cm[lo:lo + ch]
        dt_c = dt[lo:lo + ch]
        cum = _dot_exact_left(tri, da[lo:lo + ch])
        cum_t = cum.T
        dt_t = dt_c.T
        cum_x = _dot_exact_right(cum, expand)
        dt_x = _dot_exact_right(dt_c, expand, terms=2)
        tot_x = cum_x[edge:edge + 1, :]
        y_c = jnp.zeros((ch, BRANCH), F32)
        for g in range(SSD_GROUPS):
            cb = _dot_nt(jnp.where(lane_n == g, cm_c, 0.0).astype(BF16), bm_c)
            for h in range(g * (SSD_HEADS // SSD_GROUPS), (g + 1) * (SSD_HEADS // SSD_GROUPS)):
                seg = cum[:, off + h:off + h + 1] - cum_t[off + h:off + h + 1, :]
                lm = jnp.exp(jnp.where(keep, seg, -1e30))
                m = cb * lm * dt_t[off + h:off + h + 1, :]
                res = _dot(m.astype(BF16), xs_c.astype(BF16))
                y_c = y_c + jnp.where(lane_p == h, res, 0.0)
        st = st_ref[...]
        y_c = y_c + _dot(cm_c.astype(BF16), st.astype(BF16)) * jnp.exp(cum_x)
        xw = xs_c * (jnp.exp(tot_x - cum_x) * dt_x)
        upd = _dot_tn(bm_c, xw.astype(BF16))
        st_ref[...] = st * jnp.exp(tot_x) + jnp.where(bd, upd, 0.0)
        outs[ci] = y_c
    y = jnp.concatenate(outs, axis=0)

    if reverse:
        y = y + yf_ref[0] + dsk_ref[...] * xs
        yz = y * _silu(ub[:, 512:768])
        ms = jnp.mean(yz * yz, axis=-1, keepdims=True)
        o_ref[0] = yz * lax.rsqrt(ms + EPS) * nw_ref[...]
    else:
        o_ref[0] = y


def _ssd_call(u, cw, cb, dtb, aneg, reverse, y_fwd=None, dskip=None, nw=None):
    b, rows, _ = u.shape
    nblk = rows // BLK
    _, main, prev, nxt = _pass_specs(nblk, reverse, W_SSD)
    const = lambda shape: pl.BlockSpec(shape, lambda bb, i: tuple(0 for _ in shape))
    in_specs = [pl.BlockSpec((1, BLK, W_SSD), main), prev, nxt,
                const((CONV_W, 512)), const((1, 512)), const((1, 128)), const((1, 128))]
    args = [u, u, u, cw, cb, dtb, aneg]
    if reverse:
        in_specs += [pl.BlockSpec((1, BLK, BRANCH), main), const((1, BRANCH)), const((1, BRANCH))]
        args += [y_fwd, dskip, nw]
    return pl.pallas_call(
        functools.partial(_ssd_body, reverse=reverse, nblk=nblk),
        grid=(b, nblk),
        in_specs=in_specs,
        out_specs=pl.BlockSpec((1, BLK, BRANCH), main),
        out_shape=jax.ShapeDtypeStruct((b, rows, BRANCH), F32),
        scratch_shapes=[pltpu.VMEM((BLK + 2 * HALO, 512), F32), pltpu.VMEM((128, BRANCH), F32)],
        compiler_params=_params(("arbitrary", "arbitrary")),
        name="ssd_bwd" if reverse else "ssd_fwd",
    )(*args)


def _attn_prep_body(u_ref, cos_ref, sin_ref, qt_ref, k_ref, vt_ref):
    i = pl.program_id(1)
    ub = u_ref[0]
    is_ctx = i == 0
    cs = jnp.where(is_ctx, 1.0, cos_ref[...])
    sn = jnp.where(is_ctx, 0.0, sin_ref[...])
    first = (_iota((BLK, BRANCH), 1) & 15) < 8

    def rope(x):
        rot = jnp.where(first, pltpu.roll(x, BRANCH - 8, 1), pltpu.roll(x, 8, 1))
        return x * cs + rot * sn

    qt_ref[0, 0] = (rope(ub[:, 0:256]) * (DIFF_D ** -0.5 * math.log2(math.e))).T.astype(BF16)
    k_ref[0] = rope(ub[:, 256:512]).astype(BF16)
    vt_ref[0, 0] = ub[:, 512:768].T.astype(BF16)


def _attn_prep_call(u, cos, sin):
    b, rows, _ = u.shape
    nblk = rows // BLK
    tab = pl.BlockSpec((BLK, BRANCH), lambda bb, i: (jnp.maximum(i - 1, 0), 0))
    return pl.pallas_call(
        _attn_prep_body,
        grid=(b, nblk),
        in_specs=[pl.BlockSpec((1, BLK, W_DIFF), lambda bb, i: (bb, i, 0)), tab, tab],
        out_specs=[pl.BlockSpec((1, 1, BRANCH, BLK), lambda bb, i: (bb, i, 0, 0)),
                   pl.BlockSpec((1, BLK, BRANCH), lambda bb, i: (bb, i, 0)),
                   pl.BlockSpec((1, 1, BRANCH, BLK), lambda bb, i: (bb, i, 0, 0))],
        out_shape=[jax.ShapeDtypeStruct((b, nblk, BRANCH, BLK), BF16),
                   jax.ShapeDtypeStruct((b, rows, BRANCH), BF16),
                   jax.ShapeDtypeStruct((b, nblk, BRANCH, BLK), BF16)],
        compiler_params=_params(("arbitrary", "arbitrary")),
        name="attn_prep",
    )(u, cos, sin)


def _key_step(nk):
    return max(t for t in range(BLK, min(ATT_TK, nk) + 1, BLK) if nk % t == 0)


def _attn_body(qt_ref, k_ref, vt_ref, g_ref, lam_ref, sw_ref, o_ref, s0_ref, s1_ref, *, nk, lam_init):
    tk = _key_step(nk)
    n_steps = nk // tk
    sub = tk // BLK
    fold = 32
    lv = lam_ref[...]
    lam = (jnp.exp(jnp.sum(lv[0:1] * lv[1:2], axis=-1, keepdims=True))
           - jnp.exp(jnp.sum(lv[2:3] * lv[3:4], axis=-1, keepdims=True)) + lam_init)
    row = _iota((128, BLK), 0)
    pairs = [(h, half) for h in range(DIFF_HEADS) for half in range(2)]
    s_bufs = (s0_ref, s1_ref)

    def masked_qt(p):
        h, half = pairs[p]
        qt = qt_ref[0, 0, (h // 2) * 128:(h // 2 + 1) * 128, :]
        lo = (h % 2) * 64 + half * DIFF_D
        return jnp.where(jnp.logical_and(row >= lo, row < lo + DIFF_D), qt, jnp.zeros_like(qt))

    def key_start(j):
        return j * tk if isinstance(j, int) else pl.multiple_of(j * tk, BLK)

    def stage_a(p, qm, j, mx):
        tile = pairs[p][0] // 2
        for jj in range(sub):
            rows = pl.ds(key_start(j) + jj * BLK, BLK)
            s = _dot(k_ref[0, rows, tile * 128:(tile + 1) * 128], qm)
            s_bufs[p % 2][rows, :] = s
            mx = jnp.maximum(mx, jnp.max(s.reshape(BLK // fold, fold, BLK), axis=0))
        return mx

    def stage_bc(p, m, j, ls, acc):
        h = pairs[p][0]
        for jj in range(sub):
            e = jnp.exp2(s_bufs[p % 2][pl.ds(key_start(j) + jj * BLK, BLK), :] - m)
            ls = ls + jnp.sum(e.reshape(BLK // fold, fold, BLK), axis=0)
            acc = acc + _dot(vt_ref[0, j * sub + jj, h * DIFF_DV:(h + 1) * DIFF_DV, :], e.astype(BF16))
        return ls, acc

    results = []
    m_prev = None
    for p in range(len(pairs) + 1):
        do_a = p < len(pairs)
        do_bc = p >= 1
        qm = masked_qt(p) if do_a else None

        def step(j, carry, p=p, qm=qm, m_prev=m_prev, do_a=do_a, do_bc=do_bc):
            mx, ls, acc = carry
            if do_bc:
                ls, acc = stage_bc(p - 1, m_prev, j, ls, acc)
            if do_a:
                mx = stage_a(p, qm, j, mx)
            return mx, ls, acc

        carry = (jnp.full((fold, BLK), -jnp.inf, F32), jnp.zeros((fold, BLK), F32),
                 jnp.zeros((DIFF_DV, BLK), F32))
        if n_steps > 1:
            mx, ls, acc = lax.fori_loop(0, n_steps, step, carry)
        else:
            mx, ls, acc = step(0, carry)
        if do_bc:
            results.append(acc / jnp.sum(ls, axis=0, keepdims=True))
        if do_a:
            m_prev = jnp.max(mx, axis=0, keepdims=True)

    heads = [results[2 * h] - lam * results[2 * h + 1] for h in range(DIFF_HEADS)]
    o = jnp.concatenate(heads, axis=0).T
    grp = ((_iota((BRANCH, BRANCH), 0) >> 6) == (_iota((BRANCH, BRANCH), 1) >> 6)).astype(BF16)
    ms = _dot_exact_right(o * o, grp, terms=2) * (1.0 / DIFF_DV)
    y = o * lax.rsqrt(ms + EPS) * sw_ref[...] * (1.0 - lam_init)
    o_ref[0] = y * _silu(g_ref[0])


def _attn_call(qt, k, vt, u, lam_vecs, sw, lam_init, ctx_only):
    b, rows, _ = k.shape
    nblk = rows // BLK
    nk = BLK if ctx_only else rows
    nq = 1 if ctx_only else nblk - 1
    qoff = 0 if ctx_only else 1
    const = lambda shape: pl.BlockSpec(shape, lambda bb, i: tuple(0 for _ in shape))
    return pl.pallas_call(
        functools.partial(_attn_body, nk=nk, lam_init=lam_init),
        grid=(b, nq),
        in_specs=[pl.BlockSpec((1, 1, BRANCH, BLK), lambda bb, i: (bb, i + qoff, 0, 0)),
                  pl.BlockSpec((1, nk, BRANCH), lambda bb, i: (bb, 0, 0)),
                  pl.BlockSpec((1, nk // BLK, BRANCH, BLK), lambda bb, i: (bb, 0, 0, 0)),
                  pl.BlockSpec((1, BLK, BRANCH), lambda bb, i: (bb, i + qoff, 3)),
                  const((4, DIFF_D)), const((1, BRANCH))],
        out_specs=pl.BlockSpec((1, BLK, BRANCH), lambda bb, i: (bb, i, 0)),
        out_shape=jax.ShapeDtypeStruct((b, nq * BLK, BRANCH), F32),
        scratch_shapes=[pltpu.VMEM((nk, BLK), F32), pltpu.VMEM((nk, BLK), F32)],
        compiler_params=_params(("arbitrary", "arbitrary")),
        name="diff_attn_ctx" if ctx_only else "diff_attn",
    )(qt, k, vt, u, lam_vecs, sw)


def _pad_cols(parts, width):
    n = sum(p.shape[-1] for p in parts)
    if n < width:
        parts = parts + [jnp.zeros(parts[0].shape[:-1] + (width - n,), parts[0].dtype)]
    return jnp.concatenate(parts, axis=-1)


def _layout_w_in(w):
    a, b_, c_, d_ = 0, 800, 1312, 2336
    gla = _pad_cols([w[:, a:a + 512], w[:, a + 544:a + 800], w[:, a + 512:a + 544]], W_GLA)
    lru = w[:, b_:b_ + 512]
    diff = w[:, c_:c_ + 1024]
    ssd = _pad_cols([w[:, d_:d_ + 512], w[:, d_ + 520:d_ + 776], w[:, d_ + 512:d_ + 520]], W_SSD)
    return jnp.concatenate([gla, lru, diff, ssd], axis=-1).astype(BF16)


def _block_diag(w):
    nb, n, _ = w.shape
    eye = jnp.eye(nb, dtype=w.dtype)
    return (eye[:, None, :, None] * w[:, :, None, :]).reshape(nb * n, nb * n)


def _rope_tables(t_lat):
    n_freq = DIFF_D // 4
    inv = ROPE_BASE ** (-jnp.arange(n_freq, dtype=F32) / n_freq)
    tpos = jnp.arange(t_lat)
    ang_r = (tpos // GRID_W).astype(F32)[:, None] * inv
    ang_c = (tpos % GRID_W).astype(F32)[:, None] * inv
    ang = jnp.concatenate([ang_r, ang_r, ang_c, ang_c], axis=-1)
    reps = BRANCH // DIFF_D
    sign = jnp.where((jnp.arange(DIFF_D) % 16) < 8, -1.0, 1.0).astype(F32)
    return jnp.tile(jnp.cos(ang), (1, reps)), jnp.tile(jnp.sin(ang) * sign, (1, reps))


def kernel(x, c, ctx, c_ctx, w_mod, b_mod, norm_w, w_in, w_out, gla_w2, gla_b2, gla_norm_w, lru_conv_w, lru_conv_b, lru_wa, lru_ba, lru_wx, lru_bx, lru_lam, diff_lam, diff_subln_w, ssd_conv_w, ssd_conv_b, ssd_dt_bias, ssd_a_log, ssd_d, ssd_norm_w, final_norm_w):
    bsz, t_lat, d = x.shape
    depth = w_mod.shape[0]
    assert ctx.shape[1] == BLK and t_lat % BLK == 0 and d == D_MODEL and bsz + 1 <= 8

    h = jnp.concatenate([ctx, x], axis=1)
    cvec = jnp.zeros((8, d), F32).at[0:bsz].set(c).at[bsz].set(c_ctx)
    mod = _mod_call(cvec, w_mod, b_mod)
    cos, sin = _rope_tables(t_lat)

    for l in range(depth):
        last = l == depth - 1
        lam_init = 0.8 - 0.6 * math.exp(-0.3 * l)
        m_lat = mod[l, 0:bsz]
        m_ctx = jnp.broadcast_to(mod[l, bsz], (bsz, 3 * d))
        rows = jnp.stack([m_ctx, m_lat], axis=1)[:, :, None, :]
        shift, scale, gate = rows[..., 0:d], rows[..., d:2 * d], rows[..., 2 * d:3 * d]

        ua, ub, uc, ud = _inproj_call(h, norm_w[l][None], scale, shift, _layout_w_in(w_in[l]))

        w2 = jnp.zeros((2, 128, 128), F32)
        w2 = w2.at[0, 0:GLA_RANK].set(gla_w2[l, 0]).at[1, GLA_RANK:2 * GLA_RANK].set(gla_w2[l, 1]).astype(BF16)
        gnw = jnp.tile(gla_norm_w[l], GLA_HEADS)[None]
        of = _gla_call(ua, w2[0], gla_b2[l, 0][None], False)
        ya = _gla_call(ua, w2[1], gla_b2[l, 1][None], True, of, gnw)

        wg = [jnp.concatenate([_block_diag(lru_wa[l, dr]), _block_diag(lru_wx[l, dr])], axis=1).astype(BF16)
              for dr in range(2)]
        bg = [jnp.concatenate([lru_ba[l, dr], lru_bx[l, dr]])[None] for dr in range(2)]
        hf = _lru_call(ub, lru_conv_w[l], lru_conv_b[l][None], wg[0], bg[0], lru_lam[l, 0][None], False)
        yb = _lru_call(ub, lru_conv_w[l], lru_conv_b[l][None], wg[1], bg[1], lru_lam[l, 1][None], True, hf)

        qp, kp, vt = _attn_prep_call(uc, cos, sin)
        sw = jnp.tile(diff_subln_w[l], DIFF_HEADS)[None]
        yc_lat = _attn_call(qp, kp, vt, uc, diff_lam[l], sw, lam_init, False)
        if last:
            yc = jnp.concatenate([jnp.zeros((bsz, BLK, BRANCH), F32), yc_lat], axis=1)
        else:
            yc = jnp.concatenate([_attn_call(qp, kp, vt, uc, diff_lam[l], sw, lam_init, True), yc_lat], axis=1)

        dtb = jnp.zeros((1, 128), F32).at[0, 0:2 * SSD_HEADS].set(ssd_dt_bias[l].reshape(-1))
        aneg = jnp.zeros((1, 128), F32).at[0, 0:2 * SSD_HEADS].set(-jnp.exp(ssd_a_log[l].reshape(-1)))
        dsk = jnp.repeat(ssd_d[l], SSD_P)[None]
        yf = _ssd_call(ud, ssd_conv_w[l], ssd_conv_b[l][None], dtb, aneg, False)
        yd = _ssd_call(ud, ssd_conv_w[l], ssd_conv_b[l][None], dtb, aneg, True, yf, dsk, ssd_norm_w[l][None])

        h = _outproj_call([ya, yb, yc, yd], h, gate, w_out[l].astype(BF16), final_norm_w[None], last)
    return h
```

```python
import functools
import math

import jax
import jax.numpy as jnp
from jax import lax
from jax.experimental import pallas as pl
from jax.experimental.pallas import tpu as pltpu

F32 = jnp.float32
BF16 = jnp.bfloat16

D_MODEL = 1024
BRANCH = D_MODEL // 4
GRID_W = 64
EPS = 1e-6
CONV_W = 4
BLK = 256
HALO = 8
GLA_HEADS, GLA_DK, GLA_DV = 4, 32, 64
GLA_RANK = 16
GLA_TAU = 16.0
GLA_CHUNK = 64
LRU_C = 8.0
DIFF_HEADS, DIFF_D, DIFF_DV = 4, 32, 64
ROPE_BASE = 10000.0
SSD_HEADS, SSD_P, SSD_N, SSD_GROUPS = 4, 64, 64, 2
SSD_CHUNK = 128
ATT_TK = 2816
W_GLA, W_LRU, W_DIFF, W_SSD = 896, 512, 1024, 896
W_ALL = W_GLA + W_LRU + W_DIFF + W_SSD
VMEM_LIMIT = 56 * 1024 * 1024


def _dot(a, b):
    return jnp.dot(a, b, preferred_element_type=F32)


def _dot_nt(a, b):
    return lax.dot_general(a, b, (((1,), (1,)), ((), ())), preferred_element_type=F32)


def _dot_tn(a, b):
    return lax.dot_general(a, b, (((0,), (0,)), ((), ())), preferred_element_type=F32)


def _split(x, terms):
    out = []
    r = x
    for _ in range(terms):
        p = r.astype(BF16)
        out.append(p)
        r = r - p.astype(F32)
    return out


def _dot_exact_left(m, x, terms=3):
    return sum(_dot(m, p) for p in _split(x, terms))


def _dot_exact_right(x, m, terms=3):
    return sum(_dot(p, m) for p in _split(x, terms))


def _iota(shape, dim):
    return lax.broadcasted_iota(jnp.int32, shape, dim)


def _silu(x):
    return x * jax.nn.sigmoid(x)


def _params(n_axes):
    return pltpu.CompilerParams(dimension_semantics=("arbitrary",) * n_axes, vmem_limit_bytes=VMEM_LIMIT)


def _layer_spec(shape, layer, n_axes):
    block = (1,) + tuple(shape[1:])
    idx = (layer,) + (0,) * (len(shape) - 1)
    if n_axes == 1:
        return pl.BlockSpec(block, lambda i: idx)
    return pl.BlockSpec(block, lambda bb, i: idx)


def _bwd_block(i, nblk):
    return jnp.where(i == 0, 0, nblk - i)


def _mod_body(c_ref, w_ref, b_ref, o_ref):
    s = _silu(c_ref[...])
    o_ref[0] = _dot(s.astype(BF16), w_ref[0].astype(BF16)) + b_ref[0]


def _mod_call(cvec, w_mod, b_mod):
    depth, d, n = w_mod.shape
    tn = 1024
    return pl.pallas_call(
        _mod_body,
        grid=(depth, n // tn),
        in_specs=[pl.BlockSpec((8, d), lambda l, j: (0, 0)),
                  pl.BlockSpec((1, d, tn), lambda l, j: (l, 0, j)),
                  pl.BlockSpec((1, 1, tn), lambda l, j: (l, 0, j))],
        out_specs=pl.BlockSpec((1, 8, tn), lambda l, j: (l, 0, j)),
        out_shape=jax.ShapeDtypeStruct((depth, 8, n), F32),
        compiler_params=_params(2),
        name="adaln_mod",
    )(cvec, w_mod, b_mod.reshape(depth, 1, n))


def _mod_row(mod_ref, is_ctx, bb, n_batch, part):
    r = jnp.where(is_ctx, n_batch, bb)
    return mod_ref[0, pl.ds(r, 1), part * D_MODEL:(part + 1) * D_MODEL]


def _inproj_body(lat_ref, ctx_ref, nw_ref, mod_ref, w_ref, cos_ref, sin_ref,
                 oa_ref, ob_ref, od_ref, qt_ref, k_ref, vt_ref, g_ref, *, n_batch):
    bb = pl.program_id(0)
    i = pl.program_id(1)
    is_ctx = i == 0
    x = jnp.where(is_ctx, ctx_ref[0], lat_ref[0])
    ms = jnp.mean(x * x, axis=-1, keepdims=True)
    xn = x * lax.rsqrt(ms + EPS) * nw_ref[0]
    xm = xn * (1.0 + _mod_row(mod_ref, is_ctx, bb, n_batch, 1)) + _mod_row(mod_ref, is_ctx, bb, n_batch, 0)
    u = _dot(xm.astype(BF16), w_ref[0])
    oa_ref[0] = u[:, 0:W_GLA]
    ob_ref[0] = u[:, W_GLA:W_GLA + W_LRU]
    od_ref[0] = u[:, W_GLA + W_LRU + W_DIFF:W_ALL]

    ud = u[:, W_GLA + W_LRU:W_GLA + W_LRU + W_DIFF]
    cs = jnp.where(is_ctx, 1.0, cos_ref[...])
    sn = jnp.where(is_ctx, 0.0, sin_ref[...])
    first = (_iota((BLK, BRANCH), 1) & 15) < 8

    def rope(t):
        rot = jnp.where(first, pltpu.roll(t, BRANCH - 8, 1), pltpu.roll(t, 8, 1))
        return t * cs + rot * sn

    qt_ref[0, 0] = (rope(ud[:, 0:256]) * (DIFF_D ** -0.5 * math.log2(math.e))).T.astype(BF16)
    k_ref[0] = rope(ud[:, 256:512]).astype(BF16)
    vt_ref[0, 0] = ud[:, 512:768].T.astype(BF16)
    g_ref[0] = ud[:, 768:1024]


def _inproj_call(lat, lat_off, ctx, layer, nblk, nw, mod, w, cos, sin):
    b = lat.shape[0]
    d = D_MODEL
    rows = nblk * BLK
    row_out = lambda n: pl.BlockSpec((1, BLK, n), lambda bb, i: (bb, i, 0))
    t_out = pl.BlockSpec((1, 1, BRANCH, BLK), lambda bb, i: (bb, i, 0, 0))
    tab = pl.BlockSpec((BLK, BRANCH), lambda bb, i: (jnp.maximum(i - 1, 0), 0))
    return pl.pallas_call(
        functools.partial(_inproj_body, n_batch=b),
        grid=(b, nblk),
        in_specs=[pl.BlockSpec((1, BLK, d), lambda bb, i: (bb, jnp.maximum(i - 1, 0) + lat_off, 0)),
                  pl.BlockSpec((1, BLK, d), lambda bb, i: (bb, 0, 0)),
                  _layer_spec(nw.shape, layer, 2), _layer_spec(mod.shape, layer, 2), _layer_spec(w.shape, layer, 2),
                  tab, tab],
        out_specs=[row_out(W_GLA), row_out(W_LRU), row_out(W_SSD), t_out, row_out(BRANCH), t_out, row_out(BRANCH)],
        out_shape=[jax.ShapeDtypeStruct((b, rows, W_GLA), F32),
                   jax.ShapeDtypeStruct((b, rows, W_LRU), F32),
                   jax.ShapeDtypeStruct((b, rows, W_SSD), F32),
                   jax.ShapeDtypeStruct((b, nblk, BRANCH, BLK), BF16),
                   jax.ShapeDtypeStruct((b, rows, BRANCH), BF16),
                   jax.ShapeDtypeStruct((b, nblk, BRANCH, BLK), BF16),
                   jax.ShapeDtypeStruct((b, rows, BRANCH), F32)],
        compiler_params=_params(2),
        name="norm_inproj",
    )(lat, ctx, nw, mod, w, cos, sin)


def _outproj_body(ya_ref, yb_ref, ycl_ref, ycc_ref, yd_ref, lat_ref, ctx_ref, mod_ref, w_ref, fw_ref, o_ref,
                  *, final, n_batch):
    bb = pl.program_id(0)
    i = pl.program_id(1)
    if final:
        is_ctx = False
        yc, res = ycl_ref[0], lat_ref[0]
    else:
        is_ctx = i == 0
        yc = jnp.where(is_ctx, ycc_ref[0], ycl_ref[0])
        res = jnp.where(is_ctx, ctx_ref[0], lat_ref[0])
    y = jnp.concatenate([ya_ref[0], yb_ref[0], yc, yd_ref[0]], axis=-1)
    z = _dot(y, w_ref[0])
    hn = res + _mod_row(mod_ref, is_ctx, bb, n_batch, 2) * z
    if final:
        ms = jnp.mean(hn * hn, axis=-1, keepdims=True)
        hn = hn * lax.rsqrt(ms + EPS) * fw_ref[...]
    o_ref[0] = hn


def _outproj_call(ya, yb, yc_lat, yc_ctx, yd, lat, lat_off, ctx, layer, nblk, mod, w, fw, final):
    b = lat.shape[0]
    d = D_MODEL
    off = 1 if final else 0
    nout = nblk - off
    y_spec = pl.BlockSpec((1, BLK, BRANCH), lambda bb, i: (bb, i + off, 0))
    lat_blk = (lambda i: i) if final else (lambda i: jnp.maximum(i - 1, 0))
    return pl.pallas_call(
        functools.partial(_outproj_body, final=final, n_batch=b),
        grid=(b, nout),
        in_specs=[y_spec, y_spec,
                  pl.BlockSpec((1, BLK, BRANCH), lambda bb, i: (bb, lat_blk(i), 0)),
                  pl.BlockSpec((1, BLK, BRANCH), lambda bb, i: (bb, 0, 0)),
                  y_spec,
                  pl.BlockSpec((1, BLK, d), lambda bb, i: (bb, lat_blk(i) + lat_off, 0)),
                  pl.BlockSpec((1, BLK, d), lambda bb, i: (bb, 0, 0)),
                  _layer_spec(mod.shape, layer, 2), _layer_spec(w.shape, layer, 2),
                  pl.BlockSpec((1, d), lambda bb, i: (0, 0))],
        out_specs=pl.BlockSpec((1, BLK, d), lambda bb, i: (bb, i, 0)),
        out_shape=jax.ShapeDtypeStruct((b, nout * BLK, d), F32),
        compiler_params=_params(2),
        name="outproj_final" if final else "outproj",
    )(ya, yb, yc_lat, yc_ctx, yd, lat, ctx, mod, w, fw)


def _pass_specs(nblk, reverse, n_batch, width):
    def blk_of(i):
        return _bwd_block(i, nblk) if reverse else i
    per_blk = BLK // HALO
    main = lambda i: (0, blk_of(i), 0)
    prev = pl.BlockSpec((n_batch, HALO, width), lambda i: (0, jnp.maximum(blk_of(i) * per_blk - 1, 0), 0))
    nxt = pl.BlockSpec((n_batch, HALO, width),
                       lambda i: (0, jnp.minimum((blk_of(i) + 1) * per_blk, nblk * per_blk - 1), 0))
    return main, prev, nxt


def _short_conv(xe_ref, x, prev, nxt, blk, nblk, cw, cb):
    prev_ok = blk >= 2
    next_ok = jnp.logical_and(blk >= 1, blk <= nblk - 2)
    xe_ref[0:HALO, :] = jnp.where(prev_ok, prev, 0.0)
    xe_ref[HALO:HALO + BLK, :] = x
    xe_ref[HALO + BLK:2 * HALO + BLK, :] = jnp.where(next_ok, nxt, 0.0)
    left = CONV_W // 2
    y = cb + xe_ref[HALO - left:HALO - left + BLK, :] * cw[0:1, :]
    for j in range(1, CONV_W):
        y = y + xe_ref[HALO - left + j:HALO - left + j + BLK, :] * cw[j:j + 1, :]
    return y


def _gla_blocks(ubs, w2, b2, sts, reverse):
    nb = len(ubs)
    r = _iota((BLK, BLK), 0)
    c = _iota((BLK, BLK), 1)
    same = (r >> 6) == (c >> 6)
    tri = jnp.logical_and(same, (c >= r) if reverse else (c <= r))
    sel = jnp.concatenate([tri, same], axis=0).astype(BF16)
    lane_k = _iota((GLA_CHUNK, 128), 1) >> 5
    rr = _iota((GLA_HEADS * GLA_CHUNK, GLA_CHUNK), 0) & (GLA_CHUNK - 1)
    cc = _iota((GLA_HEADS * GLA_CHUNK, GLA_CHUNK), 1)
    amask = (cc >= rr) if reverse else (cc <= rr)
    lane_v = _iota((GLA_CHUNK, BRANCH), 1) >> 6
    bd = (_iota((BRANCH, 128), 0) >> 6) == (_iota((BRANCH, 128), 1) >> 5)

    las = [jax.nn.log_sigmoid(_dot(ub[:, 768:896].astype(BF16), w2) + b2) * (1.0 / GLA_TAU) for ub in ubs]
    ggs = [_dot_exact_left(sel, la) for la in las]
    gs = [gg[0:BLK] for gg in ggs]
    gls = [gg[BLK:2 * BLK] for gg in ggs]
    qgs = [ub[:, 0:128] * (GLA_DK ** -0.5) * jnp.exp(g) for ub, g in zip(ubs, gs)]
    kgs = [(ub[:, 128:256] * jnp.exp(-g)).astype(BF16) for ub, g in zip(ubs, gs)]
    kds = [(ub[:, 128:256] * jnp.exp(gl - g)).astype(BF16) for ub, g, gl in zip(ubs, gs, gls)]
    egls = [jnp.exp(gl) for gl in gls]
    vs = [ub[:, 256:512].astype(BF16) for ub in ubs]

    n_chunks = BLK // GLA_CHUNK
    order = list(reversed(range(n_chunks))) if reverse else list(range(n_chunks))
    rows = lambda a, ci: a[ci * GLA_CHUNK:(ci + 1) * GLA_CHUNK]
    cb = [(ci, bb) for ci in order for bb in range(nb)]

    def stacked_q(ci, bb):
        qg_c = rows(qgs[bb], ci)
        return jnp.concatenate([jnp.where(lane_k == h, qg_c, 0.0) for h in range(GLA_HEADS)], axis=0).astype(BF16)

    att = {k: jnp.where(amask, _dot_nt(stacked_q(*k), rows(kgs[k[1]], k[0])), 0.0).astype(BF16) for k in cb}
    res = {k: _dot(att[k], rows(vs[k[1]], k[0])) for k in cb}
    upd = {k: jnp.where(bd, _dot_tn(rows(vs[k[1]], k[0]), rows(kds[k[1]], k[0])), 0.0) for k in cb}
    sts = list(sts)
    outs = [[None] * n_chunks for _ in range(nb)]
    for ci, bb in cb:
        o_c = _dot_nt(rows(qgs[bb], ci).astype(BF16), sts[bb].astype(BF16))
        for h in range(GLA_HEADS):
            o_c = o_c + jnp.where(lane_v == h, res[(ci, bb)][h * GLA_CHUNK:(h + 1) * GLA_CHUNK], 0.0)
        sts[bb] = sts[bb] * egls[bb][ci * GLA_CHUNK:ci * GLA_CHUNK + 1, :] + upd[(ci, bb)]
        outs[bb][ci] = o_c
    return [jnp.concatenate(o, axis=0) for o in outs], sts


def _gla_body(*refs, reverse, n_batch):
    if reverse:
        u_ref, w2_ref, b2_ref, of_ref, nw_ref, o_ref, st_ref = refs
    else:
        u_ref, w2_ref, b2_ref, o_ref, st_ref = refs

    @pl.when(pl.program_id(0) == 0)
    def _():
        st_ref[...] = jnp.zeros_like(st_ref)

    ubs = [u_ref[bb] for bb in range(n_batch)]
    os_, sts = _gla_blocks(ubs, w2_ref[0, 0], b2_ref[0, 0], [st_ref[bb] for bb in range(n_batch)], reverse)
    if reverse:
        grp = ((_iota((BRANCH, BRANCH), 0) >> 6) == (_iota((BRANCH, BRANCH), 1) >> 6)).astype(BF16)
        os_ = [o + of_ref[bb] for bb, o in enumerate(os_)]
        mss = [_dot_exact_right(o * o, grp, terms=2) * (1.0 / GLA_DV) for o in os_]
        os_ = [(o * lax.rsqrt(ms + EPS) * nw_ref[0] * _silu(ub[:, 512:768])).astype(BF16)
               for o, ms, ub in zip(os_, mss, ubs)]
    for bb in range(n_batch):
        st_ref[bb] = sts[bb]
        o_ref[bb] = os_[bb]


def _gla_call(u, layer, w2, b2, reverse, o_fwd=None, nw=None):
    b, rows, _ = u.shape
    nblk = rows // BLK
    main, _, _ = _pass_specs(nblk, reverse, b, W_GLA)
    dr = 1 if reverse else 0
    in_specs = [pl.BlockSpec((b, BLK, W_GLA), main),
                pl.BlockSpec((1, 1, 128, 128), lambda i: (layer, dr, 0, 0)),
                pl.BlockSpec((1, 1, 1, 128), lambda i: (layer, dr, 0, 0))]
    args = [u, w2, b2]
    if reverse:
        in_specs += [pl.BlockSpec((b, BLK, BRANCH), main), _layer_spec(nw.shape, layer, 1)]
        args += [o_fwd, nw]
    return pl.pallas_call(
        functools.partial(_gla_body, reverse=reverse, n_batch=b),
        grid=(nblk,),
        in_specs=in_specs,
        out_specs=pl.BlockSpec((b, BLK, BRANCH), main),
        out_shape=jax.ShapeDtypeStruct((b, rows, BRANCH), BF16 if reverse else F32),
        scratch_shapes=[pltpu.VMEM((b, BRANCH, 128), F32)],
        compiler_params=_params(1),
        name="gla_bwd" if reverse else "gla_fwd",
    )(*args)


def _lru_body(*refs, reverse, nblk, n_batch):
    if reverse:
        (u_ref, prev_ref, next_ref, cw_ref, cb_ref, wg_ref, bg_ref, lam_ref, hf_ref,
         o_ref, xe_ref, a_ref, s_ref, hb_ref, h_ref) = refs
    else:
        (u_ref, prev_ref, next_ref, cw_ref, cb_ref, wg_ref, bg_ref, lam_ref,
         o_ref, xe_ref, a_ref, s_ref, h_ref) = refs
    i = pl.program_id(0)
    blk = _bwd_block(i, nblk) if reverse else i

    @pl.when(i == 0)
    def _():
        h_ref[...] = jnp.zeros_like(h_ref)

    for bb in range(n_batch):
        ub = u_ref[bb]
        xc = _short_conv(xe_ref.at[bb], ub[:, 0:BRANCH], prev_ref[bb][:, 0:BRANCH], next_ref[bb][:, 0:BRANCH],
                         blk, nblk, cw_ref[0], cb_ref[0])
        z = _dot(xc.astype(BF16), wg_ref[0, 0]) + bg_ref[0, 0]
        rg = jax.nn.sigmoid(z[:, 0:BRANCH])
        ig = jax.nn.sigmoid(z[:, BRANCH:2 * BRANCH])
        log_a = -LRU_C * rg * jax.nn.softplus(-lam_ref[0, 0])
        a = jnp.exp(log_a)
        a_ref[bb] = a
        s_ref[bb] = jnp.sqrt(jnp.tanh(-log_a) * (1.0 + a * a)) * (ig * xc)

    def group(gi, hs):
        base = pl.multiple_of((BLK // 8 - 1 - gi if reverse else gi) * 8, 8)
        hs = list(hs)
        for rr in range(8):
            t = base + (7 - rr if reverse else rr)
            for bb in range(n_batch):
                hs[bb] = a_ref[bb, pl.ds(t, 1), :] * hs[bb] + s_ref[bb, pl.ds(t, 1), :]
                if reverse:
                    hb_ref[bb, pl.ds(t, 1), :] = hs[bb]
                else:
                    o_ref[bb, pl.ds(t, 1), :] = hs[bb]
        return tuple(hs)

    hs = lax.fori_loop(0, BLK // 8, group, tuple(h_ref[bb] for bb in range(n_batch)))
    for bb in range(n_batch):
        h_ref[bb] = hs[bb]
        if reverse:
            o_ref[bb] = ((hf_ref[bb] + hb_ref[bb]) * _silu(u_ref[bb][:, BRANCH:2 * BRANCH])).astype(BF16)


def _lru_call(u, layer, cw, cb, wg, bg, lam, reverse, h_fwd=None):
    b, rows, _ = u.shape
    nblk = rows // BLK
    main, prev, nxt = _pass_specs(nblk, reverse, b, W_LRU)
    dr = 1 if reverse else 0
    in_specs = [pl.BlockSpec((b, BLK, W_LRU), main), prev, nxt,
                _layer_spec(cw.shape, layer, 1), _layer_spec(cb.shape, layer, 1),
                pl.BlockSpec((1, 1, BRANCH, 2 * BRANCH), lambda i: (layer, dr, 0, 0)),
                pl.BlockSpec((1, 1, 1, 2 * BRANCH), lambda i: (layer, dr, 0, 0)),
                pl.BlockSpec((1, 1, 1, BRANCH), lambda i: (layer, dr, 0, 0))]
    args = [u, u, u, cw, cb, wg, bg, lam]
    scratch = [pltpu.VMEM((b, BLK + 2 * HALO, BRANCH), F32), pltpu.VMEM((b, BLK, BRANCH), F32),
               pltpu.VMEM((b, BLK, BRANCH), F32)]
    if reverse:
        in_specs.append(pl.BlockSpec((b, BLK, BRANCH), main))
        args.append(h_fwd)
        scratch.append(pltpu.VMEM((b, BLK, BRANCH), F32))
    scratch.append(pltpu.VMEM((b, 1, BRANCH), F32))
    return pl.pallas_call(
        functools.partial(_lru_body, reverse=reverse, nblk=nblk, n_batch=b),
        grid=(nblk,),
        in_specs=in_specs,
        out_specs=pl.BlockSpec((b, BLK, BRANCH), main),
        out_shape=jax.ShapeDtypeStruct((b, rows, BRANCH), BF16 if reverse else F32),
        scratch_shapes=scratch,
        compiler_params=_params(1),
        name="lru_bwd" if reverse else "lru_fwd",
    )(*args)


def _ssd_blocks(ubs, xcs, dtb, aneg, sts, reverse):
    nb = len(ubs)
    off = SSD_HEADS if reverse else 0
    ch = SSD_CHUNK
    hpg = SSD_HEADS // SSD_GROUPS
    r = _iota((ch, ch), 0)
    c = _iota((ch, ch), 1)
    keep = (c >= r) if reverse else (c <= r)
    tri = keep.astype(BF16)
    expand = (_iota((128, BRANCH), 0) == off + (_iota((128, BRANCH), 1) >> 6)).astype(BF16)
    lane_n = _iota((ch, 128), 1) >> 6
    lane_p = _iota((ch, BRANCH), 1) >> 6
    bd = (_iota((128, BRANCH), 0) >> 6) == (_iota((128, BRANCH), 1) >> 7)
    edge = 0 if reverse else ch - 1

    n_chunks = BLK // ch
    order = list(reversed(range(n_chunks))) if reverse else list(range(n_chunks))
    cb_keys = [(ci, bb) for ci in order for bb in range(nb)]
    rows = lambda a, ci: a[ci * ch:(ci + 1) * ch]
    xs = {k: rows(xcs[k[1]][:, 0:256], k[0]) for k in cb_keys}
    bm = {k: rows(xcs[k[1]][:, 256:384], k[0]).astype(BF16) for k in cb_keys}
    cm = {k: rows(xcs[k[1]][:, 384:512], k[0]) for k in cb_keys}
    dt = {k: jax.nn.softplus(rows(ubs[k[1]][:, 768:896], k[0]) + dtb) for k in cb_keys}

    cum = {k: _dot_exact_left(tri, dt[k] * aneg) for k in cb_keys}
    cum_x = {k: _dot_exact_right(cum[k], expand) for k in cb_keys}
    dt_x = {k: _dot_exact_right(dt[k], expand, terms=2) for k in cb_keys}
    cum_t = {k: cum[k].T for k in cb_keys}
    dt_t = {k: dt[k].T for k in cb_keys}
    cbm = {(k, g): _dot_nt(jnp.where(lane_n == g, cm[k], 0.0).astype(BF16), bm[k])
           for k in cb_keys for g in range(SSD_GROUPS)}

    def decay_mix(k, h):
        seg = cum[k][:, off + h:off + h + 1] - cum_t[k][off + h:off + h + 1, :]
        lm = jnp.exp(jnp.where(keep, seg, -1e30))
        return (cbm[(k, h // hpg)] * lm * dt_t[k][off + h:off + h + 1, :]).astype(BF16)

    y_in = {}
    for k in cb_keys:
        m_cat = jnp.concatenate([decay_mix(k, h) for h in range(SSD_HEADS)], axis=1)
        x_bd = jnp.concatenate([jnp.where(lane_p == h, xs[k], 0.0) for h in range(SSD_HEADS)], axis=0)
        y_in[k] = _dot(m_cat, x_bd.astype(BF16))
    tot_x = {k: cum_x[k][edge:edge + 1, :] for k in cb_keys}
    upd = {k: jnp.where(bd, _dot_tn(bm[k], (xs[k] * (jnp.exp(tot_x[k] - cum_x[k]) * dt_x[k])).astype(BF16)), 0.0)
           for k in cb_keys}
    sts = list(sts)
    outs = [[None] * n_chunks for _ in range(nb)]
    for k in cb_keys:
        ci, bb = k
        outs[bb][ci] = y_in[k] + _dot(cm[k].astype(BF16), sts[bb].astype(BF16)) * jnp.exp(cum_x[k])
        sts[bb] = sts[bb] * jnp.exp(tot_x[k]) + upd[k]
    return [jnp.concatenate(o, axis=0) for o in outs], sts


def _ssd_body(*refs, reverse, nblk, n_batch):
    if reverse:
        (u_ref, prev_ref, next_ref, cw_ref, cb_ref, dtb_ref, an_ref, yf_ref, dsk_ref, nw_ref,
         o_ref, xe_ref, st_ref) = refs
    else:
        u_ref, prev_ref, next_ref, cw_ref, cb_ref, dtb_ref, an_ref, o_ref, xe_ref, st_ref = refs
    i = pl.program_id(0)
    blk = _bwd_block(i, nblk) if reverse else i

    @pl.when(i == 0)
    def _():
        st_ref[...] = jnp.zeros_like(st_ref)

    ubs = [u_ref[bb] for bb in range(n_batch)]
    xcs = [_silu(_short_conv(xe_ref.at[bb], ubs[bb][:, 0:512], prev_ref[bb][:, 0:512], next_ref[bb][:, 0:512],
                             blk, nblk, cw_ref[0], cb_ref[0])) for bb in range(n_batch)]
    ys, sts = _ssd_blocks(ubs, xcs, dtb_ref[0], an_ref[0], [st_ref[bb] for bb in range(n_batch)], reverse)
    for bb in range(n_batch):
        st_ref[bb] = sts[bb]
        if reverse:
            y = ys[bb] + yf_ref[bb] + dsk_ref[0] * xcs[bb][:, 0:256]
            yz = y * _silu(ubs[bb][:, 512:768])
            ms = jnp.mean(yz * yz, axis=-1, keepdims=True)
            o_ref[bb] = (yz * lax.rsqrt(ms + EPS) * nw_ref[0]).astype(BF16)
        else:
            o_ref[bb] = ys[bb]


def _ssd_call(u, layer, cw, cb, dtb, aneg, reverse, y_fwd=None, dskip=None, nw=None):
    b, rows, _ = u.shape
    nblk = rows // BLK
    main, prev, nxt = _pass_specs(nblk, reverse, b, W_SSD)
    in_specs = [pl.BlockSpec((b, BLK, W_SSD), main), prev, nxt,
                _layer_spec(cw.shape, layer, 1), _layer_spec(cb.shape, layer, 1),
                _layer_spec(dtb.shape, layer, 1), _layer_spec(aneg.shape, layer, 1)]
    args = [u, u, u, cw, cb, dtb, aneg]
    if reverse:
        in_specs += [pl.BlockSpec((b, BLK, BRANCH), main),
                     _layer_spec(dskip.shape, layer, 1), _layer_spec(nw.shape, layer, 1)]
        args += [y_fwd, dskip, nw]
    return pl.pallas_call(
        functools.partial(_ssd_body, reverse=reverse, nblk=nblk, n_batch=b),
        grid=(nblk,),
        in_specs=in_specs,
        out_specs=pl.BlockSpec((b, BLK, BRANCH), main),
        out_shape=jax.ShapeDtypeStruct((b, rows, BRANCH), BF16 if reverse else F32),
        scratch_shapes=[pltpu.VMEM((b, BLK + 2 * HALO, 512), F32), pltpu.VMEM((b, 128, BRANCH), F32)],
        compiler_params=_params(1),
        name="ssd_bwd" if reverse else "ssd_fwd",
    )(*args)


def _key_step(nk):
    return max(t for t in range(BLK, min(ATT_TK, nk) + 1, BLK) if nk % t == 0)


def _attn_body(qt_ref, k_ref, vt_ref, g_ref, lam_ref, sw_ref, o_ref, s0_ref, s1_ref, *, nk, lam_init):
    tk = _key_step(nk)
    n_steps = nk // tk
    sub = tk // BLK
    fold = 32
    lv = lam_ref[0]
    lam = (jnp.exp(jnp.sum(lv[0:1] * lv[1:2], axis=-1, keepdims=True))
           - jnp.exp(jnp.sum(lv[2:3] * lv[3:4], axis=-1, keepdims=True)) + lam_init)
    row = _iota((128, BLK), 0)
    pairs = [(h, half) for h in range(DIFF_HEADS) for half in range(2)]
    s_bufs = (s0_ref, s1_ref)

    def masked_qt(p):
        h, half = pairs[p]
        qt = qt_ref[0, 0, (h // 2) * 128:(h // 2 + 1) * 128, :]
        lo = (h % 2) * 64 + half * DIFF_D
        return jnp.where(jnp.logical_and(row >= lo, row < lo + DIFF_D), qt, jnp.zeros_like(qt))

    def key_start(j):
        return j * tk if isinstance(j, int) else pl.multiple_of(j * tk, BLK)

    def stage_a(p, qm, j, jj, mx):
        tile = pairs[p][0] // 2
        rows = pl.ds(key_start(j) + jj * BLK, BLK)
        s = _dot(k_ref[0, rows, tile * 128:(tile + 1) * 128], qm)
        s_bufs[p % 2][rows, :] = s
        return jnp.maximum(mx, jnp.max(s.reshape(BLK // fold, fold, BLK), axis=0))

    def stage_bc(p, m, j, jj, ls, acc):
        h = pairs[p][0]
        e = jnp.exp2(s_bufs[p % 2][pl.ds(key_start(j) + jj * BLK, BLK), :] - m)
        ls = ls + jnp.sum(e.reshape(BLK // fold, fold, BLK), axis=0)
        acc = acc + _dot(vt_ref[0, j * sub + jj, h * DIFF_DV:(h + 1) * DIFF_DV, :], e.astype(BF16))
        return ls, acc

    results = []
    m_prev = None
    for p in range(len(pairs) + 1):
        do_a = p < len(pairs)
        do_bc = p >= 1
        qm = masked_qt(p) if do_a else None

        def step(j, carry, p=p, qm=qm, m_prev=m_prev, do_a=do_a, do_bc=do_bc):
            mx, ls, acc = carry
            for jj in range(sub):
                if do_bc:
                    ls, acc = stage_bc(p - 1, m_prev, j, jj, ls, acc)
                if do_a:
                    mx = stage_a(p, qm, j, jj, mx)
            return mx, ls, acc

        carry = (jnp.full((fold, BLK), -jnp.inf, F32), jnp.zeros((fold, BLK), F32),
                 jnp.zeros((DIFF_DV, BLK), F32))
        if n_steps > 1:
            mx, ls, acc = lax.fori_loop(0, n_steps, step, carry)
        else:
            mx, ls, acc = step(0, carry)
        if do_bc:
            results.append(acc / jnp.sum(ls, axis=0, keepdims=True))
        if do_a:
            m_prev = jnp.max(mx, axis=0, keepdims=True)

    heads = [results[2 * h] - lam * results[2 * h + 1] for h in range(DIFF_HEADS)]
    o = jnp.concatenate(heads, axis=0).T
    grp = ((_iota((BRANCH, BRANCH), 0) >> 6) == (_iota((BRANCH, BRANCH), 1) >> 6)).astype(BF16)
    ms = _dot_exact_right(o * o, grp, terms=2) * (1.0 / DIFF_DV)
    y = o * lax.rsqrt(ms + EPS) * sw_ref[0] * (1.0 - lam_init)
    o_ref[0] = (y * _silu(g_ref[0])).astype(BF16)


def _attn_call(qt, k, vt, gate, layer, lam_vecs, sw, lam_init, ctx_only):
    b, rows, _ = k.shape
    nblk = rows // BLK
    nk = BLK if ctx_only else rows
    nq = 1 if ctx_only else nblk - 1
    qoff = 0 if ctx_only else 1
    return pl.pallas_call(
        functools.partial(_attn_body, nk=nk, lam_init=lam_init),
        grid=(b, nq),
        in_specs=[pl.BlockSpec((1, 1, BRANCH, BLK), lambda bb, i: (bb, i + qoff, 0, 0)),
                  pl.BlockSpec((1, nk, BRANCH), lambda bb, i: (bb, 0, 0)),
                  pl.BlockSpec((1, nk // BLK, BRANCH, BLK), lambda bb, i: (bb, 0, 0, 0)),
                  pl.BlockSpec((1, BLK, BRANCH), lambda bb, i: (bb, i + qoff, 0)),
                  _layer_spec(lam_vecs.shape, layer, 2), _layer_spec(sw.shape, layer, 2)],
        out_specs=pl.BlockSpec((1, BLK, BRANCH), lambda bb, i: (bb, i, 0)),
        out_shape=jax.ShapeDtypeStruct((b, nq * BLK, BRANCH), BF16),
        scratch_shapes=[pltpu.VMEM((nk, BLK), F32), pltpu.VMEM((nk, BLK), F32)],
        compiler_params=_params(2),
        name="diff_attn_ctx" if ctx_only else "diff_attn",
    )(qt, k, vt, gate, lam_vecs, sw)


def _pad_cols(parts, width):
    n = sum(p.shape[-1] for p in parts)
    if n < width:
        parts = parts + [jnp.zeros(parts[0].shape[:-1] + (width - n,), parts[0].dtype)]
    return jnp.concatenate(parts, axis=-1)


def _layout_w_in(w):
    a, b_, c_, d_ = 0, 800, 1312, 2336
    gla = _pad_cols([w[..., a:a + 512], w[..., a + 544:a + 800], w[..., a + 512:a + 544]], W_GLA)
    lru = w[..., b_:b_ + 512]
    diff = w[..., c_:c_ + 1024]
    ssd = _pad_cols([w[..., d_:d_ + 512], w[..., d_ + 520:d_ + 776], w[..., d_ + 512:d_ + 520]], W_SSD)
    return jnp.concatenate([gla, lru, diff, ssd], axis=-1).astype(BF16)


def _block_diag(w):
    nb, n = w.shape[-3], w.shape[-1]
    eye = jnp.eye(nb, dtype=w.dtype)
    dense = eye[:, None, :, None] * w[..., :, :, None, :]
    return dense.reshape(w.shape[:-3] + (nb * n, nb * n))


def _rope_tables(t_lat):
    n_freq = DIFF_D // 4
    inv = ROPE_BASE ** (-jnp.arange(n_freq, dtype=F32) / n_freq)
    tpos = jnp.arange(t_lat)
    ang_r = (tpos // GRID_W).astype(F32)[:, None] * inv
    ang_c = (tpos % GRID_W).astype(F32)[:, None] * inv
    ang = jnp.concatenate([ang_r, ang_r, ang_c, ang_c], axis=-1)
    reps = BRANCH // DIFF_D
    sign = jnp.where((jnp.arange(DIFF_D) % 16) < 8, -1.0, 1.0).astype(F32)
    return jnp.tile(jnp.cos(ang), (1, reps)), jnp.tile(jnp.sin(ang) * sign, (1, reps))


def kernel(x, c, ctx, c_ctx, w_mod, b_mod, norm_w, w_in, w_out, gla_w2, gla_b2, gla_norm_w, lru_conv_w, lru_conv_b, lru_wa, lru_ba, lru_wx, lru_bx, lru_lam, diff_lam, diff_subln_w, ssd_conv_w, ssd_conv_b, ssd_dt_bias, ssd_a_log, ssd_d, ssd_norm_w, final_norm_w):
    bsz, t_lat, d = x.shape
    depth = w_mod.shape[0]
    assert ctx.shape[1] == BLK and t_lat % BLK == 0 and d == D_MODEL and bsz + 1 <= 8
    nblk = 1 + t_lat // BLK

    cvec = jnp.concatenate([c, c_ctx[None], jnp.zeros((8 - bsz - 1, d), F32)], axis=0)
    mod = _mod_call(cvec, w_mod, b_mod)
    cos, sin = _rope_tables(t_lat)

    nw3 = norm_w[:, None, :]
    w_in_p = _layout_w_in(w_in)
    w_out_b = w_out.astype(BF16)
    pad_r = lambda w, lo: jnp.pad(w, ((0, 0), (lo, 128 - GLA_RANK - lo), (0, 0)))
    w2 = jnp.stack([pad_r(gla_w2[:, 0], 0), pad_r(gla_w2[:, 1], GLA_RANK)], axis=1).astype(BF16)
    b2 = gla_b2[:, :, None, :]
    gnw = jnp.tile(gla_norm_w, (1, GLA_HEADS))[:, None, :]
    wg = jnp.concatenate([_block_diag(lru_wa), _block_diag(lru_wx)], axis=-1).astype(BF16)
    bg = jnp.concatenate([lru_ba, lru_bx], axis=-1)[:, :, None, :]
    llam = lru_lam[:, :, None, :]
    lcb = lru_conv_b[:, None, :]
    sw = jnp.tile(diff_subln_w, (1, DIFF_HEADS))[:, None, :]
    pad_l = lambda v: jnp.pad(v.reshape(depth, -1), ((0, 0), (0, 128 - 2 * SSD_HEADS)))[:, None, :]
    dtb = pad_l(ssd_dt_bias)
    aneg = pad_l(-jnp.exp(ssd_a_log))
    dsk = jnp.repeat(ssd_d, SSD_P, axis=-1)[:, None, :]
    scb = ssd_conv_b[:, None, :]
    snw = ssd_norm_w[:, None, :]

    lat, lat_off, cx = x, 0, ctx
    for l in range(depth):
        last = l == depth - 1
        lam_init = 0.8 - 0.6 * math.exp(-0.3 * l)
        ua, ub, ud, qt, kp, vt, gd = _inproj_call(lat, lat_off, cx, l, nblk, nw3, mod, w_in_p, cos, sin)

        of = _gla_call(ua, l, w2, b2, False)
        ya = _gla_call(ua, l, w2, b2, True, of, gnw)

        hf = _lru_call(ub, l, lru_conv_w, lcb, wg, bg, llam, False)
        yb = _lru_call(ub, l, lru_conv_w, lcb, wg, bg, llam, True, hf)

        yc_lat = _attn_call(qt, kp, vt, gd, l, diff_lam, sw, lam_init, False)
        yc_ctx = yc_lat if last else _attn_call(qt, kp, vt, gd, l, diff_lam, sw, lam_init, True)

        yf = _ssd_call(ud, l, ssd_conv_w, scb, dtb, aneg, False)
        yd = _ssd_call(ud, l, ssd_conv_w, scb, dtb, aneg, True, yf, dsk, snw)

        h = _outproj_call(ya, yb, yc_lat, yc_ctx, yd, lat, lat_off, cx, l, nblk, mod, w_out_b,
                          final_norm_w[None], last)
        lat, lat_off, cx = h, 1, h
    return h
```

```python
import functools
import math

import jax
import jax.numpy as jnp
from jax import lax
from jax.experimental import pallas as pl
from jax.experimental.pallas import tpu as pltpu

F32 = jnp.float32
BF16 = jnp.bfloat16

D_MODEL = 1024
BRANCH = D_MODEL // 4
GRID_W = 64
EPS = 1e-6
CONV_W = 4
BLK = 256
HALO = 8
GLA_HEADS, GLA_DK, GLA_DV = 4, 32, 64
GLA_RANK = 16
GLA_TAU = 16.0
GLA_CHUNK = 64
LRU_C = 8.0
DIFF_HEADS, DIFF_D, DIFF_DV = 4, 32, 64
ROPE_BASE = 10000.0
SSD_HEADS, SSD_P, SSD_N, SSD_GROUPS = 4, 64, 64, 2
SSD_CHUNK = 128
ATT_TK = 2816
W_GLA, W_LRU, W_DIFF, W_SSD = 896, 512, 1024, 896
COL_GLA, COL_LRU, COL_DIFF, COL_SSD = 0, 800, 1312, 2336
W_ALL = W_GLA + W_LRU + W_DIFF + W_SSD
VMEM_LIMIT = 56 * 1024 * 1024


def _dot(a, b):
    return jnp.dot(a, b, preferred_element_type=F32)


def _dot_nt(a, b):
    return lax.dot_general(a, b, (((1,), (1,)), ((), ())), preferred_element_type=F32)


def _dot_tn(a, b):
    return lax.dot_general(a, b, (((0,), (0,)), ((), ())), preferred_element_type=F32)


def _split(x, terms):
    out = []
    r = x
    for _ in range(terms):
        p = r.astype(BF16)
        out.append(p)
        r = r - p.astype(F32)
    return out


def _dot_exact_left(m, x, terms=3):
    return sum(_dot(m, p) for p in _split(x, terms))


def _dot_exact_right(x, m, terms=3):
    return sum(_dot(p, m) for p in _split(x, terms))


def _iota(shape, dim):
    return lax.broadcasted_iota(jnp.int32, shape, dim)


def _silu(x):
    return x * jax.nn.sigmoid(x)


def _params(n_axes):
    return pltpu.CompilerParams(dimension_semantics=("arbitrary",) * n_axes, vmem_limit_bytes=VMEM_LIMIT)


def _layer_spec(shape, layer, n_axes):
    block = (1,) + tuple(shape[1:])
    idx = (layer,) + (0,) * (len(shape) - 1)
    if n_axes == 1:
        return pl.BlockSpec(block, lambda i: idx)
    return pl.BlockSpec(block, lambda bb, i: idx)


def _bwd_block(i, nblk):
    return jnp.where(i == 0, 0, nblk - i)


def _mod_body(c_ref, w_ref, b_ref, o_ref):
    s = _silu(c_ref[...])
    o_ref[0] = _dot(s.astype(BF16), w_ref[0].astype(BF16)) + b_ref[0]


def _mod_call(cvec, w_mod, b_mod):
    depth, d, n = w_mod.shape
    tn = 1024
    return pl.pallas_call(
        _mod_body,
        grid=(depth, n // tn),
        in_specs=[pl.BlockSpec((8, d), lambda l, j: (0, 0)),
                  pl.BlockSpec((1, d, tn), lambda l, j: (l, 0, j)),
                  pl.BlockSpec((1, 1, tn), lambda l, j: (l, 0, j))],
        out_specs=pl.BlockSpec((1, 8, tn), lambda l, j: (l, 0, j)),
        out_shape=jax.ShapeDtypeStruct((depth, 8, n), F32),
        compiler_params=_params(2),
        name="adaln_mod",
    )(cvec, w_mod, b_mod.reshape(depth, 1, n))


def _mod_row(mod_ref, is_ctx, bb, n_batch, part):
    r = jnp.where(is_ctx, n_batch, bb)
    return mod_ref[0, pl.ds(r, 1), part * D_MODEL:(part + 1) * D_MODEL]


def _inproj_body(lat_ref, ctx_ref, nw_ref, mod_ref, w_ref, cos_ref, sin_ref,
                 oa_ref, ob_ref, od_ref, qt_ref, k_ref, vt_ref, g_ref, *, n_batch):
    bb = pl.program_id(0)
    i = pl.program_id(1)
    is_ctx = i == 0
    x = jnp.where(is_ctx, ctx_ref[0], lat_ref[0])
    ms = jnp.mean(x * x, axis=-1, keepdims=True)
    xn = x * lax.rsqrt(ms + EPS) * nw_ref[0]
    xm = xn * (1.0 + _mod_row(mod_ref, is_ctx, bb, n_batch, 1)) + _mod_row(mod_ref, is_ctx, bb, n_batch, 0)
    u = _dot(xm.astype(BF16), w_ref[0])
    oa_ref[0] = u[:, 0:W_GLA]
    ob_ref[0] = u[:, W_GLA:W_GLA + W_LRU]
    od_ref[0] = u[:, W_GLA + W_LRU + W_DIFF:W_ALL]

    ud = u[:, W_GLA + W_LRU:W_GLA + W_LRU + W_DIFF]
    cs = jnp.where(is_ctx, 1.0, cos_ref[...])
    sn = jnp.where(is_ctx, 0.0, sin_ref[...])
    first = (_iota((BLK, BRANCH), 1) & 15) < 8

    def rope(t):
        rot = jnp.where(first, pltpu.roll(t, BRANCH - 8, 1), pltpu.roll(t, 8, 1))
        return t * cs + rot * sn

    qt_ref[0, 0] = (rope(ud[:, 0:256]) * (DIFF_D ** -0.5 * math.log2(math.e))).T.astype(BF16)
    k_ref[0] = rope(ud[:, 256:512]).astype(BF16)
    vt_ref[0, 0] = ud[:, 512:768].T.astype(BF16)
    g_ref[0] = ud[:, 768:1024]


def _inproj_call(lat, lat_off, ctx, layer, nblk, nw, mod, w, cos, sin):
    b = lat.shape[0]
    d = D_MODEL
    rows = nblk * BLK
    row_out = lambda n: pl.BlockSpec((1, BLK, n), lambda bb, i: (bb, i, 0))
    t_out = pl.BlockSpec((1, 1, BRANCH, BLK), lambda bb, i: (bb, i, 0, 0))
    tab = pl.BlockSpec((BLK, BRANCH), lambda bb, i: (jnp.maximum(i - 1, 0), 0))
    return pl.pallas_call(
        functools.partial(_inproj_body, n_batch=b),
        grid=(b, nblk),
        in_specs=[pl.BlockSpec((1, BLK, d), lambda bb, i: (bb, jnp.maximum(i - 1, 0) + lat_off, 0)),
                  pl.BlockSpec((1, BLK, d), lambda bb, i: (bb, 0, 0)),
                  _layer_spec(nw.shape, layer, 2), _layer_spec(mod.shape, layer, 2), _layer_spec(w.shape, layer, 2),
                  tab, tab],
        out_specs=[row_out(W_GLA), row_out(W_LRU), row_out(W_SSD), t_out, row_out(BRANCH), t_out, row_out(BRANCH)],
        out_shape=[jax.ShapeDtypeStruct((b, rows, W_GLA), F32),
                   jax.ShapeDtypeStruct((b, rows, W_LRU), F32),
                   jax.ShapeDtypeStruct((b, rows, W_SSD), F32),
                   jax.ShapeDtypeStruct((b, nblk, BRANCH, BLK), BF16),
                   jax.ShapeDtypeStruct((b, rows, BRANCH), BF16),
                   jax.ShapeDtypeStruct((b, nblk, BRANCH, BLK), BF16),
                   jax.ShapeDtypeStruct((b, rows, BRANCH), F32)],
        compiler_params=_params(2),
        name="norm_inproj",
    )(lat, ctx, nw, mod, w, cos, sin)


def _outproj_body(ya_ref, yb_ref, ycl_ref, ycc_ref, yd_ref, lat_ref, ctx_ref, mod_ref, w_ref, fw_ref, o_ref,
                  *, final, n_batch):
    bb = pl.program_id(0)
    i = pl.program_id(1)
    if final:
        is_ctx = False
        yc, res = ycl_ref[0], lat_ref[0]
    else:
        is_ctx = i == 0
        yc = jnp.where(is_ctx, ycc_ref[0], ycl_ref[0])
        res = jnp.where(is_ctx, ctx_ref[0], lat_ref[0])
    y = jnp.concatenate([ya_ref[0], yb_ref[0], yc, yd_ref[0]], axis=-1)
    z = _dot(y, w_ref[0])
    hn = res + _mod_row(mod_ref, is_ctx, bb, n_batch, 2) * z
    if final:
        ms = jnp.mean(hn * hn, axis=-1, keepdims=True)
        hn = hn * lax.rsqrt(ms + EPS) * fw_ref[...]
    o_ref[0] = hn


def _outproj_call(ya, yb, yc_lat, yc_ctx, yd, lat, lat_off, ctx, layer, nblk, mod, w, fw, final):
    b = lat.shape[0]
    d = D_MODEL
    off = 1 if final else 0
    nout = nblk - off
    y_spec = pl.BlockSpec((1, BLK, BRANCH), lambda bb, i: (bb, i + off, 0))
    lat_blk = (lambda i: i) if final else (lambda i: jnp.maximum(i - 1, 0))
    return pl.pallas_call(
        functools.partial(_outproj_body, final=final, n_batch=b),
        grid=(b, nout),
        in_specs=[y_spec, y_spec,
                  pl.BlockSpec((1, BLK, BRANCH), lambda bb, i: (bb, lat_blk(i), 0)),
                  pl.BlockSpec((1, BLK, BRANCH), lambda bb, i: (bb, 0, 0)),
                  y_spec,
                  pl.BlockSpec((1, BLK, d), lambda bb, i: (bb, lat_blk(i) + lat_off, 0)),
                  pl.BlockSpec((1, BLK, d), lambda bb, i: (bb, 0, 0)),
                  _layer_spec(mod.shape, layer, 2), _layer_spec(w.shape, layer, 2),
                  pl.BlockSpec((1, d), lambda bb, i: (0, 0))],
        out_specs=pl.BlockSpec((1, BLK, d), lambda bb, i: (bb, i, 0)),
        out_shape=jax.ShapeDtypeStruct((b, nout * BLK, d), F32),
        compiler_params=_params(2),
        name="outproj_final" if final else "outproj",
    )(ya, yb, yc_lat, yc_ctx, yd, lat, ctx, mod, w, fw)


def _pass_specs(nblk, reverse, n_batch, width):
    def blk_of(i):
        return _bwd_block(i, nblk) if reverse else i
    per_blk = BLK // HALO
    main = lambda i: (0, blk_of(i), 0)
    prev = pl.BlockSpec((n_batch, HALO, width), lambda i: (0, jnp.maximum(blk_of(i) * per_blk - 1, 0), 0))
    nxt = pl.BlockSpec((n_batch, HALO, width),
                       lambda i: (0, jnp.minimum((blk_of(i) + 1) * per_blk, nblk * per_blk - 1), 0))
    return main, prev, nxt


def _short_conv(xe_ref, x, prev, nxt, blk, nblk, cw, cb):
    prev_ok = blk >= 2
    next_ok = jnp.logical_and(blk >= 1, blk <= nblk - 2)
    xe_ref[0:HALO, :] = jnp.where(prev_ok, prev, 0.0)
    xe_ref[HALO:HALO + BLK, :] = x
    xe_ref[HALO + BLK:2 * HALO + BLK, :] = jnp.where(next_ok, nxt, 0.0)
    left = CONV_W // 2
    y = cb + xe_ref[HALO - left:HALO - left + BLK, :] * cw[0:1, :]
    for j in range(1, CONV_W):
        y = y + xe_ref[HALO - left + j:HALO - left + j + BLK, :] * cw[j:j + 1, :]
    return y


def _gla_blocks(ubs, w2, b2, sts, reverse):
    nb = len(ubs)
    r = _iota((BLK, BLK), 0)
    c = _iota((BLK, BLK), 1)
    same = (r >> 6) == (c >> 6)
    tri = jnp.logical_and(same, (c >= r) if reverse else (c <= r))
    sel = jnp.concatenate([tri, same], axis=0).astype(BF16)
    lane_k = _iota((GLA_CHUNK, 128), 1) >> 5
    rr = _iota((GLA_HEADS * GLA_CHUNK, GLA_CHUNK), 0) & (GLA_CHUNK - 1)
    cc = _iota((GLA_HEADS * GLA_CHUNK, GLA_CHUNK), 1)
    amask = (cc >= rr) if reverse else (cc <= rr)
    lane_v = _iota((GLA_CHUNK, BRANCH), 1) >> 6
    bd = (_iota((BRANCH, 128), 0) >> 6) == (_iota((BRANCH, 128), 1) >> 5)

    las = [jax.nn.log_sigmoid(_dot(ub[:, 768:896].astype(BF16), w2) + b2) * (1.0 / GLA_TAU) for ub in ubs]
    ggs = [_dot_exact_left(sel, la) for la in las]
    gs = [gg[0:BLK] for gg in ggs]
    gls = [gg[BLK:2 * BLK] for gg in ggs]
    qgs = [ub[:, 0:128] * (GLA_DK ** -0.5) * jnp.exp(g) for ub, g in zip(ubs, gs)]
    kgs = [(ub[:, 128:256] * jnp.exp(-g)).astype(BF16) for ub, g in zip(ubs, gs)]
    kds = [(ub[:, 128:256] * jnp.exp(gl - g)).astype(BF16) for ub, g, gl in zip(ubs, gs, gls)]
    egls = [jnp.exp(gl) for gl in gls]
    vs = [ub[:, 256:512].astype(BF16) for ub in ubs]

    n_chunks = BLK // GLA_CHUNK
    order = list(reversed(range(n_chunks))) if reverse else list(range(n_chunks))
    rows = lambda a, ci: a[ci * GLA_CHUNK:(ci + 1) * GLA_CHUNK]
    cb = [(ci, bb) for ci in order for bb in range(nb)]

    def stacked_q(ci, bb):
        qg_c = rows(qgs[bb], ci)
        return jnp.concatenate([jnp.where(lane_k == h, qg_c, 0.0) for h in range(GLA_HEADS)], axis=0).astype(BF16)

    att = {k: jnp.where(amask, _dot_nt(stacked_q(*k), rows(kgs[k[1]], k[0])), 0.0).astype(BF16) for k in cb}
    res = {k: _dot(att[k], rows(vs[k[1]], k[0])) for k in cb}
    upd = {k: jnp.where(bd, _dot_tn(rows(vs[k[1]], k[0]), rows(kds[k[1]], k[0])), 0.0) for k in cb}
    sts = list(sts)
    outs = [[None] * n_chunks for _ in range(nb)]
    for ci, bb in cb:
        o_c = _dot_nt(rows(qgs[bb], ci).astype(BF16), sts[bb].astype(BF16))
        for h in range(GLA_HEADS):
            o_c = o_c + jnp.where(lane_v == h, res[(ci, bb)][h * GLA_CHUNK:(h + 1) * GLA_CHUNK], 0.0)
        sts[bb] = sts[bb] * egls[bb][ci * GLA_CHUNK:ci * GLA_CHUNK + 1, :] + upd[(ci, bb)]
        outs[bb][ci] = o_c
    return [jnp.concatenate(o, axis=0) for o in outs], sts


def _gla_body(*refs, reverse, n_batch):
    if reverse:
        u_ref, w2_ref, b2_ref, of_ref, nw_ref, o_ref, st_ref = refs
    else:
        u_ref, w2_ref, b2_ref, o_ref, st_ref = refs

    @pl.when(pl.program_id(0) == 0)
    def _():
        st_ref[...] = jnp.zeros_like(st_ref)

    ubs = [u_ref[bb] for bb in range(n_batch)]
    os_, sts = _gla_blocks(ubs, w2_ref[0, 0], b2_ref[0, 0], [st_ref[bb] for bb in range(n_batch)], reverse)
    if reverse:
        grp = ((_iota((BRANCH, BRANCH), 0) >> 6) == (_iota((BRANCH, BRANCH), 1) >> 6)).astype(BF16)
        os_ = [o + of_ref[bb] for bb, o in enumerate(os_)]
        mss = [_dot_exact_right(o * o, grp, terms=2) * (1.0 / GLA_DV) for o in os_]
        os_ = [(o * lax.rsqrt(ms + EPS) * nw_ref[0] * _silu(ub[:, 512:768])).astype(BF16)
               for o, ms, ub in zip(os_, mss, ubs)]
    for bb in range(n_batch):
        st_ref[bb] = sts[bb]
        o_ref[bb] = os_[bb]


def _gla_call(u, layer, w2, b2, reverse, o_fwd=None, nw=None):
    b, rows, _ = u.shape
    nblk = rows // BLK
    main, _, _ = _pass_specs(nblk, reverse, b, W_GLA)
    dr = 1 if reverse else 0
    in_specs = [pl.BlockSpec((b, BLK, W_GLA), main),
                pl.BlockSpec((1, 1, 128, 128), lambda i: (layer, dr, 0, 0)),
                pl.BlockSpec((1, 1, 1, 128), lambda i: (layer, dr, 0, 0))]
    args = [u, w2, b2]
    if reverse:
        in_specs += [pl.BlockSpec((b, BLK, BRANCH), main), _layer_spec(nw.shape, layer, 1)]
        args += [o_fwd, nw]
    return pl.pallas_call(
        functools.partial(_gla_body, reverse=reverse, n_batch=b),
        grid=(nblk,),
        in_specs=in_specs,
        out_specs=pl.BlockSpec((b, BLK, BRANCH), main),
        out_shape=jax.ShapeDtypeStruct((b, rows, BRANCH), BF16 if reverse else F32),
        scratch_shapes=[pltpu.VMEM((b, BRANCH, 128), F32)],
        compiler_params=_params(1),
        name="gla_bwd" if reverse else "gla_fwd",
    )(*args)


def _lru_body(*refs, reverse, nblk, n_batch):
    if reverse:
        (u_ref, prev_ref, next_ref, cw_ref, cb_ref, wg_ref, bg_ref, lam_ref, hf_ref,
         o_ref, xe_ref, a_ref, s_ref, hb_ref, h_ref) = refs
    else:
        (u_ref, prev_ref, next_ref, cw_ref, cb_ref, wg_ref, bg_ref, lam_ref,
         o_ref, xe_ref, a_ref, s_ref, h_ref) = refs
    i = pl.program_id(0)
    blk = _bwd_block(i, nblk) if reverse else i

    @pl.when(i == 0)
    def _():
        h_ref[...] = jnp.zeros_like(h_ref)

    for bb in range(n_batch):
        ub = u_ref[bb]
        xc = _short_conv(xe_ref.at[bb], ub[:, 0:BRANCH], prev_ref[bb][:, 0:BRANCH], next_ref[bb][:, 0:BRANCH],
                         blk, nblk, cw_ref[0], cb_ref[0])
        z = _dot(xc.astype(BF16), wg_ref[0, 0]) + bg_ref[0, 0]
        rg = jax.nn.sigmoid(z[:, 0:BRANCH])
        ig = jax.nn.sigmoid(z[:, BRANCH:2 * BRANCH])
        log_a = -LRU_C * rg * jax.nn.softplus(-lam_ref[0, 0])
        a = jnp.exp(log_a)
        a_ref[bb] = a
        s_ref[bb] = jnp.sqrt(jnp.tanh(-log_a) * (1.0 + a * a)) * (ig * xc)

    def group(gi, hs):
        base = pl.multiple_of((BLK // 8 - 1 - gi if reverse else gi) * 8, 8)
        hs = list(hs)
        for rr in range(8):
            t = base + (7 - rr if reverse else rr)
            for bb in range(n_batch):
                hs[bb] = a_ref[bb, pl.ds(t, 1), :] * hs[bb] + s_ref[bb, pl.ds(t, 1), :]
                if reverse:
                    hb_ref[bb, pl.ds(t, 1), :] = hs[bb]
                else:
                    o_ref[bb, pl.ds(t, 1), :] = hs[bb]
        return tuple(hs)

    hs = lax.fori_loop(0, BLK // 8, group, tuple(h_ref[bb] for bb in range(n_batch)))
    for bb in range(n_batch):
        h_ref[bb] = hs[bb]
        if reverse:
            o_ref[bb] = ((hf_ref[bb] + hb_ref[bb]) * _silu(u_ref[bb][:, BRANCH:2 * BRANCH])).astype(BF16)


def _lru_call(u, layer, cw, cb, wg, bg, lam, reverse, h_fwd=None):
    b, rows, _ = u.shape
    nblk = rows // BLK
    main, prev, nxt = _pass_specs(nblk, reverse, b, W_LRU)
    dr = 1 if reverse else 0
    in_specs = [pl.BlockSpec((b, BLK, W_LRU), main), prev, nxt,
                _layer_spec(cw.shape, layer, 1), _layer_spec(cb.shape, layer, 1),
                pl.BlockSpec((1, 1, BRANCH, 2 * BRANCH), lambda i: (layer, dr, 0, 0)),
                pl.BlockSpec((1, 1, 1, 2 * BRANCH), lambda i: (layer, dr, 0, 0)),
                pl.BlockSpec((1, 1, 1, BRANCH), lambda i: (layer, dr, 0, 0))]
    args = [u, u, u, cw, cb, wg, bg, lam]
    scratch = [pltpu.VMEM((b, BLK + 2 * HALO, BRANCH), F32), pltpu.VMEM((b, BLK, BRANCH), F32),
               pltpu.VMEM((b, BLK, BRANCH), F32)]
    if reverse:
        in_specs.append(pl.BlockSpec((b, BLK, BRANCH), main))
        args.append(h_fwd)
        scratch.append(pltpu.VMEM((b, BLK, BRANCH), F32))
    scratch.append(pltpu.VMEM((b, 1, BRANCH), F32))
    return pl.pallas_call(
        functools.partial(_lru_body, reverse=reverse, nblk=nblk, n_batch=b),
        grid=(nblk,),
        in_specs=in_specs,
        out_specs=pl.BlockSpec((b, BLK, BRANCH), main),
        out_shape=jax.ShapeDtypeStruct((b, rows, BRANCH), BF16 if reverse else F32),
        scratch_shapes=scratch,
        compiler_params=_params(1),
        name="lru_bwd" if reverse else "lru_fwd",
    )(*args)


def _ssd_blocks(ubs, xcs, dtb, aneg, sts, reverse):
    nb = len(ubs)
    off = SSD_HEADS if reverse else 0
    ch = SSD_CHUNK
    hpg = SSD_HEADS // SSD_GROUPS
    r = _iota((ch, ch), 0)
    c = _iota((ch, ch), 1)
    keep = (c >= r) if reverse else (c <= r)
    tri = keep.astype(BF16)
    expand = (_iota((128, BRANCH), 0) == off + (_iota((128, BRANCH), 1) >> 6)).astype(BF16)
    lane_n = _iota((ch, 128), 1) >> 6
    lane_p = _iota((ch, BRANCH), 1) >> 6
    bd = (_iota((128, BRANCH), 0) >> 6) == (_iota((128, BRANCH), 1) >> 7)
    edge = 0 if reverse else ch - 1

    n_chunks = BLK // ch
    order = list(reversed(range(n_chunks))) if reverse else list(range(n_chunks))
    cb_keys = [(ci, bb) for ci in order for bb in range(nb)]
    rows = lambda a, ci: a[ci * ch:(ci + 1) * ch]
    xs = {k: rows(xcs[k[1]][:, 0:256], k[0]) for k in cb_keys}
    bm = {k: rows(xcs[k[1]][:, 256:384], k[0]).astype(BF16) for k in cb_keys}
    cm = {k: rows(xcs[k[1]][:, 384:512], k[0]) for k in cb_keys}
    dt = {k: jax.nn.softplus(rows(ubs[k[1]][:, 768:896], k[0]) + dtb) for k in cb_keys}

    cum = {k: _dot_exact_left(tri, dt[k] * aneg) for k in cb_keys}
    cum_x = {k: _dot_exact_right(cum[k], expand, terms=2) for k in cb_keys}
    dt_x = {k: _dot_exact_right(dt[k], expand, terms=2) for k in cb_keys}
    cum_t = {k: cum[k].T for k in cb_keys}
    dt_t = {k: dt[k].T for k in cb_keys}
    cbm = {(k, g): _dot_nt(jnp.where(lane_n == g, cm[k], 0.0).astype(BF16), bm[k])
           for k in cb_keys for g in range(SSD_GROUPS)}

    def decay_mix(k, h):
        seg = cum[k][:, off + h:off + h + 1] - cum_t[k][off + h:off + h + 1, :]
        lm = jnp.exp(jnp.where(keep, seg, -1e30))
        return (cbm[(k, h // hpg)] * lm * dt_t[k][off + h:off + h + 1, :]).astype(BF16)

    y_in = {}
    for k in cb_keys:
        m_cat = jnp.concatenate([decay_mix(k, h) for h in range(SSD_HEADS)], axis=1)
        x_bd = jnp.concatenate([jnp.where(lane_p == h, xs[k], 0.0) for h in range(SSD_HEADS)], axis=0)
        y_in[k] = _dot(m_cat, x_bd.astype(BF16))
    tot_x = {k: cum_x[k][edge:edge + 1, :] for k in cb_keys}
    upd = {k: jnp.where(bd, _dot_tn(bm[k], (xs[k] * (jnp.exp(tot_x[k] - cum_x[k]) * dt_x[k])).astype(BF16)), 0.0)
           for k in cb_keys}
    sts = list(sts)
    outs = [[None] * n_chunks for _ in range(nb)]
    for k in cb_keys:
        ci, bb = k
        outs[bb][ci] = y_in[k] + _dot(cm[k].astype(BF16), sts[bb].astype(BF16)) * jnp.exp(cum_x[k])
        sts[bb] = sts[bb] * jnp.exp(tot_x[k]) + upd[k]
    return [jnp.concatenate(o, axis=0) for o in outs], sts


def _ssd_body(*refs, reverse, nblk, n_batch):
    if reverse:
        (u_ref, prev_ref, next_ref, cw_ref, cb_ref, dtb_ref, an_ref, yf_ref, dsk_ref, nw_ref,
         o_ref, xe_ref, st_ref) = refs
    else:
        u_ref, prev_ref, next_ref, cw_ref, cb_ref, dtb_ref, an_ref, o_ref, xe_ref, st_ref = refs
    i = pl.program_id(0)
    blk = _bwd_block(i, nblk) if reverse else i

    @pl.when(i == 0)
    def _():
        st_ref[...] = jnp.zeros_like(st_ref)

    ubs = [u_ref[bb] for bb in range(n_batch)]
    xcs = [_silu(_short_conv(xe_ref.at[bb], ubs[bb][:, 0:512], prev_ref[bb][:, 0:512], next_ref[bb][:, 0:512],
                             blk, nblk, cw_ref[0], cb_ref[0])) for bb in range(n_batch)]
    ys, sts = _ssd_blocks(ubs, xcs, dtb_ref[0], an_ref[0], [st_ref[bb] for bb in range(n_batch)], reverse)
    for bb in range(n_batch):
        st_ref[bb] = sts[bb]
        if reverse:
            y = ys[bb] + yf_ref[bb] + dsk_ref[0] * xcs[bb][:, 0:256]
            yz = y * _silu(ubs[bb][:, 512:768])
            ms = jnp.mean(yz * yz, axis=-1, keepdims=True)
            o_ref[bb] = (yz * lax.rsqrt(ms + EPS) * nw_ref[0]).astype(BF16)
        else:
            o_ref[bb] = ys[bb]


def _ssd_call(u, layer, cw, cb, dtb, aneg, reverse, y_fwd=None, dskip=None, nw=None):
    b, rows, _ = u.shape
    nblk = rows // BLK
    main, prev, nxt = _pass_specs(nblk, reverse, b, W_SSD)
    in_specs = [pl.BlockSpec((b, BLK, W_SSD), main), prev, nxt,
                _layer_spec(cw.shape, layer, 1), _layer_spec(cb.shape, layer, 1),
                _layer_spec(dtb.shape, layer, 1), _layer_spec(aneg.shape, layer, 1)]
    args = [u, u, u, cw, cb, dtb, aneg]
    if reverse:
        in_specs += [pl.BlockSpec((b, BLK, BRANCH), main),
                     _layer_spec(dskip.shape, layer, 1), _layer_spec(nw.shape, layer, 1)]
        args += [y_fwd, dskip, nw]
    return pl.pallas_call(
        functools.partial(_ssd_body, reverse=reverse, nblk=nblk, n_batch=b),
        grid=(nblk,),
        in_specs=in_specs,
        out_specs=pl.BlockSpec((b, BLK, BRANCH), main),
        out_shape=jax.ShapeDtypeStruct((b, rows, BRANCH), BF16 if reverse else F32),
        scratch_shapes=[pltpu.VMEM((b, BLK + 2 * HALO, 512), F32), pltpu.VMEM((b, 128, BRANCH), F32)],
        compiler_params=_params(1),
        name="ssd_bwd" if reverse else "ssd_fwd",
    )(*args)


def _key_step(nk):
    return max(t for t in range(BLK, min(ATT_TK, nk) + 1, BLK) if nk % t == 0)


def _attn_body(*refs, nk, lam_init, n_qb):
    qt_refs, (k_ref, vt_ref), g_refs = refs[0:n_qb], refs[n_qb:n_qb + 2], refs[n_qb + 2:2 * n_qb + 2]
    lam_ref, sw_ref, o_ref, s0_ref, s1_ref = refs[2 * n_qb + 2:]
    tk = _key_step(nk)
    n_steps = nk // tk
    sub = tk // BLK
    fold = 32
    lv = lam_ref[0]
    lam = (jnp.exp(jnp.sum(lv[0:1] * lv[1:2], axis=-1, keepdims=True))
           - jnp.exp(jnp.sum(lv[2:3] * lv[3:4], axis=-1, keepdims=True)) + lam_init)
    row = _iota((128, BLK), 0)
    pairs = [(qb, h, half) for qb in range(n_qb) for h in range(DIFF_HEADS) for half in range(2)]
    s_bufs = (s0_ref, s1_ref)

    def masked_qt(p):
        qb, h, half = pairs[p]
        qt = qt_refs[qb][0, 0, (h // 2) * 128:(h // 2 + 1) * 128, :]
        lo = (h % 2) * 64 + half * DIFF_D
        return jnp.where(jnp.logical_and(row >= lo, row < lo + DIFF_D), qt, jnp.zeros_like(qt))

    def key_start(j):
        return j * tk if isinstance(j, int) else pl.multiple_of(j * tk, BLK)

    def stage_a(p, qm, j, jj, mx):
        tile = pairs[p][1] // 2
        rows = pl.ds(key_start(j) + jj * BLK, BLK)
        s = _dot(k_ref[0, rows, tile * 128:(tile + 1) * 128], qm)
        s_bufs[p % 2][rows, :] = s
        return jnp.maximum(mx, jnp.max(s.reshape(BLK // fold, fold, BLK), axis=0))

    def stage_bc(p, m, j, jj, ls, acc):
        h = pairs[p][1]
        e = jnp.exp2(s_bufs[p % 2][pl.ds(key_start(j) + jj * BLK, BLK), :] - m)
        ls = ls + jnp.sum(e.reshape(BLK // fold, fold, BLK), axis=0)
        acc = acc + _dot(vt_ref[0, j * sub + jj, h * DIFF_DV:(h + 1) * DIFF_DV, :], e.astype(BF16))
        return ls, acc

    results = []
    m_prev = None
    for p in range(len(pairs) + 1):
        do_a = p < len(pairs)
        do_bc = p >= 1
        qm = masked_qt(p) if do_a else None

        def step(j, carry, p=p, qm=qm, m_prev=m_prev, do_a=do_a, do_bc=do_bc):
            mx, ls, acc = carry
            for jj in range(sub):
                if do_bc:
                    ls, acc = stage_bc(p - 1, m_prev, j, jj, ls, acc)
                if do_a:
                    mx = stage_a(p, qm, j, jj, mx)
            return mx, ls, acc

        carry = (jnp.full((fold, BLK), -jnp.inf, F32), jnp.zeros((fold, BLK), F32),
                 jnp.zeros((DIFF_DV, BLK), F32))
        if n_steps > 1:
            mx, ls, acc = lax.fori_loop(0, n_steps, step, carry)
        else:
            mx, ls, acc = step(0, carry)
        if do_bc:
            results.append(acc / jnp.sum(ls, axis=0, keepdims=True))
        if do_a:
            m_prev = jnp.max(mx, axis=0, keepdims=True)

    grp = ((_iota((BRANCH, BRANCH), 0) >> 6) == (_iota((BRANCH, BRANCH), 1) >> 6)).astype(BF16)
    per_qb = 2 * DIFF_HEADS
    for qb in range(n_qb):
        res = results[qb * per_qb:(qb + 1) * per_qb]
        heads = [res[2 * h] - lam * res[2 * h + 1] for h in range(DIFF_HEADS)]
        o = jnp.concatenate(heads, axis=0).T
        ms = _dot_exact_right(o * o, grp, terms=2) * (1.0 / DIFF_DV)
        y = o * lax.rsqrt(ms + EPS) * sw_ref[0] * (1.0 - lam_init)
        o_ref[0, qb * BLK:(qb + 1) * BLK, :] = (y * _silu(g_refs[qb][0])).astype(BF16)


def _attn_call(qt, k, vt, gate, layer, lam_vecs, sw, lam_init, ctx_only):
    b, rows, _ = k.shape
    nblk = rows // BLK
    nk = BLK if ctx_only else rows
    nq = 1 if ctx_only else nblk - 1
    qoff = 0 if ctx_only else 1
    n_qb = max(t for t in (1, 2, 4) if nq % t == 0)
    q_specs = [pl.BlockSpec((1, 1, BRANCH, BLK), lambda bb, i, t=t: (bb, n_qb * i + t + qoff, 0, 0))
               for t in range(n_qb)]
    g_specs = [pl.BlockSpec((1, BLK, BRANCH), lambda bb, i, t=t: (bb, n_qb * i + t + qoff, 0))
               for t in range(n_qb)]
    return pl.pallas_call(
        functools.partial(_attn_body, nk=nk, lam_init=lam_init, n_qb=n_qb),
        grid=(b, nq // n_qb),
        in_specs=q_specs + [pl.BlockSpec((1, nk, BRANCH), lambda bb, i: (bb, 0, 0)),
                            pl.BlockSpec((1, nk // BLK, BRANCH, BLK), lambda bb, i: (bb, 0, 0, 0))]
        + g_specs + [_layer_spec(lam_vecs.shape, layer, 2), _layer_spec(sw.shape, layer, 2)],
        out_specs=pl.BlockSpec((1, n_qb * BLK, BRANCH), lambda bb, i: (bb, i, 0)),
        out_shape=jax.ShapeDtypeStruct((b, nq * BLK, BRANCH), BF16),
        scratch_shapes=[pltpu.VMEM((nk, BLK), F32), pltpu.VMEM((nk, BLK), F32)],
        compiler_params=_params(2),
        name="diff_attn_ctx" if ctx_only else "diff_attn",
    )(*([qt] * n_qb), k, vt, *([gate] * n_qb), lam_vecs, sw)


def _layout_w_in(w):
    def group(lo, pieces, width):
        parts = [w[..., lo + a:lo + b] for a, b in pieces]
        used = sum(b - a for a, b in pieces)
        if used < width:
            parts.append(jnp.zeros(w.shape[:-1] + (width - used,), w.dtype))
        return parts
    parts = (group(COL_GLA, [(0, 512), (544, 800), (512, 544)], W_GLA)
             + group(COL_LRU, [(0, 512)], W_LRU)
             + group(COL_DIFF, [(0, 1024)], W_DIFF)
             + group(COL_SSD, [(0, 512), (520, 776), (512, 520)], W_SSD))
    return jnp.concatenate(parts, axis=-1)


def _block_diag(w):
    nb, n = w.shape[-3], w.shape[-1]
    eye = jnp.eye(nb, dtype=w.dtype)
    dense = eye[:, None, :, None] * w[..., :, :, None, :]
    return dense.reshape(w.shape[:-3] + (nb * n, nb * n))


def _rope_tables(t_lat):
    n_freq = DIFF_D // 4
    inv = ROPE_BASE ** (-jnp.arange(n_freq, dtype=F32) / n_freq)
    tpos = jnp.arange(t_lat)
    ang_r = (tpos // GRID_W).astype(F32)[:, None] * inv
    ang_c = (tpos % GRID_W).astype(F32)[:, None] * inv
    ang = jnp.concatenate([ang_r, ang_r, ang_c, ang_c], axis=-1)
    reps = BRANCH // DIFF_D
    sign = jnp.where((jnp.arange(DIFF_D) % 16) < 8, -1.0, 1.0).astype(F32)
    return jnp.tile(jnp.cos(ang), (1, reps)), jnp.tile(jnp.sin(ang) * sign, (1, reps))


def kernel(x, c, ctx, c_ctx, w_mod, b_mod, norm_w, w_in, w_out, gla_w2, gla_b2, gla_norm_w, lru_conv_w, lru_conv_b, lru_wa, lru_ba, lru_wx, lru_bx, lru_lam, diff_lam, diff_subln_w, ssd_conv_w, ssd_conv_b, ssd_dt_bias, ssd_a_log, ssd_d, ssd_norm_w, final_norm_w):
    bsz, t_lat, d = x.shape
    depth = w_mod.shape[0]
    assert ctx.shape[1] == BLK and t_lat % BLK == 0 and d == D_MODEL and bsz + 1 <= 8
    nblk = 1 + t_lat // BLK

    cvec = jnp.concatenate([c, c_ctx[None], jnp.zeros((8 - bsz - 1, d), F32)], axis=0)
    mod = _mod_call(cvec, w_mod, b_mod)
    cos, sin = _rope_tables(t_lat)

    nw3 = norm_w[:, None, :]
    w_in_p = _layout_w_in(w_in.astype(BF16))
    w_out_b = w_out.astype(BF16)
    pad_r = lambda w, lo: jnp.pad(w, ((0, 0), (lo, 128 - GLA_RANK - lo), (0, 0)))
    w2 = jnp.stack([pad_r(gla_w2[:, 0], 0), pad_r(gla_w2[:, 1], GLA_RANK)], axis=1).astype(BF16)
    b2 = gla_b2[:, :, None, :]
    gnw = jnp.tile(gla_norm_w, (1, GLA_HEADS))[:, None, :]
    wg = jnp.concatenate([_block_diag(lru_wa), _block_diag(lru_wx)], axis=-1).astype(BF16)
    bg = jnp.concatenate([lru_ba, lru_bx], axis=-1)[:, :, None, :]
    llam = lru_lam[:, :, None, :]
    lcb = lru_conv_b[:, None, :]
    sw = jnp.tile(diff_subln_w, (1, DIFF_HEADS))[:, None, :]
    pad_l = lambda v: jnp.pad(v.reshape(depth, -1), ((0, 0), (0, 128 - 2 * SSD_HEADS)))[:, None, :]
    dtb = pad_l(ssd_dt_bias)
    aneg = pad_l(-jnp.exp(ssd_a_log))
    dsk = jnp.repeat(ssd_d, SSD_P, axis=-1)[:, None, :]
    scb = ssd_conv_b[:, None, :]
    snw = ssd_norm_w[:, None, :]

    lat, lat_off, cx = x, 0, ctx
    for l in range(depth):
        last = l == depth - 1
        lam_init = 0.8 - 0.6 * math.exp(-0.3 * l)
        ua, ub, ud, qt, kp, vt, gd = _inproj_call(lat, lat_off, cx, l, nblk, nw3, mod, w_in_p, cos, sin)

        of = _gla_call(ua, l, w2, b2, False)
        ya = _gla_call(ua, l, w2, b2, True, of, gnw)

        hf = _lru_call(ub, l, lru_conv_w, lcb, wg, bg, llam, False)
        yb = _lru_call(ub, l, lru_conv_w, lcb, wg, bg, llam, True, hf)

        yc_lat = _attn_call(qt, kp, vt, gd, l, diff_lam, sw, lam_init, False)
        yc_ctx = yc_lat if last else _attn_call(qt, kp, vt, gd, l, diff_lam, sw, lam_init, True)

        yf = _ssd_call(ud, l, ssd_conv_w, scb, dtb, aneg, False)
        yd = _ssd_call(ud, l, ssd_conv_w, scb, dtb, aneg, True, yf, dsk, snw)

        h = _outproj_call(ya, yb, yc_lat, yc_ctx, yd, lat, lat_off, cx, l, nblk, mod, w_out_b,
                          final_norm_w[None], last)
        lat, lat_off, cx = h, 1, h
    return h
```

```python
import functools
import math

import jax
import jax.numpy as jnp
from jax import lax
from jax.experimental import pallas as pl
from jax.experimental.pallas import tpu as pltpu

F32 = jnp.float32
BF16 = jnp.bfloat16

D_MODEL = 1024
BRANCH = D_MODEL // 4
GRID_W = 64
EPS = 1e-6
CONV_W = 4
BLK = 256
HALO = 8
GLA_HEADS, GLA_DK, GLA_DV = 4, 32, 64
GLA_RANK = 16
GLA_TAU = 16.0
GLA_CHUNK = 64
LRU_C = 8.0
DIFF_HEADS, DIFF_D, DIFF_DV = 4, 32, 64
ROPE_BASE = 10000.0
SSD_HEADS, SSD_P, SSD_N, SSD_GROUPS = 4, 64, 64, 2
SSD_CHUNK = 128
ATT_TK = 2816
W_GLA, W_LRU, W_DIFF, W_SSD = 896, 512, 1024, 896
COL_GLA, COL_LRU, COL_DIFF, COL_SSD = 0, 800, 1312, 2336
W_ALL = W_GLA + W_LRU + W_DIFF + W_SSD
VMEM_LIMIT = 56 * 1024 * 1024


def _dot(a, b):
    return jnp.dot(a, b, preferred_element_type=F32)


def _dot_nt(a, b):
    return lax.dot_general(a, b, (((1,), (1,)), ((), ())), preferred_element_type=F32)


def _dot_tn(a, b):
    return lax.dot_general(a, b, (((0,), (0,)), ((), ())), preferred_element_type=F32)


def _split(x, terms):
    out = []
    r = x
    for _ in range(terms):
        p = r.astype(BF16)
        out.append(p)
        r = r - p.astype(F32)
    return out


def _dot_exact_left(m, x, terms=3):
    return sum(_dot(m, p) for p in _split(x, terms))


def _dot_exact_right(x, m, terms=3):
    return sum(_dot(p, m) for p in _split(x, terms))


def _iota(shape, dim):
    return lax.broadcasted_iota(jnp.int32, shape, dim)


def _silu(x):
    return x * jax.nn.sigmoid(x)


def _params(n_axes):
    return pltpu.CompilerParams(dimension_semantics=("arbitrary",) * n_axes, vmem_limit_bytes=VMEM_LIMIT)


def _layer_spec(shape, layer, n_axes):
    block = (1,) + tuple(shape[1:])
    idx = (layer,) + (0,) * (len(shape) - 1)
    if n_axes == 1:
        return pl.BlockSpec(block, lambda i: idx)
    return pl.BlockSpec(block, lambda bb, i: idx)


def _bwd_block(i, nblk):
    return jnp.where(i == 0, 0, nblk - i)


def _mod_body(c_ref, w_ref, b_ref, o_ref):
    s = _silu(c_ref[...])
    o_ref[0] = _dot(s.astype(BF16), w_ref[0].astype(BF16)) + b_ref[0]


def _mod_call(cvec, w_mod, b_mod):
    depth, d, n = w_mod.shape
    tn = 1024
    return pl.pallas_call(
        _mod_body,
        grid=(depth, n // tn),
        in_specs=[pl.BlockSpec((8, d), lambda l, j: (0, 0)),
                  pl.BlockSpec((1, d, tn), lambda l, j: (l, 0, j)),
                  pl.BlockSpec((1, 1, tn), lambda l, j: (l, 0, j))],
        out_specs=pl.BlockSpec((1, 8, tn), lambda l, j: (l, 0, j)),
        out_shape=jax.ShapeDtypeStruct((depth, 8, n), F32),
        compiler_params=_params(2),
        name="adaln_mod",
    )(cvec, w_mod, b_mod.reshape(depth, 1, n))


def _mod_row(mod_ref, is_ctx, bb, n_batch, part):
    r = jnp.where(is_ctx, n_batch, bb)
    return mod_ref[0, pl.ds(r, 1), part * D_MODEL:(part + 1) * D_MODEL]


def _inproj_body(lat_ref, ctx_ref, nw_ref, mod_ref, w_ref, cos_ref, sin_ref,
                 oa_ref, ob_ref, od_ref, qt_ref, k_ref, vt_ref, g_ref, *, n_batch):
    bb = pl.program_id(0)
    i = pl.program_id(1)
    is_ctx = i == 0
    x = jnp.where(is_ctx, ctx_ref[0], lat_ref[0])
    ms = jnp.mean(x * x, axis=-1, keepdims=True)
    xn = x * lax.rsqrt(ms + EPS) * nw_ref[0]
    xm = xn * (1.0 + _mod_row(mod_ref, is_ctx, bb, n_batch, 1)) + _mod_row(mod_ref, is_ctx, bb, n_batch, 0)
    u = _dot(xm.astype(BF16), w_ref[0])
    oa_ref[0] = u[:, 0:W_GLA]
    ob_ref[0] = u[:, W_GLA:W_GLA + W_LRU]
    od_ref[0] = u[:, W_GLA + W_LRU + W_DIFF:W_ALL]

    ud = u[:, W_GLA + W_LRU:W_GLA + W_LRU + W_DIFF]
    cs = jnp.where(is_ctx, 1.0, cos_ref[...])
    sn = jnp.where(is_ctx, 0.0, sin_ref[...])
    first = (_iota((BLK, BRANCH), 1) & 15) < 8

    def rope(t):
        rot = jnp.where(first, pltpu.roll(t, BRANCH - 8, 1), pltpu.roll(t, 8, 1))
        return t * cs + rot * sn

    qt_ref[0, 0] = (rope(ud[:, 0:256]) * (DIFF_D ** -0.5 * math.log2(math.e))).T.astype(BF16)
    k_ref[0] = rope(ud[:, 256:512]).astype(BF16)
    vt_ref[0, 0] = ud[:, 512:768].T.astype(BF16)
    g_ref[0] = ud[:, 768:1024]


def _inproj_call(lat, lat_off, ctx, layer, nblk, nw, mod, w, cos, sin):
    b = lat.shape[0]
    d = D_MODEL
    rows = nblk * BLK
    row_out = lambda n: pl.BlockSpec((1, BLK, n), lambda bb, i: (bb, i, 0))
    t_out = pl.BlockSpec((1, 1, BRANCH, BLK), lambda bb, i: (bb, i, 0, 0))
    tab = pl.BlockSpec((BLK, BRANCH), lambda bb, i: (jnp.maximum(i - 1, 0), 0))
    return pl.pallas_call(
        functools.partial(_inproj_body, n_batch=b),
        grid=(b, nblk),
        in_specs=[pl.BlockSpec((1, BLK, d), lambda bb, i: (bb, jnp.maximum(i - 1, 0) + lat_off, 0)),
                  pl.BlockSpec((1, BLK, d), lambda bb, i: (bb, 0, 0)),
                  _layer_spec(nw.shape, layer, 2), _layer_spec(mod.shape, layer, 2), _layer_spec(w.shape, layer, 2),
                  tab, tab],
        out_specs=[row_out(W_GLA), row_out(W_LRU), row_out(W_SSD), t_out, row_out(BRANCH), t_out, row_out(BRANCH)],
        out_shape=[jax.ShapeDtypeStruct((b, rows, W_GLA), F32),
                   jax.ShapeDtypeStruct((b, rows, W_LRU), F32),
                   jax.ShapeDtypeStruct((b, rows, W_SSD), F32),
                   jax.ShapeDtypeStruct((b, nblk, BRANCH, BLK), BF16),
                   jax.ShapeDtypeStruct((b, rows, BRANCH), BF16),
                   jax.ShapeDtypeStruct((b, nblk, BRANCH, BLK), BF16),
                   jax.ShapeDtypeStruct((b, rows, BRANCH), F32)],
        compiler_params=_params(2),
        name="norm_inproj",
    )(lat, ctx, nw, mod, w, cos, sin)


def _outproj_body(ya_ref, yb_ref, ycl_ref, ycc_ref, yd_ref, lat_ref, ctx_ref, mod_ref, w_ref, fw_ref, o_ref,
                  *, final, n_batch):
    bb = pl.program_id(0)
    i = pl.program_id(1)
    if final:
        is_ctx = False
        yc, res = ycl_ref[0], lat_ref[0]
    else:
        is_ctx = i == 0
        yc = jnp.where(is_ctx, ycc_ref[0], ycl_ref[0])
        res = jnp.where(is_ctx, ctx_ref[0], lat_ref[0])
    y = jnp.concatenate([ya_ref[0], yb_ref[0], yc, yd_ref[0]], axis=-1)
    z = _dot(y, w_ref[0])
    hn = res + _mod_row(mod_ref, is_ctx, bb, n_batch, 2) * z
    if final:
        ms = jnp.mean(hn * hn, axis=-1, keepdims=True)
        hn = hn * lax.rsqrt(ms + EPS) * fw_ref[...]
    o_ref[0] = hn


def _outproj_call(ya, yb, yc_lat, yc_ctx, yd, lat, lat_off, ctx, layer, nblk, mod, w, fw, final):
    b = lat.shape[0]
    d = D_MODEL
    off = 1 if final else 0
    nout = nblk - off
    y_spec = pl.BlockSpec((1, BLK, BRANCH), lambda bb, i: (bb, i + off, 0))
    lat_blk = (lambda i: i) if final else (lambda i: jnp.maximum(i - 1, 0))
    return pl.pallas_call(
        functools.partial(_outproj_body, final=final, n_batch=b),
        grid=(b, nout),
        in_specs=[y_spec, y_spec,
                  pl.BlockSpec((1, BLK, BRANCH), lambda bb, i: (bb, lat_blk(i), 0)),
                  pl.BlockSpec((1, BLK, BRANCH), lambda bb, i: (bb, 0, 0)),
                  y_spec,
                  pl.BlockSpec((1, BLK, d), lambda bb, i: (bb, lat_blk(i) + lat_off, 0)),
                  pl.BlockSpec((1, BLK, d), lambda bb, i: (bb, 0, 0)),
                  _layer_spec(mod.shape, layer, 2), _layer_spec(w.shape, layer, 2),
                  pl.BlockSpec((1, d), lambda bb, i: (0, 0))],
        out_specs=pl.BlockSpec((1, BLK, d), lambda bb, i: (bb, i, 0)),
        out_shape=jax.ShapeDtypeStruct((b, nout * BLK, d), F32),
        compiler_params=_params(2),
        name="outproj_final" if final else "outproj",
    )(ya, yb, yc_lat, yc_ctx, yd, lat, ctx, mod, w, fw)


def _pass_specs(nblk, reverse, n_batch, width):
    def blk_of(i):
        return _bwd_block(i, nblk) if reverse else i
    per_blk = BLK // HALO
    main = lambda i: (0, blk_of(i), 0)
    prev = pl.BlockSpec((n_batch, HALO, width), lambda i: (0, jnp.maximum(blk_of(i) * per_blk - 1, 0), 0))
    nxt = pl.BlockSpec((n_batch, HALO, width),
                       lambda i: (0, jnp.minimum((blk_of(i) + 1) * per_blk, nblk * per_blk - 1), 0))
    return main, prev, nxt


def _short_conv(xe_ref, x, prev, nxt, blk, nblk, cw, cb):
    prev_ok = blk >= 2
    next_ok = jnp.logical_and(blk >= 1, blk <= nblk - 2)
    xe_ref[0:HALO, :] = jnp.where(prev_ok, prev, 0.0)
    xe_ref[HALO:HALO + BLK, :] = x
    xe_ref[HALO + BLK:2 * HALO + BLK, :] = jnp.where(next_ok, nxt, 0.0)
    left = CONV_W // 2
    y = cb + xe_ref[HALO - left:HALO - left + BLK, :] * cw[0:1, :]
    for j in range(1, CONV_W):
        y = y + xe_ref[HALO - left + j:HALO - left + j + BLK, :] * cw[j:j + 1, :]
    return y


def _gla_blocks(ubs, w2, b2, sts, reverse):
    nb = len(ubs)
    r = _iota((BLK, BLK), 0)
    c = _iota((BLK, BLK), 1)
    same = (r >> 6) == (c >> 6)
    tri = jnp.logical_and(same, (c >= r) if reverse else (c <= r))
    sel = jnp.concatenate([tri, same], axis=0).astype(BF16)
    lane_k = _iota((GLA_CHUNK, 128), 1) >> 5
    rr = _iota((GLA_HEADS * GLA_CHUNK, GLA_CHUNK), 0) & (GLA_CHUNK - 1)
    cc = _iota((GLA_HEADS * GLA_CHUNK, GLA_CHUNK), 1)
    amask = (cc >= rr) if reverse else (cc <= rr)
    lane_v = _iota((GLA_CHUNK, BRANCH), 1) >> 6
    bd = (_iota((BRANCH, 128), 0) >> 6) == (_iota((BRANCH, 128), 1) >> 5)

    las = [jax.nn.log_sigmoid(_dot(ub[:, 768:896].astype(BF16), w2) + b2) * (1.0 / GLA_TAU) for ub in ubs]
    ggs = [_dot_exact_left(sel, la, terms=2) for la in las]
    gs = [gg[0:BLK] for gg in ggs]
    gls = [gg[BLK:2 * BLK] for gg in ggs]
    qgs = [ub[:, 0:128] * (GLA_DK ** -0.5) * jnp.exp(g) for ub, g in zip(ubs, gs)]
    kgs = [(ub[:, 128:256] * jnp.exp(-g)).astype(BF16) for ub, g in zip(ubs, gs)]
    kds = [(ub[:, 128:256] * jnp.exp(gl - g)).astype(BF16) for ub, g, gl in zip(ubs, gs, gls)]
    egls = [jnp.exp(gl) for gl in gls]
    vs = [ub[:, 256:512].astype(BF16) for ub in ubs]

    n_chunks = BLK // GLA_CHUNK
    order = list(reversed(range(n_chunks))) if reverse else list(range(n_chunks))
    rows = lambda a, ci: a[ci * GLA_CHUNK:(ci + 1) * GLA_CHUNK]
    cb = [(ci, bb) for ci in order for bb in range(nb)]

    def stacked_q(ci, bb):
        qg_c = rows(qgs[bb], ci)
        return jnp.concatenate([jnp.where(lane_k == h, qg_c, 0.0) for h in range(GLA_HEADS)], axis=0).astype(BF16)

    att = {k: jnp.where(amask, _dot_nt(stacked_q(*k), rows(kgs[k[1]], k[0])), 0.0).astype(BF16) for k in cb}
    res = {k: _dot(att[k], rows(vs[k[1]], k[0])) for k in cb}
    upd = {k: jnp.where(bd, _dot_tn(rows(vs[k[1]], k[0]), rows(kds[k[1]], k[0])), 0.0) for k in cb}
    sts = list(sts)
    outs = [[None] * n_chunks for _ in range(nb)]
    for ci, bb in cb:
        o_c = _dot_nt(rows(qgs[bb], ci).astype(BF16), sts[bb].astype(BF16))
        for h in range(GLA_HEADS):
            o_c = o_c + jnp.where(lane_v == h, res[(ci, bb)][h * GLA_CHUNK:(h + 1) * GLA_CHUNK], 0.0)
        sts[bb] = sts[bb] * egls[bb][ci * GLA_CHUNK:ci * GLA_CHUNK + 1, :] + upd[(ci, bb)]
        outs[bb][ci] = o_c
    return [jnp.concatenate(o, axis=0) for o in outs], sts


def _gla_body(*refs, reverse, n_batch):
    if reverse:
        u_ref, w2_ref, b2_ref, of_ref, nw_ref, o_ref, st_ref = refs
    else:
        u_ref, w2_ref, b2_ref, o_ref, st_ref = refs

    @pl.when(pl.program_id(0) == 0)
    def _():
        st_ref[...] = jnp.zeros_like(st_ref)

    ubs = [u_ref[bb] for bb in range(n_batch)]
    os_, sts = _gla_blocks(ubs, w2_ref[0, 0], b2_ref[0, 0], [st_ref[bb] for bb in range(n_batch)], reverse)
    if reverse:
        grp = ((_iota((BRANCH, BRANCH), 0) >> 6) == (_iota((BRANCH, BRANCH), 1) >> 6)).astype(BF16)
        os_ = [o + of_ref[bb] for bb, o in enumerate(os_)]
        mss = [_dot_exact_right(o * o, grp, terms=2) * (1.0 / GLA_DV) for o in os_]
        os_ = [(o * lax.rsqrt(ms + EPS) * nw_ref[0] * _silu(ub[:, 512:768])).astype(BF16)
               for o, ms, ub in zip(os_, mss, ubs)]
    for bb in range(n_batch):
        st_ref[bb] = sts[bb]
        o_ref[bb] = os_[bb]


def _gla_call(u, layer, w2, b2, reverse, o_fwd=None, nw=None):
    b, rows, _ = u.shape
    nblk = rows // BLK
    main, _, _ = _pass_specs(nblk, reverse, b, W_GLA)
    dr = 1 if reverse else 0
    in_specs = [pl.BlockSpec((b, BLK, W_GLA), main),
                pl.BlockSpec((1, 1, 128, 128), lambda i: (layer, dr, 0, 0)),
                pl.BlockSpec((1, 1, 1, 128), lambda i: (layer, dr, 0, 0))]
    args = [u, w2, b2]
    if reverse:
        in_specs += [pl.BlockSpec((b, BLK, BRANCH), main), _layer_spec(nw.shape, layer, 1)]
        args += [o_fwd, nw]
    return pl.pallas_call(
        functools.partial(_gla_body, reverse=reverse, n_batch=b),
        grid=(nblk,),
        in_specs=in_specs,
        out_specs=pl.BlockSpec((b, BLK, BRANCH), main),
        out_shape=jax.ShapeDtypeStruct((b, rows, BRANCH), BF16 if reverse else F32),
        scratch_shapes=[pltpu.VMEM((b, BRANCH, 128), F32)],
        compiler_params=_params(1),
        name="gla_bwd" if reverse else "gla_fwd",
    )(*args)


def _lru_body(*refs, reverse, nblk, n_batch):
    if reverse:
        (u_ref, prev_ref, next_ref, cw_ref, cb_ref, wg_ref, bg_ref, lam_ref, hf_ref,
         o_ref, xe_ref, a_ref, s_ref, p_ref, hb_ref, h_ref) = refs
    else:
        (u_ref, prev_ref, next_ref, cw_ref, cb_ref, wg_ref, bg_ref, lam_ref,
         o_ref, xe_ref, a_ref, s_ref, p_ref, h_ref) = refs
    i = pl.program_id(0)
    blk = _bwd_block(i, nblk) if reverse else i

    @pl.when(i == 0)
    def _():
        h_ref[...] = jnp.zeros_like(h_ref)

    for bb in range(n_batch):
        ub = u_ref[bb]
        xc = _short_conv(xe_ref.at[bb], ub[:, 0:BRANCH], prev_ref[bb][:, 0:BRANCH], next_ref[bb][:, 0:BRANCH],
                         blk, nblk, cw_ref[0], cb_ref[0])
        z = _dot(xc.astype(BF16), wg_ref[0, 0]) + bg_ref[0, 0]
        rg = jax.nn.sigmoid(z[:, 0:BRANCH])
        ig = jax.nn.sigmoid(z[:, BRANCH:2 * BRANCH])
        log_a = -LRU_C * rg * jax.nn.softplus(-lam_ref[0, 0])
        a = jnp.exp(log_a)
        a_ref[bb] = a
        s_ref[bb] = jnp.sqrt(jnp.tanh(-log_a) * (1.0 + a * a)) * (ig * xc)

    half = BLK // 2
    dst = hb_ref if reverse else o_ref
    if reverse:
        first, second, lo2 = range(BLK - 1, half - 1, -1), range(half - 1, -1, -1), 0
    else:
        first, second, lo2 = range(0, half), range(half, BLK), half
    hs = [h_ref[bb] for bb in range(n_batch)]
    ps = [None] * n_batch
    qs = [None] * n_batch
    for t1, t2 in zip(first, second):
        for bb in range(n_batch):
            hs[bb] = a_ref[bb, t1:t1 + 1, :] * hs[bb] + s_ref[bb, t1:t1 + 1, :]
            dst[bb, t1:t1 + 1, :] = hs[bb]
            a2 = a_ref[bb, t2:t2 + 1, :]
            s2 = s_ref[bb, t2:t2 + 1, :]
            ps[bb] = a2 if ps[bb] is None else a2 * ps[bb]
            qs[bb] = s2 if qs[bb] is None else a2 * qs[bb] + s2
            dst[bb, t2:t2 + 1, :] = qs[bb]
            p_ref[bb, t2 - lo2:t2 - lo2 + 1, :] = ps[bb]
    for bb in range(n_batch):
        dst[bb, lo2:lo2 + half, :] = dst[bb, lo2:lo2 + half, :] + p_ref[bb] * hs[bb]
        h_ref[bb] = qs[bb] + ps[bb] * hs[bb]
        if reverse:
            o_ref[bb] = ((hf_ref[bb] + hb_ref[bb]) * _silu(u_ref[bb][:, BRANCH:2 * BRANCH])).astype(BF16)


def _lru_call(u, layer, cw, cb, wg, bg, lam, reverse, h_fwd=None):
    b, rows, _ = u.shape
    nblk = rows // BLK
    main, prev, nxt = _pass_specs(nblk, reverse, b, W_LRU)
    dr = 1 if reverse else 0
    in_specs = [pl.BlockSpec((b, BLK, W_LRU), main), prev, nxt,
                _layer_spec(cw.shape, layer, 1), _layer_spec(cb.shape, layer, 1),
                pl.BlockSpec((1, 1, BRANCH, 2 * BRANCH), lambda i: (layer, dr, 0, 0)),
                pl.BlockSpec((1, 1, 1, 2 * BRANCH), lambda i: (layer, dr, 0, 0)),
                pl.BlockSpec((1, 1, 1, BRANCH), lambda i: (layer, dr, 0, 0))]
    args = [u, u, u, cw, cb, wg, bg, lam]
    scratch = [pltpu.VMEM((b, BLK + 2 * HALO, BRANCH), F32), pltpu.VMEM((b, BLK, BRANCH), F32),
               pltpu.VMEM((b, BLK, BRANCH), F32), pltpu.VMEM((b, BLK // 2, BRANCH), F32)]
    if reverse:
        in_specs.append(pl.BlockSpec((b, BLK, BRANCH), main))
        args.append(h_fwd)
        scratch.append(pltpu.VMEM((b, BLK, BRANCH), F32))
    scratch.append(pltpu.VMEM((b, 1, BRANCH), F32))
    return pl.pallas_call(
        functools.partial(_lru_body, reverse=reverse, nblk=nblk, n_batch=b),
        grid=(nblk,),
        in_specs=in_specs,
        out_specs=pl.BlockSpec((b, BLK, BRANCH), main),
        out_shape=jax.ShapeDtypeStruct((b, rows, BRANCH), BF16 if reverse else F32),
        scratch_shapes=scratch,
        compiler_params=_params(1),
        name="lru_bwd" if reverse else "lru_fwd",
    )(*args)


def _ssd_blocks(ubs, xcs, dtb, aneg, sts, reverse):
    nb = len(ubs)
    off = SSD_HEADS if reverse else 0
    ch = SSD_CHUNK
    hpg = SSD_HEADS // SSD_GROUPS
    r = _iota((ch, ch), 0)
    c = _iota((ch, ch), 1)
    keep = (c >= r) if reverse else (c <= r)
    tri = keep.astype(BF16)
    expand = (_iota((128, BRANCH), 0) == off + (_iota((128, BRANCH), 1) >> 6)).astype(BF16)
    lane_n = _iota((ch, 128), 1) >> 6
    lane_p = _iota((ch, BRANCH), 1) >> 6
    bd = (_iota((128, BRANCH), 0) >> 6) == (_iota((128, BRANCH), 1) >> 7)
    edge = 0 if reverse else ch - 1

    n_chunks = BLK // ch
    order = list(reversed(range(n_chunks))) if reverse else list(range(n_chunks))
    cb_keys = [(ci, bb) for ci in order for bb in range(nb)]
    rows = lambda a, ci: a[ci * ch:(ci + 1) * ch]
    xs = {k: rows(xcs[k[1]][:, 0:256], k[0]) for k in cb_keys}
    bm = {k: rows(xcs[k[1]][:, 256:384], k[0]).astype(BF16) for k in cb_keys}
    cm = {k: rows(xcs[k[1]][:, 384:512], k[0]) for k in cb_keys}
    dt = {k: jax.nn.softplus(rows(ubs[k[1]][:, 768:896], k[0]) + dtb) for k in cb_keys}

    cum = {k: _dot_exact_left(tri, dt[k] * aneg) for k in cb_keys}
    cum_x = {k: _dot_exact_right(cum[k], expand, terms=2) for k in cb_keys}
    dt_x = {k: _dot_exact_right(dt[k], expand, terms=2) for k in cb_keys}
    cum_t = {k: cum[k].T for k in cb_keys}
    dt_t = {k: dt[k].T for k in cb_keys}
    cbm = {(k, g): _dot_nt(jnp.where(lane_n == g, cm[k], 0.0).astype(BF16), bm[k])
           for k in cb_keys for g in range(SSD_GROUPS)}

    def decay_mix(k, h):
        seg = cum[k][:, off + h:off + h + 1] - cum_t[k][off + h:off + h + 1, :]
        lm = jnp.exp(jnp.where(keep, seg, -1e30))
        return (cbm[(k, h // hpg)] * lm * dt_t[k][off + h:off + h + 1, :]).astype(BF16)

    y_in = {}
    for k in cb_keys:
        m_cat = jnp.concatenate([decay_mix(k, h) for h in range(SSD_HEADS)], axis=1)
        x_bd = jnp.concatenate([jnp.where(lane_p == h, xs[k], 0.0) for h in range(SSD_HEADS)], axis=0)
        y_in[k] = _dot(m_cat, x_bd.astype(BF16))
    tot_x = {k: cum_x[k][edge:edge + 1, :] for k in cb_keys}
    upd = {k: jnp.where(bd, _dot_tn(bm[k], (xs[k] * (jnp.exp(tot_x[k] - cum_x[k]) * dt_x[k])).astype(BF16)), 0.0)
           for k in cb_keys}
    sts = list(sts)
    outs = [[None] * n_chunks for _ in range(nb)]
    for k in cb_keys:
        ci, bb = k
        outs[bb][ci] = y_in[k] + _dot(cm[k].astype(BF16), sts[bb].astype(BF16)) * jnp.exp(cum_x[k])
        sts[bb] = sts[bb] * jnp.exp(tot_x[k]) + upd[k]
    return [jnp.concatenate(o, axis=0) for o in outs], sts


def _ssd_body(*refs, reverse, nblk, n_batch):
    if reverse:
        (u_ref, prev_ref, next_ref, cw_ref, cb_ref, dtb_ref, an_ref, yf_ref, dsk_ref, nw_ref,
         o_ref, xe_ref, st_ref) = refs
    else:
        u_ref, prev_ref, next_ref, cw_ref, cb_ref, dtb_ref, an_ref, o_ref, xe_ref, st_ref = refs
    i = pl.program_id(0)
    blk = _bwd_block(i, nblk) if reverse else i

    @pl.when(i == 0)
    def _():
        st_ref[...] = jnp.zeros_like(st_ref)

    ubs = [u_ref[bb] for bb in range(n_batch)]
    xcs = [_silu(_short_conv(xe_ref.at[bb], ubs[bb][:, 0:512], prev_ref[bb][:, 0:512], next_ref[bb][:, 0:512],
                             blk, nblk, cw_ref[0], cb_ref[0])) for bb in range(n_batch)]
    ys, sts = _ssd_blocks(ubs, xcs, dtb_ref[0], an_ref[0], [st_ref[bb] for bb in range(n_batch)], reverse)
    for bb in range(n_batch):
        st_ref[bb] = sts[bb]
        if reverse:
            y = ys[bb] + yf_ref[bb] + dsk_ref[0] * xcs[bb][:, 0:256]
            yz = y * _silu(ubs[bb][:, 512:768])
            ms = jnp.mean(yz * yz, axis=-1, keepdims=True)
            o_ref[bb] = (yz * lax.rsqrt(ms + EPS) * nw_ref[0]).astype(BF16)
        else:
            o_ref[bb] = ys[bb]


def _ssd_call(u, layer, cw, cb, dtb, aneg, reverse, y_fwd=None, dskip=None, nw=None):
    b, rows, _ = u.shape
    nblk = rows // BLK
    main, prev, nxt = _pass_specs(nblk, reverse, b, W_SSD)
    in_specs = [pl.BlockSpec((b, BLK, W_SSD), main), prev, nxt,
                _layer_spec(cw.shape, layer, 1), _layer_spec(cb.shape, layer, 1),
                _layer_spec(dtb.shape, layer, 1), _layer_spec(aneg.shape, layer, 1)]
    args = [u, u, u, cw, cb, dtb, aneg]
    if reverse:
        in_specs += [pl.BlockSpec((b, BLK, BRANCH), main),
                     _layer_spec(dskip.shape, layer, 1), _layer_spec(nw.shape, layer, 1)]
        args += [y_fwd, dskip, nw]
    return pl.pallas_call(
        functools.partial(_ssd_body, reverse=reverse, nblk=nblk, n_batch=b),
        grid=(nblk,),
        in_specs=in_specs,
        out_specs=pl.BlockSpec((b, BLK, BRANCH), main),
        out_shape=jax.ShapeDtypeStruct((b, rows, BRANCH), BF16 if reverse else F32),
        scratch_shapes=[pltpu.VMEM((b, BLK + 2 * HALO, 512), F32), pltpu.VMEM((b, 128, BRANCH), F32)],
        compiler_params=_params(1),
        name="ssd_bwd" if reverse else "ssd_fwd",
    )(*args)


def _key_step(nk):
    return max(t for t in range(BLK, min(ATT_TK, nk) + 1, BLK) if nk % t == 0)


def _attn_body(*refs, nk, lam_init, n_qb):
    qt_refs, (k_ref, vt_ref), g_refs = refs[0:n_qb], refs[n_qb:n_qb + 2], refs[n_qb + 2:2 * n_qb + 2]
    lam_ref, sw_ref, o_ref, s0_ref, s1_ref = refs[2 * n_qb + 2:]
    tk = _key_step(nk)
    n_steps = nk // tk
    sub = tk // BLK
    fold = 32
    lv = lam_ref[0]
    lam = (jnp.exp(jnp.sum(lv[0:1] * lv[1:2], axis=-1, keepdims=True))
           - jnp.exp(jnp.sum(lv[2:3] * lv[3:4], axis=-1, keepdims=True)) + lam_init)
    row = _iota((128, BLK), 0)
    pairs = [(qb, h, half) for qb in range(n_qb) for h in range(DIFF_HEADS) for half in range(2)]
    s_bufs = (s0_ref, s1_ref)

    def masked_qt(p):
        qb, h, half = pairs[p]
        qt = qt_refs[qb][0, 0, (h // 2) * 128:(h // 2 + 1) * 128, :]
        lo = (h % 2) * 64 + half * DIFF_D
        return jnp.where(jnp.logical_and(row >= lo, row < lo + DIFF_D), qt, jnp.zeros_like(qt))

    def key_start(j):
        return j * tk if isinstance(j, int) else pl.multiple_of(j * tk, BLK)

    def stage_a(p, qm, j, jj, mx):
        tile = pairs[p][1] // 2
        rows = pl.ds(key_start(j) + jj * BLK, BLK)
        s = _dot(k_ref[0, rows, tile * 128:(tile + 1) * 128], qm)
        s_bufs[p % 2][rows, :] = s
        return jnp.maximum(mx, jnp.max(s.reshape(BLK // fold, fold, BLK), axis=0))

    def stage_bc(p, m, j, jj, ls, acc):
        h = pairs[p][1]
        e = jnp.exp2(s_bufs[p % 2][pl.ds(key_start(j) + jj * BLK, BLK), :] - m)
        ls = ls + jnp.sum(e.reshape(BLK // fold, fold, BLK), axis=0)
        acc = acc + _dot(vt_ref[0, j * sub + jj, h * DIFF_DV:(h + 1) * DIFF_DV, :], e.astype(BF16))
        return ls, acc

    results = []
    m_prev = None
    for p in range(len(pairs) + 1):
        do_a = p < len(pairs)
        do_bc = p >= 1
        qm = masked_qt(p) if do_a else None

        def step(j, carry, p=p, qm=qm, m_prev=m_prev, do_a=do_a, do_bc=do_bc):
            mx, ls, acc = carry
            for jj in range(sub):
                if do_bc:
                    ls, acc = stage_bc(p - 1, m_prev, j, jj, ls, acc)
                if do_a:
                    mx = stage_a(p, qm, j, jj, mx)
            return mx, ls, acc

        carry = (jnp.full((fold, BLK), -jnp.inf, F32), jnp.zeros((fold, BLK), F32),
                 jnp.zeros((DIFF_DV, BLK), F32))
        if n_steps > 1:
            mx, ls, acc = lax.fori_loop(0, n_steps, step, carry)
        else:
            mx, ls, acc = step(0, carry)
        if do_bc:
            results.append(acc / jnp.sum(ls, axis=0, keepdims=True))
        if do_a:
            m_prev = jnp.max(mx, axis=0, keepdims=True)

    grp = ((_iota((BRANCH, BRANCH), 0) >> 6) == (_iota((BRANCH, BRANCH), 1) >> 6)).astype(BF16)
    per_qb = 2 * DIFF_HEADS
    for qb in range(n_qb):
        res = results[qb * per_qb:(qb + 1) * per_qb]
        heads = [res[2 * h] - lam * res[2 * h + 1] for h in range(DIFF_HEADS)]
        o = jnp.concatenate(heads, axis=0).T
        ms = _dot_exact_right(o * o, grp, terms=2) * (1.0 / DIFF_DV)
        y = o * lax.rsqrt(ms + EPS) * sw_ref[0] * (1.0 - lam_init)
        o_ref[0, qb * BLK:(qb + 1) * BLK, :] = (y * _silu(g_refs[qb][0])).astype(BF16)


def _attn_call(qt, k, vt, gate, layer, lam_vecs, sw, lam_init, ctx_only):
    b, rows, _ = k.shape
    nblk = rows // BLK
    nk = BLK if ctx_only else rows
    nq = 1 if ctx_only else nblk - 1
    qoff = 0 if ctx_only else 1
    n_qb = max(t for t in (1, 2, 4) if nq % t == 0)
    q_specs = [pl.BlockSpec((1, 1, BRANCH, BLK), lambda bb, i, t=t: (bb, n_qb * i + t + qoff, 0, 0))
               for t in range(n_qb)]
    g_specs = [pl.BlockSpec((1, BLK, BRANCH), lambda bb, i, t=t: (bb, n_qb * i + t + qoff, 0))
               for t in range(n_qb)]
    return pl.pallas_call(
        functools.partial(_attn_body, nk=nk, lam_init=lam_init, n_qb=n_qb),
        grid=(b, nq // n_qb),
        in_specs=q_specs + [pl.BlockSpec((1, nk, BRANCH), lambda bb, i: (bb, 0, 0)),
                            pl.BlockSpec((1, nk // BLK, BRANCH, BLK), lambda bb, i: (bb, 0, 0, 0))]
        + g_specs + [_layer_spec(lam_vecs.shape, layer, 2), _layer_spec(sw.shape, layer, 2)],
        out_specs=pl.BlockSpec((1, n_qb * BLK, BRANCH), lambda bb, i: (bb, i, 0)),
        out_shape=jax.ShapeDtypeStruct((b, nq * BLK, BRANCH), BF16),
        scratch_shapes=[pltpu.VMEM((nk, BLK), F32), pltpu.VMEM((nk, BLK), F32)],
        compiler_params=_params(2),
        name="diff_attn_ctx" if ctx_only else "diff_attn",
    )(*([qt] * n_qb), k, vt, *([gate] * n_qb), lam_vecs, sw)


def _layout_w_in(w):
    def group(lo, pieces, width):
        parts = [w[..., lo + a:lo + b] for a, b in pieces]
        used = sum(b - a for a, b in pieces)
        if used < width:
            parts.append(jnp.zeros(w.shape[:-1] + (width - used,), w.dtype))
        return parts
    parts = (group(COL_GLA, [(0, 512), (544, 800), (512, 544)], W_GLA)
             + group(COL_LRU, [(0, 512)], W_LRU)
             + group(COL_DIFF, [(0, 1024)], W_DIFF)
             + group(COL_SSD, [(0, 512), (520, 776), (512, 520)], W_SSD))
    return jnp.concatenate(parts, axis=-1)


def _block_diag(w):
    nb, n = w.shape[-3], w.shape[-1]
    eye = jnp.eye(nb, dtype=w.dtype)
    dense = eye[:, None, :, None] * w[..., :, :, None, :]
    return dense.reshape(w.shape[:-3] + (nb * n, nb * n))


def _rope_tables(t_lat):
    n_freq = DIFF_D // 4
    inv = ROPE_BASE ** (-jnp.arange(n_freq, dtype=F32) / n_freq)
    tpos = jnp.arange(t_lat)
    ang_r = (tpos // GRID_W).astype(F32)[:, None] * inv
    ang_c = (tpos % GRID_W).astype(F32)[:, None] * inv
    ang = jnp.concatenate([ang_r, ang_r, ang_c, ang_c], axis=-1)
    reps = BRANCH // DIFF_D
    sign = jnp.where((jnp.arange(DIFF_D) % 16) < 8, -1.0, 1.0).astype(F32)
    return jnp.tile(jnp.cos(ang), (1, reps)), jnp.tile(jnp.sin(ang) * sign, (1, reps))


def kernel(x, c, ctx, c_ctx, w_mod, b_mod, norm_w, w_in, w_out, gla_w2, gla_b2, gla_norm_w, lru_conv_w, lru_conv_b, lru_wa, lru_ba, lru_wx, lru_bx, lru_lam, diff_lam, diff_subln_w, ssd_conv_w, ssd_conv_b, ssd_dt_bias, ssd_a_log, ssd_d, ssd_norm_w, final_norm_w):
    bsz, t_lat, d = x.shape
    depth = w_mod.shape[0]
    assert ctx.shape[1] == BLK and t_lat % BLK == 0 and d == D_MODEL and bsz + 1 <= 8
    nblk = 1 + t_lat // BLK

    cvec = jnp.concatenate([c, c_ctx[None], jnp.zeros((8 - bsz - 1, d), F32)], axis=0)
    mod = _mod_call(cvec, w_mod, b_mod)
    cos, sin = _rope_tables(t_lat)

    nw3 = norm_w[:, None, :]
    w_in_p = _layout_w_in(w_in.astype(BF16))
    w_out_b = w_out.astype(BF16)
    pad_r = lambda w, lo: jnp.pad(w, ((0, 0), (lo, 128 - GLA_RANK - lo), (0, 0)))
    w2 = jnp.stack([pad_r(gla_w2[:, 0], 0), pad_r(gla_w2[:, 1], GLA_RANK)], axis=1).astype(BF16)
    b2 = gla_b2[:, :, None, :]
    gnw = jnp.tile(gla_norm_w, (1, GLA_HEADS))[:, None, :]
    wg = jnp.concatenate([_block_diag(lru_wa), _block_diag(lru_wx)], axis=-1).astype(BF16)
    bg = jnp.concatenate([lru_ba, lru_bx], axis=-1)[:, :, None, :]
    llam = lru_lam[:, :, None, :]
    lcb = lru_conv_b[:, None, :]
    sw = jnp.tile(diff_subln_w, (1, DIFF_HEADS))[:, None, :]
    pad_l = lambda v: jnp.pad(v.reshape(depth, -1), ((0, 0), (0, 128 - 2 * SSD_HEADS)))[:, None, :]
    dtb = pad_l(ssd_dt_bias)
    aneg = pad_l(-jnp.exp(ssd_a_log))
    dsk = jnp.repeat(ssd_d, SSD_P, axis=-1)[:, None, :]
    scb = ssd_conv_b[:, None, :]
    snw = ssd_norm_w[:, None, :]

    lat, lat_off, cx = x, 0, ctx
    for l in range(depth):
        last = l == depth - 1
        lam_init = 0.8 - 0.6 * math.exp(-0.3 * l)
        ua, ub, ud, qt, kp, vt, gd = _inproj_call(lat, lat_off, cx, l, nblk, nw3, mod, w_in_p, cos, sin)

        of = _gla_call(ua, l, w2, b2, False)
        ya = _gla_call(ua, l, w2, b2, True, of, gnw)

        hf = _lru_call(ub, l, lru_conv_w, lcb, wg, bg, llam, False)
        yb = _lru_call(ub, l, lru_conv_w, lcb, wg, bg, llam, True, hf)

        yc_lat = _attn_call(qt, kp, vt, gd, l, diff_lam, sw, lam_init, False)
        yc_ctx = yc_lat if last else _attn_call(qt, kp, vt, gd, l, diff_lam, sw, lam_init, True)

        yf = _ssd_call(ud, l, ssd_conv_w, scb, dtb, aneg, False)
        yd = _ssd_call(ud, l, ssd_conv_w, scb, dtb, aneg, True, yf, dsk, snw)

        h = _outproj_call(ya, yb, yc_lat, yc_ctx, yd, lat, lat_off, cx, l, nblk, mod, w_out_b,
                          final_norm_w[None], last)
        lat, lat_off, cx = h, 1, h
    return h
```

```python
import functools
import math

import jax
import jax.numpy as jnp
from jax import lax
from jax.experimental import pallas as pl
from jax.experimental.pallas import tpu as pltpu

F32 = jnp.float32
BF16 = jnp.bfloat16

D_MODEL = 1024
BRANCH = D_MODEL // 4
GRID_W = 64
EPS = 1e-6
CONV_W = 4
BLK = 256
HALO = 8
GLA_HEADS, GLA_DK, GLA_DV = 4, 32, 64
GLA_RANK = 16
GLA_TAU = 16.0
GLA_CHUNK = 64
LRU_C = 8.0
DIFF_HEADS, DIFF_D, DIFF_DV = 4, 32, 64
ROPE_BASE = 10000.0
SSD_HEADS, SSD_P, SSD_N, SSD_GROUPS = 4, 64, 64, 2
SSD_CHUNK = 128
ATT_TK = 2816
W_GLA, W_LRU, W_DIFF, W_SSD = 896, 512, 1024, 896
COL_GLA, COL_LRU, COL_DIFF, COL_SSD = 0, 800, 1312, 2336
W_ALL = W_GLA + W_LRU + W_DIFF + W_SSD
VMEM_LIMIT = 56 * 1024 * 1024


def _dot(a, b):
    return jnp.dot(a, b, preferred_element_type=F32)


def _dot_nt(a, b):
    return lax.dot_general(a, b, (((1,), (1,)), ((), ())), preferred_element_type=F32)


def _dot_tn(a, b):
    return lax.dot_general(a, b, (((0,), (0,)), ((), ())), preferred_element_type=F32)


def _split(x, terms):
    out = []
    r = x
    for _ in range(terms):
        p = r.astype(BF16)
        out.append(p)
        r = r - p.astype(F32)
    return out


def _dot_exact_left(m, x, terms=3):
    return sum(_dot(m, p) for p in _split(x, terms))


def _dot_exact_right(x, m, terms=3):
    return sum(_dot(p, m) for p in _split(x, terms))


def _iota(shape, dim):
    return lax.broadcasted_iota(jnp.int32, shape, dim)


def _silu(x):
    return x * jax.nn.sigmoid(x)


def _params(n_axes):
    return pltpu.CompilerParams(dimension_semantics=("arbitrary",) * n_axes, vmem_limit_bytes=VMEM_LIMIT)


def _layer_spec(shape, layer, n_axes):
    block = (1,) + tuple(shape[1:])
    idx = (layer,) + (0,) * (len(shape) - 1)
    if n_axes == 1:
        return pl.BlockSpec(block, lambda i: idx)
    return pl.BlockSpec(block, lambda bb, i: idx)


def _bwd_block(i, nblk):
    return jnp.where(i == 0, 0, nblk - i)


def _mod_body(c_ref, w_ref, b_ref, o_ref):
    s = _silu(c_ref[...])
    o_ref[0] = _dot(s.astype(BF16), w_ref[0].astype(BF16)) + b_ref[0]


def _mod_call(cvec, w_mod, b_mod):
    depth, d, n = w_mod.shape
    tn = 1024
    return pl.pallas_call(
        _mod_body,
        grid=(depth, n // tn),
        in_specs=[pl.BlockSpec((8, d), lambda l, j: (0, 0)),
                  pl.BlockSpec((1, d, tn), lambda l, j: (l, 0, j)),
                  pl.BlockSpec((1, 1, tn), lambda l, j: (l, 0, j))],
        out_specs=pl.BlockSpec((1, 8, tn), lambda l, j: (l, 0, j)),
        out_shape=jax.ShapeDtypeStruct((depth, 8, n), F32),
        compiler_params=_params(2),
        name="adaln_mod",
    )(cvec, w_mod, b_mod.reshape(depth, 1, n))


def _mod_row(mod_ref, is_ctx, bb, n_batch, part):
    r = jnp.where(is_ctx, n_batch, bb)
    return mod_ref[0, pl.ds(r, 1), part * D_MODEL:(part + 1) * D_MODEL]


def _project_in(x, is_ctx, bb, n_batch, nw_ref, mod_ref, w_ref, cos_ref, sin_ref, outs):
    oa_ref, ob_ref, od_ref, qt_ref, k_ref, vt_ref, g_ref = outs
    ms = jnp.mean(x * x, axis=-1, keepdims=True)
    xn = x * lax.rsqrt(ms + EPS) * nw_ref[0]
    xm = xn * (1.0 + _mod_row(mod_ref, is_ctx, bb, n_batch, 1)) + _mod_row(mod_ref, is_ctx, bb, n_batch, 0)
    u = _dot(xm.astype(BF16), w_ref[0])
    oa_ref[0] = u[:, 0:W_GLA]
    ob_ref[0] = u[:, W_GLA:W_GLA + W_LRU]
    od_ref[0] = u[:, W_GLA + W_LRU + W_DIFF:W_ALL]

    ud = u[:, W_GLA + W_LRU:W_GLA + W_LRU + W_DIFF]
    lane_tiles = BRANCH // 128
    cs = jnp.concatenate([jnp.where(is_ctx, 1.0, cos_ref[...])] * lane_tiles, axis=1)
    sn = jnp.concatenate([jnp.where(is_ctx, 0.0, sin_ref[...])] * lane_tiles, axis=1)
    first = (_iota((BLK, BRANCH), 1) & 15) < 8

    def rope(t):
        rot = jnp.where(first, pltpu.roll(t, BRANCH - 8, 1), pltpu.roll(t, 8, 1))
        return t * cs + rot * sn

    qt_ref[0, 0] = (rope(ud[:, 0:256]) * (DIFF_D ** -0.5 * math.log2(math.e))).T.astype(BF16)
    k_ref[0] = rope(ud[:, 256:512]).astype(BF16)
    vt_ref[0, 0] = ud[:, 512:768].T.astype(BF16)
    g_ref[0] = ud[:, 768:1024]


def _inproj_body(lat_ref, ctx_ref, nw_ref, mod_ref, w_ref, cos_ref, sin_ref, *outs, n_batch):
    is_ctx = pl.program_id(1) == 0
    x = jnp.where(is_ctx, ctx_ref[0], lat_ref[0])
    _project_in(x, is_ctx, pl.program_id(0), n_batch, nw_ref, mod_ref, w_ref, cos_ref, sin_ref, outs)


def _inproj_specs(b, nblk, layer, nw, mod, w):
    rows = nblk * BLK
    row_out = lambda n: pl.BlockSpec((1, BLK, n), lambda bb, i: (bb, i, 0))
    t_out = pl.BlockSpec((1, 1, BRANCH, BLK), lambda bb, i: (bb, i, 0, 0))
    tab = pl.BlockSpec((BLK, 128), lambda bb, i: (jnp.maximum(i - 1, 0), 0))
    in_specs = [_layer_spec(nw.shape, layer, 2), _layer_spec(mod.shape, layer, 2), _layer_spec(w.shape, layer, 2),
                tab, tab]
    out_specs = [row_out(W_GLA), row_out(W_LRU), row_out(W_SSD), t_out, row_out(BRANCH), t_out, row_out(BRANCH)]
    out_shape = [jax.ShapeDtypeStruct((b, rows, W_GLA), F32),
                 jax.ShapeDtypeStruct((b, rows, W_LRU), F32),
                 jax.ShapeDtypeStruct((b, rows, W_SSD), F32),
                 jax.ShapeDtypeStruct((b, nblk, BRANCH, BLK), BF16),
                 jax.ShapeDtypeStruct((b, rows, BRANCH), BF16),
                 jax.ShapeDtypeStruct((b, nblk, BRANCH, BLK), BF16),
                 jax.ShapeDtypeStruct((b, rows, BRANCH), F32)]
    return in_specs, out_specs, out_shape


def _inproj_call(lat, lat_off, ctx, layer, nblk, nw, mod, w, cos, sin):
    b = lat.shape[0]
    d = D_MODEL
    p_specs, out_specs, out_shape = _inproj_specs(b, nblk, layer, nw, mod, w)
    return pl.pallas_call(
        functools.partial(_inproj_body, n_batch=b),
        grid=(b, nblk),
        in_specs=[pl.BlockSpec((1, BLK, d), lambda bb, i: (bb, jnp.maximum(i - 1, 0) + lat_off, 0)),
                  pl.BlockSpec((1, BLK, d), lambda bb, i: (bb, 0, 0))] + p_specs,
        out_specs=out_specs,
        out_shape=out_shape,
        compiler_params=_params(2),
        name="norm_inproj",
    )(lat, ctx, nw, mod, w, cos, sin)


def _residual_update(ya_ref, yb_ref, ycl_ref, ycc_ref, yd_ref, lat_ref, ctx_ref, mod_ref, w_ref, is_ctx, bb, n_batch):
    if is_ctx is False:
        yc, res = ycl_ref[0], lat_ref[0]
    else:
        yc = jnp.where(is_ctx, ycc_ref[0], ycl_ref[0])
        res = jnp.where(is_ctx, ctx_ref[0], lat_ref[0])
    y = jnp.concatenate([ya_ref[0], yb_ref[0], yc, yd_ref[0]], axis=-1)
    return res + _mod_row(mod_ref, is_ctx, bb, n_batch, 2) * _dot(y, w_ref[0])


def _outproj_final_body(ya_ref, yb_ref, ycl_ref, ycc_ref, yd_ref, lat_ref, ctx_ref, mod_ref, w_ref, fw_ref, o_ref,
                        *, n_batch):
    hn = _residual_update(ya_ref, yb_ref, ycl_ref, ycc_ref, yd_ref, lat_ref, ctx_ref, mod_ref, w_ref,
                          False, pl.program_id(0), n_batch)
    ms = jnp.mean(hn * hn, axis=-1, keepdims=True)
    o_ref[0] = hn * lax.rsqrt(ms + EPS) * fw_ref[...]


def _outproj_next_body(ya_ref, yb_ref, ycl_ref, ycc_ref, yd_ref, lat_ref, ctx_ref, modg_ref, wo_ref,
                       nw_ref, mod_ref, wi_ref, cos_ref, sin_ref, h_ref, *outs, n_batch):
    bb = pl.program_id(0)
    is_ctx = pl.program_id(1) == 0
    hn = _residual_update(ya_ref, yb_ref, ycl_ref, ycc_ref, yd_ref, lat_ref, ctx_ref, modg_ref, wo_ref,
                          is_ctx, bb, n_batch)
    h_ref[0] = hn
    _project_in(hn, is_ctx, bb, n_batch, nw_ref, mod_ref, wi_ref, cos_ref, sin_ref, outs)


def _outproj_in_specs(lat_off, layer, mod, w, final):
    d = D_MODEL
    off = 1 if final else 0
    y_spec = pl.BlockSpec((1, BLK, BRANCH), lambda bb, i: (bb, i + off, 0))
    lat_blk = (lambda i: i) if final else (lambda i: jnp.maximum(i - 1, 0))
    return [y_spec, y_spec,
            pl.BlockSpec((1, BLK, BRANCH), lambda bb, i: (bb, lat_blk(i), 0)),
            pl.BlockSpec((1, BLK, BRANCH), lambda bb, i: (bb, 0, 0)),
            y_spec,
            pl.BlockSpec((1, BLK, d), lambda bb, i: (bb, lat_blk(i) + lat_off, 0)),
            pl.BlockSpec((1, BLK, d), lambda bb, i: (bb, 0, 0)),
            _layer_spec(mod.shape, layer, 2), _layer_spec(w.shape, layer, 2)]


def _outproj_final_call(ya, yb, yc_lat, yd, lat, lat_off, ctx, layer, nblk, mod, w, fw):
    b = lat.shape[0]
    d = D_MODEL
    nout = nblk - 1
    return pl.pallas_call(
        functools.partial(_outproj_final_body, n_batch=b),
        grid=(b, nout),
        in_specs=_outproj_in_specs(lat_off, layer, mod, w, True) + [pl.BlockSpec((1, d), lambda bb, i: (0, 0))],
        out_specs=pl.BlockSpec((1, BLK, d), lambda bb, i: (bb, i, 0)),
        out_shape=jax.ShapeDtypeStruct((b, nout * BLK, d), F32),
        compiler_params=_params(2),
        name="outproj_final",
    )(ya, yb, yc_lat, yc_lat, yd, lat, ctx, mod, w, fw)


def _outproj_next_call(ya, yb, yc_lat, yc_ctx, yd, lat, lat_off, ctx, layer, nblk, mod, w_out, nw, w_in, cos, sin):
    b = lat.shape[0]
    d = D_MODEL
    p_specs, p_out_specs, p_out_shape = _inproj_specs(b, nblk, layer + 1, nw, mod, w_in)
    return pl.pallas_call(
        functools.partial(_outproj_next_body, n_batch=b),
        grid=(b, nblk),
        in_specs=_outproj_in_specs(lat_off, layer, mod, w_out, False) + p_specs,
        out_specs=[pl.BlockSpec((1, BLK, d), lambda bb, i: (bb, i, 0))] + p_out_specs,
        out_shape=[jax.ShapeDtypeStruct((b, nblk * BLK, d), F32)] + p_out_shape,
        compiler_params=_params(2),
        name="outproj_inproj",
    )(ya, yb, yc_lat, yc_ctx, yd, lat, ctx, mod, w_out, nw, mod, w_in, cos, sin)


def _pass_specs(nblk, reverse, n_batch, width):
    def blk_of(i):
        return _bwd_block(i, nblk) if reverse else i
    per_blk = BLK // HALO
    main = lambda i: (0, blk_of(i), 0)
    prev = pl.BlockSpec((n_batch, HALO, width), lambda i: (0, jnp.maximum(blk_of(i) * per_blk - 1, 0), 0))
    nxt = pl.BlockSpec((n_batch, HALO, width),
                       lambda i: (0, jnp.minimum((blk_of(i) + 1) * per_blk, nblk * per_blk - 1), 0))
    return main, prev, nxt


def _short_conv(xe_ref, x, prev, nxt, blk, nblk, cw, cb):
    prev_ok = blk >= 2
    next_ok = jnp.logical_and(blk >= 1, blk <= nblk - 2)
    xe_ref[0:HALO, :] = jnp.where(prev_ok, prev, 0.0)
    xe_ref[HALO:HALO + BLK, :] = x
    xe_ref[HALO + BLK:2 * HALO + BLK, :] = jnp.where(next_ok, nxt, 0.0)
    left = CONV_W // 2
    y = cb + xe_ref[HALO - left:HALO - left + BLK, :] * cw[0:1, :]
    for j in range(1, CONV_W):
        y = y + xe_ref[HALO - left + j:HALO - left + j + BLK, :] * cw[j:j + 1, :]
    return y


def _gla_blocks(ubs, w2, b2, sts, reverse):
    nb = len(ubs)
    r = _iota((BLK, BLK), 0)
    c = _iota((BLK, BLK), 1)
    same = (r >> 6) == (c >> 6)
    tri = jnp.logical_and(same, (c >= r) if reverse else (c <= r))
    sel = jnp.concatenate([tri, same], axis=0).astype(BF16)
    lane_k = _iota((GLA_CHUNK, 128), 1) >> 5
    rr = _iota((GLA_HEADS * GLA_CHUNK, GLA_CHUNK), 0) & (GLA_CHUNK - 1)
    cc = _iota((GLA_HEADS * GLA_CHUNK, GLA_CHUNK), 1)
    amask = (cc >= rr) if reverse else (cc <= rr)
    lane_v = _iota((GLA_CHUNK, BRANCH), 1) >> 6
    bd = (_iota((BRANCH, 128), 0) >> 6) == (_iota((BRANCH, 128), 1) >> 5)

    las = [jax.nn.log_sigmoid(_dot(ub[:, 768:896].astype(BF16), w2) + b2) * (1.0 / GLA_TAU) for ub in ubs]
    ggs = [_dot_exact_left(sel, la, terms=2) for la in las]
    gs = [gg[0:BLK] for gg in ggs]
    gls = [gg[BLK:2 * BLK] for gg in ggs]
    qgs = [ub[:, 0:128] * (GLA_DK ** -0.5) * jnp.exp(g) for ub, g in zip(ubs, gs)]
    kgs = [(ub[:, 128:256] * jnp.exp(-g)).astype(BF16) for ub, g in zip(ubs, gs)]
    kds = [(ub[:, 128:256] * jnp.exp(gl - g)).astype(BF16) for ub, g, gl in zip(ubs, gs, gls)]
    egls = [jnp.exp(gl) for gl in gls]
    vs = [ub[:, 256:512].astype(BF16) for ub in ubs]

    n_chunks = BLK // GLA_CHUNK
    order = list(reversed(range(n_chunks))) if reverse else list(range(n_chunks))
    rows = lambda a, ci: a[ci * GLA_CHUNK:(ci + 1) * GLA_CHUNK]
    cb = [(ci, bb) for ci in order for bb in range(nb)]

    def stacked_q(ci, bb):
        qg_c = rows(qgs[bb], ci)
        return jnp.concatenate([jnp.where(lane_k == h, qg_c, 0.0) for h in range(GLA_HEADS)], axis=0).astype(BF16)

    att = {k: jnp.where(amask, _dot_nt(stacked_q(*k), rows(kgs[k[1]], k[0])), 0.0).astype(BF16) for k in cb}
    res = {k: _dot(att[k], rows(vs[k[1]], k[0])) for k in cb}
    upd = {k: jnp.where(bd, _dot_tn(rows(vs[k[1]], k[0]), rows(kds[k[1]], k[0])), 0.0) for k in cb}
    sts = list(sts)
    outs = [[None] * n_chunks for _ in range(nb)]
    for ci, bb in cb:
        o_c = _dot_nt(rows(qgs[bb], ci).astype(BF16), sts[bb].astype(BF16))
        for h in range(GLA_HEADS):
            o_c = o_c + jnp.where(lane_v == h, res[(ci, bb)][h * GLA_CHUNK:(h + 1) * GLA_CHUNK], 0.0)
        sts[bb] = sts[bb] * egls[bb][ci * GLA_CHUNK:ci * GLA_CHUNK + 1, :] + upd[(ci, bb)]
        outs[bb][ci] = o_c
    return [jnp.concatenate(o, axis=0) for o in outs], sts


def _gla_body(*refs, reverse, n_batch):
    if reverse:
        u_ref, w2_ref, b2_ref, of_ref, nw_ref, o_ref, st_ref = refs
    else:
        u_ref, w2_ref, b2_ref, o_ref, st_ref = refs

    @pl.when(pl.program_id(0) == 0)
    def _():
        st_ref[...] = jnp.zeros_like(st_ref)

    ubs = [u_ref[bb] for bb in range(n_batch)]
    os_, sts = _gla_blocks(ubs, w2_ref[0, 0], b2_ref[0, 0], [st_ref[bb] for bb in range(n_batch)], reverse)
    if reverse:
        grp = ((_iota((BRANCH, BRANCH), 0) >> 6) == (_iota((BRANCH, BRANCH), 1) >> 6)).astype(BF16)
        os_ = [o + of_ref[bb] for bb, o in enumerate(os_)]
        mss = [_dot_exact_right(o * o, grp, terms=2) * (1.0 / GLA_DV) for o in os_]
        os_ = [(o * lax.rsqrt(ms + EPS) * nw_ref[0] * _silu(ub[:, 512:768])).astype(BF16)
               for o, ms, ub in zip(os_, mss, ubs)]
    for bb in range(n_batch):
        st_ref[bb] = sts[bb]
        o_ref[bb] = os_[bb]


def _gla_call(u, layer, w2, b2, reverse, o_fwd=None, nw=None):
    b, rows, _ = u.shape
    nblk = rows // BLK
    main, _, _ = _pass_specs(nblk, reverse, b, W_GLA)
    dr = 1 if reverse else 0
    in_specs = [pl.BlockSpec((b, BLK, W_GLA), main),
                pl.BlockSpec((1, 1, 128, 128), lambda i: (layer, dr, 0, 0)),
                pl.BlockSpec((1, 1, 1, 128), lambda i: (layer, dr, 0, 0))]
    args = [u, w2, b2]
    if reverse:
        in_specs += [pl.BlockSpec((b, BLK, BRANCH), main), _layer_spec(nw.shape, layer, 1)]
        args += [o_fwd, nw]
    return pl.pallas_call(
        functools.partial(_gla_body, reverse=reverse, n_batch=b),
        grid=(nblk,),
        in_specs=in_specs,
        out_specs=pl.BlockSpec((b, BLK, BRANCH), main),
        out_shape=jax.ShapeDtypeStruct((b, rows, BRANCH), BF16 if reverse else F32),
        scratch_shapes=[pltpu.VMEM((b, BRANCH, 128), F32)],
        compiler_params=_params(1),
        name="gla_bwd" if reverse else "gla_fwd",
    )(*args)


def _lru_body(*refs, reverse, nblk, n_batch):
    if reverse:
        (u_ref, prev_ref, next_ref, cw_ref, cb_ref, wg_ref, bg_ref, lam_ref, hf_ref,
         o_ref, xe_ref, a_ref, s_ref, p_ref, hb_ref, h_ref) = refs
    else:
        (u_ref, prev_ref, next_ref, cw_ref, cb_ref, wg_ref, bg_ref, lam_ref,
         o_ref, xe_ref, a_ref, s_ref, p_ref, h_ref) = refs
    i = pl.program_id(0)
    blk = _bwd_block(i, nblk) if reverse else i

    @pl.when(i == 0)
    def _():
        h_ref[...] = jnp.zeros_like(h_ref)

    for bb in range(n_batch):
        ub = u_ref[bb]
        xc = _short_conv(xe_ref.at[bb], ub[:, 0:BRANCH], prev_ref[bb][:, 0:BRANCH], next_ref[bb][:, 0:BRANCH],
                         blk, nblk, cw_ref[0], cb_ref[0])
        z = _dot(xc.astype(BF16), wg_ref[0, 0]) + bg_ref[0, 0]
        rg = jax.nn.sigmoid(z[:, 0:BRANCH])
        ig = jax.nn.sigmoid(z[:, BRANCH:2 * BRANCH])
        log_a = -LRU_C * rg * jax.nn.softplus(-lam_ref[0, 0])
        a = jnp.exp(log_a)
        a_ref[bb] = a
        s_ref[bb] = jnp.sqrt(jnp.tanh(-log_a) * (1.0 + a * a)) * (ig * xc)

    half = BLK // 2
    dst = hb_ref if reverse else o_ref
    if reverse:
        first, second, lo2 = range(BLK - 1, half - 1, -1), range(half - 1, -1, -1), 0
    else:
        first, second, lo2 = range(0, half), range(half, BLK), half
    hs = [h_ref[bb] for bb in range(n_batch)]
    ps = [None] * n_batch
    qs = [None] * n_batch
    for t1, t2 in zip(first, second):
        for bb in range(n_batch):
            hs[bb] = a_ref[bb, t1:t1 + 1, :] * hs[bb] + s_ref[bb, t1:t1 + 1, :]
            dst[bb, t1:t1 + 1, :] = hs[bb]
            a2 = a_ref[bb, t2:t2 + 1, :]
            s2 = s_ref[bb, t2:t2 + 1, :]
            ps[bb] = a2 if ps[bb] is None else a2 * ps[bb]
            qs[bb] = s2 if qs[bb] is None else a2 * qs[bb] + s2
            dst[bb, t2:t2 + 1, :] = qs[bb]
            p_ref[bb, t2 - lo2:t2 - lo2 + 1, :] = ps[bb]
    for bb in range(n_batch):
        dst[bb, lo2:lo2 + half, :] = dst[bb, lo2:lo2 + half, :] + p_ref[bb] * hs[bb]
        h_ref[bb] = qs[bb] + ps[bb] * hs[bb]
        if reverse:
            o_ref[bb] = ((hf_ref[bb] + hb_ref[bb]) * _silu(u_ref[bb][:, BRANCH:2 * BRANCH])).astype(BF16)


def _lru_call(u, layer, cw, cb, wg, bg, lam, reverse, h_fwd=None):
    b, rows, _ = u.shape
    nblk = rows // BLK
    main, prev, nxt = _pass_specs(nblk, reverse, b, W_LRU)
    dr = 1 if reverse else 0
    in_specs = [pl.BlockSpec((b, BLK, W_LRU), main), prev, nxt,
                _layer_spec(cw.shape, layer, 1), _layer_spec(cb.shape, layer, 1),
                pl.BlockSpec((1, 1, BRANCH, 2 * BRANCH), lambda i: (layer, dr, 0, 0)),
                pl.BlockSpec((1, 1, 1, 2 * BRANCH), lambda i: (layer, dr, 0, 0)),
                pl.BlockSpec((1, 1, 1, BRANCH), lambda i: (layer, dr, 0, 0))]
    args = [u, u, u, cw, cb, wg, bg, lam]
    scratch = [pltpu.VMEM((b, BLK + 2 * HALO, BRANCH), F32), pltpu.VMEM((b, BLK, BRANCH), F32),
               pltpu.VMEM((b, BLK, BRANCH), F32), pltpu.VMEM((b, BLK // 2, BRANCH), F32)]
    if reverse:
        in_specs.append(pl.BlockSpec((b, BLK, BRANCH), main))
        args.append(h_fwd)
        scratch.append(pltpu.VMEM((b, BLK, BRANCH), F32))
    scratch.append(pltpu.VMEM((b, 1, BRANCH), F32))
    return pl.pallas_call(
        functools.partial(_lru_body, reverse=reverse, nblk=nblk, n_batch=b),
        grid=(nblk,),
        in_specs=in_specs,
        out_specs=pl.BlockSpec((b, BLK, BRANCH), main),
        out_shape=jax.ShapeDtypeStruct((b, rows, BRANCH), BF16 if reverse else F32),
        scratch_shapes=scratch,
        compiler_params=_params(1),
        name="lru_bwd" if reverse else "lru_fwd",
    )(*args)


def _ssd_blocks(ubs, xcs, dtb, aneg, sts, reverse):
    nb = len(ubs)
    off = SSD_HEADS if reverse else 0
    ch = SSD_CHUNK
    hpg = SSD_HEADS // SSD_GROUPS
    r = _iota((ch, ch), 0)
    c = _iota((ch, ch), 1)
    keep = (c >= r) if reverse else (c <= r)
    tri = keep.astype(BF16)
    expand = (_iota((128, BRANCH), 0) == off + (_iota((128, BRANCH), 1) >> 6)).astype(BF16)
    lane_n = _iota((ch, 128), 1) >> 6
    lane_p = _iota((ch, BRANCH), 1) >> 6
    bd = (_iota((128, BRANCH), 0) >> 6) == (_iota((128, BRANCH), 1) >> 7)
    edge = 0 if reverse else ch - 1

    n_chunks = BLK // ch
    order = list(reversed(range(n_chunks))) if reverse else list(range(n_chunks))
    cb_keys = [(ci, bb) for ci in order for bb in range(nb)]
    rows = lambda a, ci: a[ci * ch:(ci + 1) * ch]
    xs = {k: rows(xcs[k[1]][:, 0:256], k[0]) for k in cb_keys}
    bm = {k: rows(xcs[k[1]][:, 256:384], k[0]).astype(BF16) for k in cb_keys}
    cm = {k: rows(xcs[k[1]][:, 384:512], k[0]) for k in cb_keys}
    dt = {k: jax.nn.softplus(rows(ubs[k[1]][:, 768:896], k[0]) + dtb) for k in cb_keys}

    cum = {k: _dot_exact_left(tri, dt[k] * aneg) for k in cb_keys}
    cum_x = {k: _dot_exact_right(cum[k], expand, terms=2) for k in cb_keys}
    dt_x = {k: _dot_exact_right(dt[k], expand, terms=2) for k in cb_keys}
    cum_t = {k: cum[k].T for k in cb_keys}
    dt_t = {k: dt[k].T for k in cb_keys}
    cbm = {(k, g): _dot_nt(jnp.where(lane_n == g, cm[k], 0.0).astype(BF16), bm[k])
           for k in cb_keys for g in range(SSD_GROUPS)}

    def decay_mix(k, h):
        seg = cum[k][:, off + h:off + h + 1] - cum_t[k][off + h:off + h + 1, :]
        lm = jnp.exp(jnp.where(keep, seg, -1e30))
        return (cbm[(k, h // hpg)] * lm * dt_t[k][off + h:off + h + 1, :]).astype(BF16)

    y_in = {}
    for k in cb_keys:
        m_cat = jnp.concatenate([decay_mix(k, h) for h in range(SSD_HEADS)], axis=1)
        x_bd = jnp.concatenate([jnp.where(lane_p == h, xs[k], 0.0) for h in range(SSD_HEADS)], axis=0)
        y_in[k] = _dot(m_cat, x_bd.astype(BF16))
    tot_x = {k: cum_x[k][edge:edge + 1, :] for k in cb_keys}
    upd = {k: jnp.where(bd, _dot_tn(bm[k], (xs[k] * (jnp.exp(tot_x[k] - cum_x[k]) * dt_x[k])).astype(BF16)), 0.0)
           for k in cb_keys}
    sts = list(sts)
    outs = [[None] * n_chunks for _ in range(nb)]
    for k in cb_keys:
        ci, bb = k
        outs[bb][ci] = y_in[k] + _dot(cm[k].astype(BF16), sts[bb].astype(BF16)) * jnp.exp(cum_x[k])
        sts[bb] = sts[bb] * jnp.exp(tot_x[k]) + upd[k]
    return [jnp.concatenate(o, axis=0) for o in outs], sts


def _ssd_body(*refs, reverse, nblk, n_batch):
    if reverse:
        (u_ref, prev_ref, next_ref, cw_ref, cb_ref, dtb_ref, an_ref, yf_ref, dsk_ref, nw_ref,
         o_ref, xe_ref, st_ref) = refs
    else:
        u_ref, prev_ref, next_ref, cw_ref, cb_ref, dtb_ref, an_ref, o_ref, xe_ref, st_ref = refs
    i = pl.program_id(0)
    blk = _bwd_block(i, nblk) if reverse else i

    @pl.when(i == 0)
    def _():
        st_ref[...] = jnp.zeros_like(st_ref)

    ubs = [u_ref[bb] for bb in range(n_batch)]
    xcs = [_silu(_short_conv(xe_ref.at[bb], ubs[bb][:, 0:512], prev_ref[bb][:, 0:512], next_ref[bb][:, 0:512],
                             blk, nblk, cw_ref[0], cb_ref[0])) for bb in range(n_batch)]
    ys, sts = _ssd_blocks(ubs, xcs, dtb_ref[0], an_ref[0], [st_ref[bb] for bb in range(n_batch)], reverse)
    for bb in range(n_batch):
        st_ref[bb] = sts[bb]
        if reverse:
            y = ys[bb] + yf_ref[bb] + dsk_ref[0] * xcs[bb][:, 0:256]
            yz = y * _silu(ubs[bb][:, 512:768])
            ms = jnp.mean(yz * yz, axis=-1, keepdims=True)
            o_ref[bb] = (yz * lax.rsqrt(ms + EPS) * nw_ref[0]).astype(BF16)
        else:
            o_ref[bb] = ys[bb]


def _ssd_call(u, layer, cw, cb, dtb, aneg, reverse, y_fwd=None, dskip=None, nw=None):
    b, rows, _ = u.shape
    nblk = rows // BLK
    main, prev, nxt = _pass_specs(nblk, reverse, b, W_SSD)
    in_specs = [pl.BlockSpec((b, BLK, W_SSD), main), prev, nxt,
                _layer_spec(cw.shape, layer, 1), _layer_spec(cb.shape, layer, 1),
                _layer_spec(dtb.shape, layer, 1), _layer_spec(aneg.shape, layer, 1)]
    args = [u, u, u, cw, cb, dtb, aneg]
    if reverse:
        in_specs += [pl.BlockSpec((b, BLK, BRANCH), main),
                     _layer_spec(dskip.shape, layer, 1), _layer_spec(nw.shape, layer, 1)]
        args += [y_fwd, dskip, nw]
    return pl.pallas_call(
        functools.partial(_ssd_body, reverse=reverse, nblk=nblk, n_batch=b),
        grid=(nblk,),
        in_specs=in_specs,
        out_specs=pl.BlockSpec((b, BLK, BRANCH), main),
        out_shape=jax.ShapeDtypeStruct((b, rows, BRANCH), BF16 if reverse else F32),
        scratch_shapes=[pltpu.VMEM((b, BLK + 2 * HALO, 512), F32), pltpu.VMEM((b, 128, BRANCH), F32)],
        compiler_params=_params(1),
        name="ssd_bwd" if reverse else "ssd_fwd",
    )(*args)


def _key_step(nk):
    return max(t for t in range(BLK, min(ATT_TK, nk) + 1, BLK) if nk % t == 0)


def _attn_body(*refs, nk, lam_init, n_qb):
    qt_refs, (k_ref, vt_ref), g_refs = refs[0:n_qb], refs[n_qb:n_qb + 2], refs[n_qb + 2:2 * n_qb + 2]
    lam_ref, sw_ref, o_ref, s0_ref, s1_ref = refs[2 * n_qb + 2:]
    tk = _key_step(nk)
    n_steps = nk // tk
    sub = tk // BLK
    fold = 32
    lv = lam_ref[0]
    lam = (jnp.exp(jnp.sum(lv[0:1] * lv[1:2], axis=-1, keepdims=True))
           - jnp.exp(jnp.sum(lv[2:3] * lv[3:4], axis=-1, keepdims=True)) + lam_init)
    row = _iota((128, BLK), 0)
    pairs = [(qb, h, half) for qb in range(n_qb) for h in range(DIFF_HEADS) for half in range(2)]
    s_bufs = (s0_ref, s1_ref)

    def masked_qt(p):
        qb, h, half = pairs[p]
        qt = qt_refs[qb][0, 0, (h // 2) * 128:(h // 2 + 1) * 128, :]
        lo = (h % 2) * 64 + half * DIFF_D
        return jnp.where(jnp.logical_and(row >= lo, row < lo + DIFF_D), qt, jnp.zeros_like(qt))

    def key_start(j):
        return j * tk if isinstance(j, int) else pl.multiple_of(j * tk, BLK)

    def stage_a(p, qm, j, jj, mx):
        tile = pairs[p][1] // 2
        rows = pl.ds(key_start(j) + jj * BLK, BLK)
        s = _dot(k_ref[0, rows, tile * 128:(tile + 1) * 128], qm)
        s_bufs[p % 2][rows, :] = s
        return jnp.maximum(mx, jnp.max(s.reshape(BLK // fold, fold, BLK), axis=0))

    def stage_bc(p, m, j, jj, ls, acc):
        h = pairs[p][1]
        e = jnp.exp2(s_bufs[p % 2][pl.ds(key_start(j) + jj * BLK, BLK), :] - m)
        ls = ls + jnp.sum(e.reshape(BLK // fold, fold, BLK), axis=0)
        acc = acc + _dot(vt_ref[0, j * sub + jj, h * DIFF_DV:(h + 1) * DIFF_DV, :], e.astype(BF16))
        return ls, acc

    results = []
    m_prev = None
    for p in range(len(pairs) + 1):
        do_a = p < len(pairs)
        do_bc = p >= 1
        qm = masked_qt(p) if do_a else None

        def step(j, carry, p=p, qm=qm, m_prev=m_prev, do_a=do_a, do_bc=do_bc):
            mx, ls, acc = carry
            for jj in range(sub):
                if do_bc:
                    ls, acc = stage_bc(p - 1, m_prev, j, jj, ls, acc)
                if do_a:
                    mx = stage_a(p, qm, j, jj, mx)
            return mx, ls, acc

        carry = (jnp.full((fold, BLK), -jnp.inf, F32), jnp.zeros((fold, BLK), F32),
                 jnp.zeros((DIFF_DV, BLK), F32))
        if n_steps > 1:
            mx, ls, acc = lax.fori_loop(0, n_steps, step, carry)
        else:
            mx, ls, acc = step(0, carry)
        if do_bc:
            results.append(acc / jnp.sum(ls, axis=0, keepdims=True))
        if do_a:
            m_prev = jnp.max(mx, axis=0, keepdims=True)

    grp = ((_iota((BRANCH, BRANCH), 0) >> 6) == (_iota((BRANCH, BRANCH), 1) >> 6)).astype(BF16)
    per_qb = 2 * DIFF_HEADS
    for qb in range(n_qb):
        res = results[qb * per_qb:(qb + 1) * per_qb]
        heads = [res[2 * h] - lam * res[2 * h + 1] for h in range(DIFF_HEADS)]
        o = jnp.concatenate(heads, axis=0).T
        ms = _dot_exact_right(o * o, grp, terms=2) * (1.0 / DIFF_DV)
        y = o * lax.rsqrt(ms + EPS) * sw_ref[0] * (1.0 - lam_init)
        o_ref[0, qb * BLK:(qb + 1) * BLK, :] = (y * _silu(g_refs[qb][0])).astype(BF16)


def _attn_call(qt, k, vt, gate, layer, lam_vecs, sw, lam_init, ctx_only):
    b, rows, _ = k.shape
    nblk = rows // BLK
    nk = BLK if ctx_only else rows
    nq = 1 if ctx_only else nblk - 1
    qoff = 0 if ctx_only else 1
    n_qb = max(t for t in (1, 2, 4) if nq % t == 0)
    q_specs = [pl.BlockSpec((1, 1, BRANCH, BLK), lambda bb, i, t=t: (bb, n_qb * i + t + qoff, 0, 0))
               for t in range(n_qb)]
    g_specs = [pl.BlockSpec((1, BLK, BRANCH), lambda bb, i, t=t: (bb, n_qb * i + t + qoff, 0))
               for t in range(n_qb)]
    return pl.pallas_call(
        functools.partial(_attn_body, nk=nk, lam_init=lam_init, n_qb=n_qb),
        grid=(b, nq // n_qb),
        in_specs=q_specs + [pl.BlockSpec((1, nk, BRANCH), lambda bb, i: (bb, 0, 0)),
                            pl.BlockSpec((1, nk // BLK, BRANCH, BLK), lambda bb, i: (bb, 0, 0, 0))]
        + g_specs + [_layer_spec(lam_vecs.shape, layer, 2), _layer_spec(sw.shape, layer, 2)],
        out_specs=pl.BlockSpec((1, n_qb * BLK, BRANCH), lambda bb, i: (bb, i, 0)),
        out_shape=jax.ShapeDtypeStruct((b, nq * BLK, BRANCH), BF16),
        scratch_shapes=[pltpu.VMEM((nk, BLK), F32), pltpu.VMEM((nk, BLK), F32)],
        compiler_params=_params(2),
        name="diff_attn_ctx" if ctx_only else "diff_attn",
    )(*([qt] * n_qb), k, vt, *([gate] * n_qb), lam_vecs, sw)


W_IN_ROWS = 128


def _w_in_layout_body(w_ref, o_ref):
    w = w_ref[0]

    def group(lo, pieces, width):
        parts = [w[:, lo + a:lo + b] for a, b in pieces]
        used = sum(b - a for a, b in pieces)
        if used < width:
            parts.append(jnp.zeros((W_IN_ROWS, width - used), F32))
        return parts

    parts = (group(COL_GLA, [(0, 512), (544, 800), (512, 544)], W_GLA)
             + group(COL_LRU, [(0, 512)], W_LRU)
             + group(COL_DIFF, [(0, 1024)], W_DIFF)
             + group(COL_SSD, [(0, 512), (520, 776), (512, 520)], W_SSD))
    o_ref[0] = jnp.concatenate(parts, axis=-1).astype(BF16)


def _w_in_layout_call(w_in):
    depth, d, n = w_in.shape
    return pl.pallas_call(
        _w_in_layout_body,
        grid=(depth, d // W_IN_ROWS),
        in_specs=[pl.BlockSpec((1, W_IN_ROWS, n), lambda l, r: (l, r, 0))],
        out_specs=pl.BlockSpec((1, W_IN_ROWS, W_ALL), lambda l, r: (l, r, 0)),
        out_shape=jax.ShapeDtypeStruct((depth, d, W_ALL), BF16),
        compiler_params=_params(2),
        name="w_in_layout",
    )(w_in)


def _block_diag(w):
    nb, n = w.shape[-3], w.shape[-1]
    eye = jnp.eye(nb, dtype=w.dtype)
    dense = eye[:, None, :, None] * w[..., :, :, None, :]
    return dense.reshape(w.shape[:-3] + (nb * n, nb * n))


def _rope_tables(t_lat):
    n_freq = DIFF_D // 4
    inv = ROPE_BASE ** (-jnp.arange(n_freq, dtype=F32) / n_freq)
    tpos = jnp.arange(t_lat)
    ang_r = (tpos // GRID_W).astype(F32)[:, None] * inv
    ang_c = (tpos % GRID_W).astype(F32)[:, None] * inv
    ang = jnp.concatenate([ang_r, ang_r, ang_c, ang_c], axis=-1)
    reps = 128 // DIFF_D
    sign = jnp.where((jnp.arange(DIFF_D) % 16) < 8, -1.0, 1.0).astype(F32)
    return jnp.tile(jnp.cos(ang), (1, reps)), jnp.tile(jnp.sin(ang) * sign, (1, reps))


def kernel(x, c, ctx, c_ctx, w_mod, b_mod, norm_w, w_in, w_out, gla_w2, gla_b2, gla_norm_w, lru_conv_w, lru_conv_b, lru_wa, lru_ba, lru_wx, lru_bx, lru_lam, diff_lam, diff_subln_w, ssd_conv_w, ssd_conv_b, ssd_dt_bias, ssd_a_log, ssd_d, ssd_norm_w, final_norm_w):
    bsz, t_lat, d = x.shape
    depth = w_mod.shape[0]
    assert ctx.shape[1] == BLK and t_lat % BLK == 0 and d == D_MODEL and bsz + 1 <= 8
    nblk = 1 + t_lat // BLK

    cvec = jnp.concatenate([c, c_ctx[None], jnp.zeros((8 - bsz - 1, d), F32)], axis=0)
    mod = _mod_call(cvec, w_mod, b_mod)
    cos, sin = _rope_tables(t_lat)

    nw3 = norm_w[:, None, :]
    w_in_p = _w_in_layout_call(w_in)
    w_out_b = w_out.astype(BF16)
    pad_r = lambda w, lo: jnp.pad(w, ((0, 0), (lo, 128 - GLA_RANK - lo), (0, 0)))
    w2 = jnp.stack([pad_r(gla_w2[:, 0], 0), pad_r(gla_w2[:, 1], GLA_RANK)], axis=1).astype(BF16)
    b2 = gla_b2[:, :, None, :]
    gnw = jnp.tile(gla_norm_w, (1, GLA_HEADS))[:, None, :]
    wg = jnp.concatenate([_block_diag(lru_wa), _block_diag(lru_wx)], axis=-1).astype(BF16)
    bg = jnp.concatenate([lru_ba, lru_bx], axis=-1)[:, :, None, :]
    llam = lru_lam[:, :, None, :]
    lcb = lru_conv_b[:, None, :]
    sw = jnp.tile(diff_subln_w, (1, DIFF_HEADS))[:, None, :]
    pad_l = lambda v: jnp.pad(v.reshape(depth, -1), ((0, 0), (0, 128 - 2 * SSD_HEADS)))[:, None, :]
    dtb = pad_l(ssd_dt_bias)
    aneg = pad_l(-jnp.exp(ssd_a_log))
    dsk = jnp.repeat(ssd_d, SSD_P, axis=-1)[:, None, :]
    scb = ssd_conv_b[:, None, :]
    snw = ssd_norm_w[:, None, :]

    lat, lat_off, cx = x, 0, ctx
    proj = _inproj_call(lat, lat_off, cx, 0, nblk, nw3, mod, w_in_p, cos, sin)
    for l in range(depth):
        last = l == depth - 1
        lam_init = 0.8 - 0.6 * math.exp(-0.3 * l)
        ua, ub, ud, qt, kp, vt, gd = proj

        of = _gla_call(ua, l, w2, b2, False)
        ya = _gla_call(ua, l, w2, b2, True, of, gnw)

        hf = _lru_call(ub, l, lru_conv_w, lcb, wg, bg, llam, False)
        yb = _lru_call(ub, l, lru_conv_w, lcb, wg, bg, llam, True, hf)

        yc_lat = _attn_call(qt, kp, vt, gd, l, diff_lam, sw, lam_init, False)
        yc_ctx = yc_lat if last else _attn_call(qt, kp, vt, gd, l, diff_lam, sw, lam_init, True)

        yf = _ssd_call(ud, l, ssd_conv_w, scb, dtb, aneg, False)
        yd = _ssd_call(ud, l, ssd_conv_w, scb, dtb, aneg, True, yf, dsk, snw)

        if last:
            return _outproj_final_call(ya, yb, yc_lat, yd, lat, lat_off, cx, l, nblk, mod, w_out_b,
                                       final_norm_w[None])
        h, *proj = _outproj_next_call(ya, yb, yc_lat, yc_ctx, yd, lat, lat_off, cx, l, nblk, mod, w_out_b,
                                      nw3, w_in_p, cos, sin)
        lat, lat_off, cx = h, 1, h
```

```python
import functools
import math

import jax
import jax.numpy as jnp
from jax import lax
from jax.experimental import pallas as pl
from jax.experimental.pallas import tpu as pltpu

F32 = jnp.float32
BF16 = jnp.bfloat16

D_MODEL = 1024
BRANCH = D_MODEL // 4
GRID_W = 64
EPS = 1e-6
CONV_W = 4
BLK = 256
HALO = 8
GLA_HEADS, GLA_DK, GLA_DV = 4, 32, 64
GLA_RANK = 16
GLA_TAU = 16.0
GLA_CHUNK = 64
LRU_C = 8.0
DIFF_HEADS, DIFF_D, DIFF_DV = 4, 32, 64
ROPE_BASE = 10000.0
SSD_HEADS, SSD_P, SSD_N, SSD_GROUPS = 4, 64, 64, 2
SSD_CHUNK = 128
ATT_TK = 2816
W_GLA, W_LRU, W_DIFF, W_SSD = 896, 512, 1024, 896
COL_GLA, COL_LRU, COL_DIFF, COL_SSD = 0, 800, 1312, 2336
W_ALL = W_GLA + W_LRU + W_DIFF + W_SSD
VMEM_LIMIT = 56 * 1024 * 1024


def _dot(a, b):
    return jnp.dot(a, b, preferred_element_type=F32)


def _dot_nt(a, b):
    return lax.dot_general(a, b, (((1,), (1,)), ((), ())), preferred_element_type=F32)


def _dot_tn(a, b):
    return lax.dot_general(a, b, (((0,), (0,)), ((), ())), preferred_element_type=F32)


def _split(x, terms):
    out = []
    r = x
    for _ in range(terms):
        p = r.astype(BF16)
        out.append(p)
        r = r - p.astype(F32)
    return out


def _dot_exact_left(m, x, terms=3):
    return sum(_dot(m, p) for p in _split(x, terms))


def _dot_exact_right(x, m, terms=3):
    return sum(_dot(p, m) for p in _split(x, terms))


def _iota(shape, dim):
    return lax.broadcasted_iota(jnp.int32, shape, dim)


def _silu(x):
    return x * jax.nn.sigmoid(x)


def _params(n_axes):
    return pltpu.CompilerParams(dimension_semantics=("arbitrary",) * n_axes, vmem_limit_bytes=VMEM_LIMIT)


def _layer_spec(shape, layer, n_axes):
    block = (1,) + tuple(shape[1:])
    idx = (layer,) + (0,) * (len(shape) - 1)
    if n_axes == 1:
        return pl.BlockSpec(block, lambda i: idx)
    return pl.BlockSpec(block, lambda bb, i: idx)


def _bwd_block(i, nblk):
    return jnp.where(i == 0, 0, nblk - i)


def _mod_body(c_ref, w_ref, b_ref, o_ref):
    s = _silu(c_ref[...])
    o_ref[0] = _dot(s.astype(BF16), w_ref[0].astype(BF16)) + b_ref[0]


def _mod_call(cvec, w_mod, b_mod):
    depth, d, n = w_mod.shape
    tn = 1024
    return pl.pallas_call(
        _mod_body,
        grid=(depth, n // tn),
        in_specs=[pl.BlockSpec((8, d), lambda l, j: (0, 0)),
                  pl.BlockSpec((1, d, tn), lambda l, j: (l, 0, j)),
                  pl.BlockSpec((1, 1, tn), lambda l, j: (l, 0, j))],
        out_specs=pl.BlockSpec((1, 8, tn), lambda l, j: (l, 0, j)),
        out_shape=jax.ShapeDtypeStruct((depth, 8, n), F32),
        compiler_params=_params(2),
        name="adaln_mod",
    )(cvec, w_mod, b_mod.reshape(depth, 1, n))


def _mod_row(mod_ref, is_ctx, bb, n_batch, part):
    r = jnp.where(is_ctx, n_batch, bb)
    return mod_ref[0, pl.ds(r, 1), part * D_MODEL:(part + 1) * D_MODEL]


def _project_in(x, is_ctx, bb, n_batch, nw_ref, mod_ref, w_ref, cos_ref, sin_ref, outs):
    oa_ref, ob_ref, od_ref, qt_ref, k_ref, vt_ref, g_ref = outs
    ms = jnp.mean(x * x, axis=-1, keepdims=True)
    xn = x * lax.rsqrt(ms + EPS) * nw_ref[0]
    xm = xn * (1.0 + _mod_row(mod_ref, is_ctx, bb, n_batch, 1)) + _mod_row(mod_ref, is_ctx, bb, n_batch, 0)
    u = _dot(xm.astype(BF16), w_ref[0])
    oa_ref[0] = u[:, 0:W_GLA]
    ob_ref[0] = u[:, W_GLA:W_GLA + W_LRU]
    od_ref[0] = u[:, W_GLA + W_LRU + W_DIFF:W_ALL]

    ud = u[:, W_GLA + W_LRU:W_GLA + W_LRU + W_DIFF]
    lane_tiles = BRANCH // 128
    cs = jnp.concatenate([jnp.where(is_ctx, 1.0, cos_ref[...])] * lane_tiles, axis=1)
    sn = jnp.concatenate([jnp.where(is_ctx, 0.0, sin_ref[...])] * lane_tiles, axis=1)
    first = (_iota((BLK, BRANCH), 1) & 15) < 8

    def rope(t):
        rot = jnp.where(first, pltpu.roll(t, BRANCH - 8, 1), pltpu.roll(t, 8, 1))
        return t * cs + rot * sn

    qt_ref[0, 0] = (rope(ud[:, 0:256]) * (DIFF_D ** -0.5 * math.log2(math.e))).T.astype(BF16)
    k_ref[0] = rope(ud[:, 256:512]).astype(BF16)
    vt_ref[0, 0] = ud[:, 512:768].T.astype(BF16)
    g_ref[0] = ud[:, 768:1024]


def _inproj_body(lat_ref, ctx_ref, nw_ref, mod_ref, w_ref, cos_ref, sin_ref, *outs, n_batch):
    is_ctx = pl.program_id(1) == 0
    x = jnp.where(is_ctx, ctx_ref[0], lat_ref[0])
    _project_in(x, is_ctx, pl.program_id(0), n_batch, nw_ref, mod_ref, w_ref, cos_ref, sin_ref, outs)


def _inproj_specs(b, nblk, layer, nw, mod, w):
    rows = nblk * BLK
    row_out = lambda n: pl.BlockSpec((1, BLK, n), lambda bb, i: (bb, i, 0))
    t_out = pl.BlockSpec((1, 1, BRANCH, BLK), lambda bb, i: (bb, i, 0, 0))
    tab = pl.BlockSpec((BLK, 128), lambda bb, i: (jnp.maximum(i - 1, 0), 0))
    in_specs = [_layer_spec(nw.shape, layer, 2), _layer_spec(mod.shape, layer, 2), _layer_spec(w.shape, layer, 2),
                tab, tab]
    out_specs = [row_out(W_GLA), row_out(W_LRU), row_out(W_SSD), t_out, row_out(BRANCH), t_out, row_out(BRANCH)]
    out_shape = [jax.ShapeDtypeStruct((b, rows, W_GLA), F32),
                 jax.ShapeDtypeStruct((b, rows, W_LRU), F32),
                 jax.ShapeDtypeStruct((b, rows, W_SSD), F32),
                 jax.ShapeDtypeStruct((b, nblk, BRANCH, BLK), BF16),
                 jax.ShapeDtypeStruct((b, rows, BRANCH), BF16),
                 jax.ShapeDtypeStruct((b, nblk, BRANCH, BLK), BF16),
                 jax.ShapeDtypeStruct((b, rows, BRANCH), F32)]
    return in_specs, out_specs, out_shape


def _inproj_call(lat, lat_off, ctx, layer, nblk, nw, mod, w, cos, sin):
    b = lat.shape[0]
    d = D_MODEL
    p_specs, out_specs, out_shape = _inproj_specs(b, nblk, layer, nw, mod, w)
    return pl.pallas_call(
        functools.partial(_inproj_body, n_batch=b),
        grid=(b, nblk),
        in_specs=[pl.BlockSpec((1, BLK, d), lambda bb, i: (bb, jnp.maximum(i - 1, 0) + lat_off, 0)),
                  pl.BlockSpec((1, BLK, d), lambda bb, i: (bb, 0, 0))] + p_specs,
        out_specs=out_specs,
        out_shape=out_shape,
        compiler_params=_params(2),
        name="norm_inproj",
    )(lat, ctx, nw, mod, w, cos, sin)


def _residual_update(ya_ref, yb_ref, ycl_ref, ycc_ref, yd_ref, lat_ref, ctx_ref, mod_ref, w_ref, is_ctx, bb, n_batch):
    if is_ctx is False:
        yc, res = ycl_ref[0], lat_ref[0]
    else:
        yc = jnp.where(is_ctx, ycc_ref[0], ycl_ref[0])
        res = jnp.where(is_ctx, ctx_ref[0], lat_ref[0])
    y = jnp.concatenate([ya_ref[0], yb_ref[0], yc, yd_ref[0]], axis=-1)
    return res + _mod_row(mod_ref, is_ctx, bb, n_batch, 2) * _dot(y, w_ref[0])


def _outproj_final_body(ya_ref, yb_ref, ycl_ref, ycc_ref, yd_ref, lat_ref, ctx_ref, mod_ref, w_ref, fw_ref, o_ref,
                        *, n_batch):
    hn = _residual_update(ya_ref, yb_ref, ycl_ref, ycc_ref, yd_ref, lat_ref, ctx_ref, mod_ref, w_ref,
                          False, pl.program_id(0), n_batch)
    ms = jnp.mean(hn * hn, axis=-1, keepdims=True)
    o_ref[0] = hn * lax.rsqrt(ms + EPS) * fw_ref[...]


def _outproj_next_body(ya_ref, yb_ref, ycl_ref, ycc_ref, yd_ref, lat_ref, ctx_ref, modg_ref, wo_ref,
                       nw_ref, mod_ref, wi_ref, cos_ref, sin_ref, h_ref, *outs, n_batch):
    bb = pl.program_id(0)
    is_ctx = pl.program_id(1) == 0
    hn = _residual_update(ya_ref, yb_ref, ycl_ref, ycc_ref, yd_ref, lat_ref, ctx_ref, modg_ref, wo_ref,
                          is_ctx, bb, n_batch)
    h_ref[0] = hn
    _project_in(hn, is_ctx, bb, n_batch, nw_ref, mod_ref, wi_ref, cos_ref, sin_ref, outs)


def _outproj_in_specs(lat_off, layer, mod, w, final):
    d = D_MODEL
    off = 1 if final else 0
    y_spec = pl.BlockSpec((1, BLK, BRANCH), lambda bb, i: (bb, i + off, 0))
    lat_blk = (lambda i: i) if final else (lambda i: jnp.maximum(i - 1, 0))
    return [y_spec, y_spec,
            pl.BlockSpec((1, BLK, BRANCH), lambda bb, i: (bb, lat_blk(i), 0)),
            pl.BlockSpec((1, BLK, BRANCH), lambda bb, i: (bb, 0, 0)),
            y_spec,
            pl.BlockSpec((1, BLK, d), lambda bb, i: (bb, lat_blk(i) + lat_off, 0)),
            pl.BlockSpec((1, BLK, d), lambda bb, i: (bb, 0, 0)),
            _layer_spec(mod.shape, layer, 2), _layer_spec(w.shape, layer, 2)]


def _outproj_final_call(ya, yb, yc_lat, yd, lat, lat_off, ctx, layer, nblk, mod, w, fw):
    b = lat.shape[0]
    d = D_MODEL
    nout = nblk - 1
    return pl.pallas_call(
        functools.partial(_outproj_final_body, n_batch=b),
        grid=(b, nout),
        in_specs=_outproj_in_specs(lat_off, layer, mod, w, True) + [pl.BlockSpec((1, d), lambda bb, i: (0, 0))],
        out_specs=pl.BlockSpec((1, BLK, d), lambda bb, i: (bb, i, 0)),
        out_shape=jax.ShapeDtypeStruct((b, nout * BLK, d), F32),
        compiler_params=_params(2),
        name="outproj_final",
    )(ya, yb, yc_lat, yc_lat, yd, lat, ctx, mod, w, fw)


def _outproj_next_call(ya, yb, yc_lat, yc_ctx, yd, lat, lat_off, ctx, layer, nblk, mod, w_out, nw, w_in, cos, sin):
    b = lat.shape[0]
    d = D_MODEL
    p_specs, p_out_specs, p_out_shape = _inproj_specs(b, nblk, layer + 1, nw, mod, w_in)
    return pl.pallas_call(
        functools.partial(_outproj_next_body, n_batch=b),
        grid=(b, nblk),
        in_specs=_outproj_in_specs(lat_off, layer, mod, w_out, False) + p_specs,
        out_specs=[pl.BlockSpec((1, BLK, d), lambda bb, i: (bb, i, 0))] + p_out_specs,
        out_shape=[jax.ShapeDtypeStruct((b, nblk * BLK, d), F32)] + p_out_shape,
        compiler_params=_params(2),
        name="outproj_inproj",
    )(ya, yb, yc_lat, yc_ctx, yd, lat, ctx, mod, w_out, nw, mod, w_in, cos, sin)


def _pass_specs(nblk, reverse, n_batch, width):
    def blk_of(i):
        return _bwd_block(i, nblk) if reverse else i
    per_blk = BLK // HALO
    main = lambda i: (0, blk_of(i), 0)
    prev = pl.BlockSpec((n_batch, HALO, width), lambda i: (0, jnp.maximum(blk_of(i) * per_blk - 1, 0), 0))
    nxt = pl.BlockSpec((n_batch, HALO, width),
                       lambda i: (0, jnp.minimum((blk_of(i) + 1) * per_blk, nblk * per_blk - 1), 0))
    return main, prev, nxt


def _short_conv(xe_ref, x, prev, nxt, blk, nblk, cw, cb):
    prev_ok = blk >= 2
    next_ok = jnp.logical_and(blk >= 1, blk <= nblk - 2)
    xe_ref[0:HALO, :] = jnp.where(prev_ok, prev, 0.0)
    xe_ref[HALO:HALO + BLK, :] = x
    xe_ref[HALO + BLK:2 * HALO + BLK, :] = jnp.where(next_ok, nxt, 0.0)
    left = CONV_W // 2
    y = cb + xe_ref[HALO - left:HALO - left + BLK, :] * cw[0:1, :]
    for j in range(1, CONV_W):
        y = y + xe_ref[HALO - left + j:HALO - left + j + BLK, :] * cw[j:j + 1, :]
    return y


def _gla_blocks(ubs, w2, b2, sts, reverse):
    nb = len(ubs)
    r = _iota((BLK, BLK), 0)
    c = _iota((BLK, BLK), 1)
    same = (r >> 6) == (c >> 6)
    tri = jnp.logical_and(same, (c >= r) if reverse else (c <= r))
    sel = jnp.concatenate([tri, same], axis=0).astype(BF16)
    lane_k = _iota((GLA_CHUNK, 128), 1) >> 5
    rr = _iota((GLA_HEADS * GLA_CHUNK, GLA_CHUNK), 0) & (GLA_CHUNK - 1)
    cc = _iota((GLA_HEADS * GLA_CHUNK, GLA_CHUNK), 1)
    amask = (cc >= rr) if reverse else (cc <= rr)
    lane_v = _iota((GLA_CHUNK, BRANCH), 1) >> 6
    bd = (_iota((BRANCH, 128), 0) >> 6) == (_iota((BRANCH, 128), 1) >> 5)

    las = [jax.nn.log_sigmoid(_dot(ub[:, 768:896].astype(BF16), w2) + b2) * (1.0 / GLA_TAU) for ub in ubs]
    ggs = [_dot_exact_left(sel, la, terms=2) for la in las]
    gs = [gg[0:BLK] for gg in ggs]
    gls = [gg[BLK:2 * BLK] for gg in ggs]
    qgs = [ub[:, 0:128] * (GLA_DK ** -0.5) * jnp.exp(g) for ub, g in zip(ubs, gs)]
    kgs = [(ub[:, 128:256] * jnp.exp(-g)).astype(BF16) for ub, g in zip(ubs, gs)]
    kds = [(ub[:, 128:256] * jnp.exp(gl - g)).astype(BF16) for ub, g, gl in zip(ubs, gs, gls)]
    egls = [jnp.exp(gl) for gl in gls]
    vs = [ub[:, 256:512].astype(BF16) for ub in ubs]

    n_chunks = BLK // GLA_CHUNK
    order = list(reversed(range(n_chunks))) if reverse else list(range(n_chunks))
    rows = lambda a, ci: a[ci * GLA_CHUNK:(ci + 1) * GLA_CHUNK]
    cb = [(ci, bb) for ci in order for bb in range(nb)]

    def stacked_q(ci, bb):
        qg_c = rows(qgs[bb], ci)
        return jnp.concatenate([jnp.where(lane_k == h, qg_c, 0.0) for h in range(GLA_HEADS)], axis=0).astype(BF16)

    att = {k: jnp.where(amask, _dot_nt(stacked_q(*k), rows(kgs[k[1]], k[0])), 0.0).astype(BF16) for k in cb}
    res = {k: _dot(att[k], rows(vs[k[1]], k[0])) for k in cb}
    upd = {k: jnp.where(bd, _dot_tn(rows(vs[k[1]], k[0]), rows(kds[k[1]], k[0])), 0.0) for k in cb}
    sts = list(sts)
    outs = [[None] * n_chunks for _ in range(nb)]
    for ci, bb in cb:
        o_c = _dot_nt(rows(qgs[bb], ci).astype(BF16), sts[bb].astype(BF16))
        for h in range(GLA_HEADS):
            o_c = o_c + jnp.where(lane_v == h, res[(ci, bb)][h * GLA_CHUNK:(h + 1) * GLA_CHUNK], 0.0)
        sts[bb] = sts[bb] * egls[bb][ci * GLA_CHUNK:ci * GLA_CHUNK + 1, :] + upd[(ci, bb)]
        outs[bb][ci] = o_c
    return [jnp.concatenate(o, axis=0) for o in outs], sts


def _gla_body(*refs, reverse, n_batch):
    if reverse:
        u_ref, w2_ref, b2_ref, of_ref, nw_ref, o_ref, st_ref = refs
    else:
        u_ref, w2_ref, b2_ref, o_ref, st_ref = refs

    @pl.when(pl.program_id(0) == 0)
    def _():
        st_ref[...] = jnp.zeros_like(st_ref)

    ubs = [u_ref[bb] for bb in range(n_batch)]
    os_, sts = _gla_blocks(ubs, w2_ref[0, 0], b2_ref[0, 0], [st_ref[bb] for bb in range(n_batch)], reverse)
    if reverse:
        grp = ((_iota((BRANCH, BRANCH), 0) >> 6) == (_iota((BRANCH, BRANCH), 1) >> 6)).astype(BF16)
        os_ = [o + of_ref[bb] for bb, o in enumerate(os_)]
        mss = [_dot_exact_right(o * o, grp, terms=2) * (1.0 / GLA_DV) for o in os_]
        os_ = [(o * lax.rsqrt(ms + EPS) * nw_ref[0] * _silu(ub[:, 512:768])).astype(BF16)
               for o, ms, ub in zip(os_, mss, ubs)]
    for bb in range(n_batch):
        st_ref[bb] = sts[bb]
        o_ref[bb] = os_[bb]


def _gla_call(u, layer, w2, b2, reverse, o_fwd=None, nw=None):
    b, rows, _ = u.shape
    nblk = rows // BLK
    main, _, _ = _pass_specs(nblk, reverse, b, W_GLA)
    dr = 1 if reverse else 0
    in_specs = [pl.BlockSpec((b, BLK, W_GLA), main),
                pl.BlockSpec((1, 1, 128, 128), lambda i: (layer, dr, 0, 0)),
                pl.BlockSpec((1, 1, 1, 128), lambda i: (layer, dr, 0, 0))]
    args = [u, w2, b2]
    if reverse:
        in_specs += [pl.BlockSpec((b, BLK, BRANCH), main), _layer_spec(nw.shape, layer, 1)]
        args += [o_fwd, nw]
    return pl.pallas_call(
        functools.partial(_gla_body, reverse=reverse, n_batch=b),
        grid=(nblk,),
        in_specs=in_specs,
        out_specs=pl.BlockSpec((b, BLK, BRANCH), main),
        out_shape=jax.ShapeDtypeStruct((b, rows, BRANCH), BF16 if reverse else F32),
        scratch_shapes=[pltpu.VMEM((b, BRANCH, 128), F32)],
        compiler_params=_params(1),
        name="gla_bwd" if reverse else "gla_fwd",
    )(*args)


def _lru_body(*refs, reverse, nblk, n_batch):
    if reverse:
        (u_ref, prev_ref, next_ref, cw_ref, cb_ref, wg_ref, bg_ref, lam_ref, hf_ref,
         o_ref, xe_ref, a_ref, s_ref, p_ref, hb_ref, h_ref) = refs
    else:
        (u_ref, prev_ref, next_ref, cw_ref, cb_ref, wg_ref, bg_ref, lam_ref,
         o_ref, xe_ref, a_ref, s_ref, p_ref, h_ref) = refs
    i = pl.program_id(0)
    blk = _bwd_block(i, nblk) if reverse else i

    @pl.when(i == 0)
    def _():
        h_ref[...] = jnp.zeros_like(h_ref)

    for bb in range(n_batch):
        ub = u_ref[bb]
        xc = _short_conv(xe_ref.at[bb], ub[:, 0:BRANCH], prev_ref[bb][:, 0:BRANCH], next_ref[bb][:, 0:BRANCH],
                         blk, nblk, cw_ref[0], cb_ref[0])
        z = _dot(xc.astype(BF16), wg_ref[0, 0]) + bg_ref[0, 0]
        rg = jax.nn.sigmoid(z[:, 0:BRANCH])
        ig = jax.nn.sigmoid(z[:, BRANCH:2 * BRANCH])
        log_a = -LRU_C * rg * jax.nn.softplus(-lam_ref[0, 0])
        a = jnp.exp(log_a)
        a_ref[bb] = a
        s_ref[bb] = jnp.sqrt(jnp.tanh(-log_a) * (1.0 + a * a)) * (ig * xc)

    half = BLK // 2
    dst = hb_ref if reverse else o_ref
    if reverse:
        first, second, lo2 = range(BLK - 1, half - 1, -1), range(half - 1, -1, -1), 0
    else:
        first, second, lo2 = range(0, half), range(half, BLK), half
    hs = [h_ref[bb] for bb in range(n_batch)]
    ps = [None] * n_batch
    qs = [None] * n_batch
    for t1, t2 in zip(first, second):
        for bb in range(n_batch):
            hs[bb] = a_ref[bb, t1:t1 + 1, :] * hs[bb] + s_ref[bb, t1:t1 + 1, :]
            dst[bb, t1:t1 + 1, :] = hs[bb]
            a2 = a_ref[bb, t2:t2 + 1, :]
            s2 = s_ref[bb, t2:t2 + 1, :]
            ps[bb] = a2 if ps[bb] is None else a2 * ps[bb]
            qs[bb] = s2 if qs[bb] is None else a2 * qs[bb] + s2
            dst[bb, t2:t2 + 1, :] = qs[bb]
            p_ref[bb, t2 - lo2:t2 - lo2 + 1, :] = ps[bb]
    for bb in range(n_batch):
        dst[bb, lo2:lo2 + half, :] = dst[bb, lo2:lo2 + half, :] + p_ref[bb] * hs[bb]
        h_ref[bb] = qs[bb] + ps[bb] * hs[bb]
        if reverse:
            o_ref[bb] = ((hf_ref[bb] + hb_ref[bb]) * _silu(u_ref[bb][:, BRANCH:2 * BRANCH])).astype(BF16)


def _lru_call(u, layer, cw, cb, wg, bg, lam, reverse, h_fwd=None):
    b, rows, _ = u.shape
    nblk = rows // BLK
    main, prev, nxt = _pass_specs(nblk, reverse, b, W_LRU)
    dr = 1 if reverse else 0
    in_specs = [pl.BlockSpec((b, BLK, W_LRU), main), prev, nxt,
                _layer_spec(cw.shape, layer, 1), _layer_spec(cb.shape, layer, 1),
                pl.BlockSpec((1, 1, BRANCH, 2 * BRANCH), lambda i: (layer, dr, 0, 0)),
                pl.BlockSpec((1, 1, 1, 2 * BRANCH), lambda i: (layer, dr, 0, 0)),
                pl.BlockSpec((1, 1, 1, BRANCH), lambda i: (layer, dr, 0, 0))]
    args = [u, u, u, cw, cb, wg, bg, lam]
    scratch = [pltpu.VMEM((b, BLK + 2 * HALO, BRANCH), F32), pltpu.VMEM((b, BLK, BRANCH), F32),
               pltpu.VMEM((b, BLK, BRANCH), F32), pltpu.VMEM((b, BLK // 2, BRANCH), F32)]
    if reverse:
        in_specs.append(pl.BlockSpec((b, BLK, BRANCH), main))
        args.append(h_fwd)
        scratch.append(pltpu.VMEM((b, BLK, BRANCH), F32))
    scratch.append(pltpu.VMEM((b, 1, BRANCH), F32))
    return pl.pallas_call(
        functools.partial(_lru_body, reverse=reverse, nblk=nblk, n_batch=b),
        grid=(nblk,),
        in_specs=in_specs,
        out_specs=pl.BlockSpec((b, BLK, BRANCH), main),
        out_shape=jax.ShapeDtypeStruct((b, rows, BRANCH), BF16 if reverse else F32),
        scratch_shapes=scratch,
        compiler_params=_params(1),
        name="lru_bwd" if reverse else "lru_fwd",
    )(*args)


def _ssd_blocks(ubs, xcs, dtb, aneg, sts, reverse):
    nb = len(ubs)
    off = SSD_HEADS if reverse else 0
    ch = SSD_CHUNK
    hpg = SSD_HEADS // SSD_GROUPS
    r = _iota((ch, ch), 0)
    c = _iota((ch, ch), 1)
    keep = (c >= r) if reverse else (c <= r)
    tri = keep.astype(BF16)
    expand = (_iota((128, BRANCH), 0) == off + (_iota((128, BRANCH), 1) >> 6)).astype(BF16)
    lane_n = _iota((ch, 128), 1) >> 6
    lane_p = _iota((ch, BRANCH), 1) >> 6
    bd = (_iota((128, BRANCH), 0) >> 6) == (_iota((128, BRANCH), 1) >> 7)
    edge = 0 if reverse else ch - 1

    n_chunks = BLK // ch
    order = list(reversed(range(n_chunks))) if reverse else list(range(n_chunks))
    cb_keys = [(ci, bb) for ci in order for bb in range(nb)]
    rows = lambda a, ci: a[ci * ch:(ci + 1) * ch]
    xs = {k: rows(xcs[k[1]][:, 0:256], k[0]) for k in cb_keys}
    bm = {k: rows(xcs[k[1]][:, 256:384], k[0]).astype(BF16) for k in cb_keys}
    cm = {k: rows(xcs[k[1]][:, 384:512], k[0]) for k in cb_keys}
    dt = {k: jax.nn.softplus(rows(ubs[k[1]][:, 768:896], k[0]) + dtb) for k in cb_keys}

    cum = {k: _dot_exact_left(tri, dt[k] * aneg) for k in cb_keys}
    cum_x = {k: _dot_exact_right(cum[k], expand, terms=2) for k in cb_keys}
    dt_x = {k: _dot_exact_right(dt[k], expand, terms=2) for k in cb_keys}
    cum_t = {k: cum[k].T for k in cb_keys}
    dt_t = {k: dt[k].T for k in cb_keys}
    cbm = {(k, g): _dot_nt(jnp.where(lane_n == g, cm[k], 0.0).astype(BF16), bm[k])
           for k in cb_keys for g in range(SSD_GROUPS)}

    def decay_mix(k, h):
        seg = cum[k][:, off + h:off + h + 1] - cum_t[k][off + h:off + h + 1, :]
        lm = jnp.exp(jnp.where(keep, seg, -1e30))
        return (cbm[(k, h // hpg)] * lm * dt_t[k][off + h:off + h + 1, :]).astype(BF16)

    y_in = {}
    for k in cb_keys:
        m_cat = jnp.concatenate([decay_mix(k, h) for h in range(SSD_HEADS)], axis=1)
        x_bd = jnp.concatenate([jnp.where(lane_p == h, xs[k], 0.0) for h in range(SSD_HEADS)], axis=0)
        y_in[k] = _dot(m_cat, x_bd.astype(BF16))
    tot_x = {k: cum_x[k][edge:edge + 1, :] for k in cb_keys}
    upd = {k: jnp.where(bd, _dot_tn(bm[k], (xs[k] * (jnp.exp(tot_x[k] - cum_x[k]) * dt_x[k])).astype(BF16)), 0.0)
           for k in cb_keys}
    sts = list(sts)
    outs = [[None] * n_chunks for _ in range(nb)]
    for k in cb_keys:
        ci, bb = k
        outs[bb][ci] = y_in[k] + _dot(cm[k].astype(BF16), sts[bb].astype(BF16)) * jnp.exp(cum_x[k])
        sts[bb] = sts[bb] * jnp.exp(tot_x[k]) + upd[k]
    return [jnp.concatenate(o, axis=0) for o in outs], sts


def _ssd_body(*refs, reverse, nblk, n_batch):
    if reverse:
        (u_ref, prev_ref, next_ref, cw_ref, cb_ref, dtb_ref, an_ref, yf_ref, dsk_ref, nw_ref,
         o_ref, xe_ref, st_ref) = refs
    else:
        u_ref, prev_ref, next_ref, cw_ref, cb_ref, dtb_ref, an_ref, o_ref, xe_ref, st_ref = refs
    i = pl.program_id(0)
    blk = _bwd_block(i, nblk) if reverse else i

    @pl.when(i == 0)
    def _():
        st_ref[...] = jnp.zeros_like(st_ref)

    ubs = [u_ref[bb] for bb in range(n_batch)]
    xcs = [_silu(_short_conv(xe_ref.at[bb], ubs[bb][:, 0:512], prev_ref[bb][:, 0:512], next_ref[bb][:, 0:512],
                             blk, nblk, cw_ref[0], cb_ref[0])) for bb in range(n_batch)]
    ys, sts = _ssd_blocks(ubs, xcs, dtb_ref[0], an_ref[0], [st_ref[bb] for bb in range(n_batch)], reverse)
    for bb in range(n_batch):
        st_ref[bb] = sts[bb]
        if reverse:
            y = ys[bb] + yf_ref[bb] + dsk_ref[0] * xcs[bb][:, 0:256]
            yz = y * _silu(ubs[bb][:, 512:768])
            ms = jnp.mean(yz * yz, axis=-1, keepdims=True)
            o_ref[bb] = (yz * lax.rsqrt(ms + EPS) * nw_ref[0]).astype(BF16)
        else:
            o_ref[bb] = ys[bb]


def _ssd_call(u, layer, cw, cb, dtb, aneg, reverse, y_fwd=None, dskip=None, nw=None):
    b, rows, _ = u.shape
    nblk = rows // BLK
    main, prev, nxt = _pass_specs(nblk, reverse, b, W_SSD)
    in_specs = [pl.BlockSpec((b, BLK, W_SSD), main), prev, nxt,
                _layer_spec(cw.shape, layer, 1), _layer_spec(cb.shape, layer, 1),
                _layer_spec(dtb.shape, layer, 1), _layer_spec(aneg.shape, layer, 1)]
    args = [u, u, u, cw, cb, dtb, aneg]
    if reverse:
        in_specs += [pl.BlockSpec((b, BLK, BRANCH), main),
                     _layer_spec(dskip.shape, layer, 1), _layer_spec(nw.shape, layer, 1)]
        args += [y_fwd, dskip, nw]
    return pl.pallas_call(
        functools.partial(_ssd_body, reverse=reverse, nblk=nblk, n_batch=b),
        grid=(nblk,),
        in_specs=in_specs,
        out_specs=pl.BlockSpec((b, BLK, BRANCH), main),
        out_shape=jax.ShapeDtypeStruct((b, rows, BRANCH), BF16 if reverse else F32),
        scratch_shapes=[pltpu.VMEM((b, BLK + 2 * HALO, 512), F32), pltpu.VMEM((b, 128, BRANCH), F32)],
        compiler_params=_params(1),
        name="ssd_bwd" if reverse else "ssd_fwd",
    )(*args)


def _key_step(nk):
    return max(t for t in range(BLK, min(ATT_TK, nk) + 1, BLK) if nk % t == 0)


def _attn_body(*refs, nk, lam_init, n_qb):
    qt_refs, (k_ref, vt_ref), g_refs = refs[0:n_qb], refs[n_qb:n_qb + 2], refs[n_qb + 2:2 * n_qb + 2]
    lam_ref, sw_ref, o_ref, s0_ref, s1_ref = refs[2 * n_qb + 2:]
    tk = _key_step(nk)
    n_steps = nk // tk
    sub = tk // BLK
    fold = 32
    lv = lam_ref[0]
    lam = (jnp.exp(jnp.sum(lv[0:1] * lv[1:2], axis=-1, keepdims=True))
           - jnp.exp(jnp.sum(lv[2:3] * lv[3:4], axis=-1, keepdims=True)) + lam_init)
    row = _iota((128, BLK), 0)
    pairs = [(qb, h, half) for qb in range(n_qb) for h in range(DIFF_HEADS) for half in range(2)]
    s_bufs = (s0_ref, s1_ref)

    def masked_qt(p):
        qb, h, half = pairs[p]
        qt = qt_refs[qb][0, 0, (h // 2) * 128:(h // 2 + 1) * 128, :]
        lo = (h % 2) * 64 + half * DIFF_D
        return jnp.where(jnp.logical_and(row >= lo, row < lo + DIFF_D), qt, jnp.zeros_like(qt))

    def key_start(j):
        return j * tk if isinstance(j, int) else pl.multiple_of(j * tk, BLK)

    def stage_a(p, qm, j, jj, mx):
        tile = pairs[p][1] // 2
        rows = pl.ds(key_start(j) + jj * BLK, BLK)
        s = _dot(k_ref[0, rows, tile * 128:(tile + 1) * 128], qm)
        s_bufs[p % 2][rows, :] = s
        return jnp.maximum(mx, jnp.max(s.reshape(BLK // fold, fold, BLK), axis=0))

    def stage_bc(p, m, j, jj, ls, acc):
        h = pairs[p][1]
        e = jnp.exp2(s_bufs[p % 2][pl.ds(key_start(j) + jj * BLK, BLK), :] - m)
        ls = ls + jnp.sum(e.reshape(BLK // fold, fold, BLK), axis=0)
        acc = acc + _dot(vt_ref[0, j * sub + jj, h * DIFF_DV:(h + 1) * DIFF_DV, :], e.astype(BF16))
        return ls, acc

    results = []
    m_prev = None
    for p in range(len(pairs) + 1):
        do_a = p < len(pairs)
        do_bc = p >= 1
        qm = masked_qt(p) if do_a else None

        def step(j, carry, p=p, qm=qm, m_prev=m_prev, do_a=do_a, do_bc=do_bc):
            mx, ls, acc = carry
            for jj in range(sub):
                if do_bc:
                    ls, acc = stage_bc(p - 1, m_prev, j, jj, ls, acc)
                if do_a:
                    mx = stage_a(p, qm, j, jj, mx)
            return mx, ls, acc

        carry = (jnp.full((fold, BLK), -jnp.inf, F32), jnp.zeros((fold, BLK), F32),
                 jnp.zeros((DIFF_DV, BLK), F32))
        if n_steps > 1:
            mx, ls, acc = lax.fori_loop(0, n_steps, step, carry)
        else:
            mx, ls, acc = step(0, carry)
        if do_bc:
            results.append(acc / jnp.sum(ls, axis=0, keepdims=True))
        if do_a:
            m_prev = jnp.max(mx, axis=0, keepdims=True)

    grp = ((_iota((BRANCH, BRANCH), 0) >> 6) == (_iota((BRANCH, BRANCH), 1) >> 6)).astype(BF16)
    per_qb = 2 * DIFF_HEADS
    for qb in range(n_qb):
        res = results[qb * per_qb:(qb + 1) * per_qb]
        heads = [res[2 * h] - lam * res[2 * h + 1] for h in range(DIFF_HEADS)]
        o = jnp.concatenate(heads, axis=0).T
        ms = _dot_exact_right(o * o, grp, terms=2) * (1.0 / DIFF_DV)
        y = o * lax.rsqrt(ms + EPS) * sw_ref[0] * (1.0 - lam_init)
        o_ref[0, qb * BLK:(qb + 1) * BLK, :] = (y * _silu(g_refs[qb][0])).astype(BF16)


def _attn_call(qt, k, vt, gate, layer, lam_vecs, sw, lam_init, ctx_only, skew_rows=0):
    b, rows, _ = k.shape
    nblk = rows // BLK
    nk = BLK if ctx_only else rows
    nq = 1 if ctx_only else nblk - 1
    qoff = 0 if ctx_only else 1
    n_qb = max(t for t in (1, 2, 4) if nq % t == 0)
    q_specs = [pl.BlockSpec((1, 1, BRANCH, BLK), lambda bb, i, t=t: (bb, n_qb * i + t + qoff, 0, 0))
               for t in range(n_qb)]
    g_specs = [pl.BlockSpec((1, BLK, BRANCH), lambda bb, i, t=t: (bb, n_qb * i + t + qoff, 0))
               for t in range(n_qb)]
    return pl.pallas_call(
        functools.partial(_attn_body, nk=nk, lam_init=lam_init, n_qb=n_qb),
        grid=(b, nq // n_qb),
        in_specs=q_specs + [pl.BlockSpec((1, nk, BRANCH), lambda bb, i: (bb, 0, 0)),
                            pl.BlockSpec((1, nk // BLK, BRANCH, BLK), lambda bb, i: (bb, 0, 0, 0))]
        + g_specs + [_layer_spec(lam_vecs.shape, layer, 2), _layer_spec(sw.shape, layer, 2)],
        out_specs=pl.BlockSpec((1, n_qb * BLK, BRANCH), lambda bb, i: (bb, i, 0)),
        out_shape=jax.ShapeDtypeStruct((b, nq * BLK, BRANCH), BF16),
        scratch_shapes=[pltpu.VMEM((nk + skew_rows, BLK), F32), pltpu.VMEM((nk, BLK), F32)],
        compiler_params=_params(2),
        name="diff_attn_ctx" if ctx_only else "diff_attn",
    )(*([qt] * n_qb), k, vt, *([gate] * n_qb), lam_vecs, sw)


W_IN_ROWS = 128


def _w_in_layout_body(w_ref, o_ref):
    w = w_ref[0]

    def group(lo, pieces, width):
        parts = [w[:, lo + a:lo + b] for a, b in pieces]
        used = sum(b - a for a, b in pieces)
        if used < width:
            parts.append(jnp.zeros((W_IN_ROWS, width - used), F32))
        return parts

    parts = (group(COL_GLA, [(0, 512), (544, 800), (512, 544)], W_GLA)
             + group(COL_LRU, [(0, 512)], W_LRU)
             + group(COL_DIFF, [(0, 1024)], W_DIFF)
             + group(COL_SSD, [(0, 512), (520, 776), (512, 520)], W_SSD))
    o_ref[0] = jnp.concatenate(parts, axis=-1).astype(BF16)


def _w_in_layout_call(w_in):
    depth, d, n = w_in.shape
    return pl.pallas_call(
        _w_in_layout_body,
        grid=(depth, d // W_IN_ROWS),
        in_specs=[pl.BlockSpec((1, W_IN_ROWS, n), lambda l, r: (l, r, 0))],
        out_specs=pl.BlockSpec((1, W_IN_ROWS, W_ALL), lambda l, r: (l, r, 0)),
        out_shape=jax.ShapeDtypeStruct((depth, d, W_ALL), BF16),
        compiler_params=_params(2),
        name="w_in_layout",
    )(w_in)


def _block_diag(w):
    nb, n = w.shape[-3], w.shape[-1]
    eye = jnp.eye(nb, dtype=w.dtype)
    dense = eye[:, None, :, None] * w[..., :, :, None, :]
    return dense.reshape(w.shape[:-3] + (nb * n, nb * n))


def _rope_tables(t_lat):
    n_freq = DIFF_D // 4
    inv = ROPE_BASE ** (-jnp.arange(n_freq, dtype=F32) / n_freq)
    tpos = jnp.arange(t_lat)
    ang_r = (tpos // GRID_W).astype(F32)[:, None] * inv
    ang_c = (tpos % GRID_W).astype(F32)[:, None] * inv
    ang = jnp.concatenate([ang_r, ang_r, ang_c, ang_c], axis=-1)
    reps = 128 // DIFF_D
    sign = jnp.where((jnp.arange(DIFF_D) % 16) < 8, -1.0, 1.0).astype(F32)
    return jnp.tile(jnp.cos(ang), (1, reps)), jnp.tile(jnp.sin(ang) * sign, (1, reps))


def kernel(x, c, ctx, c_ctx, w_mod, b_mod, norm_w, w_in, w_out, gla_w2, gla_b2, gla_norm_w, lru_conv_w, lru_conv_b, lru_wa, lru_ba, lru_wx, lru_bx, lru_lam, diff_lam, diff_subln_w, ssd_conv_w, ssd_conv_b, ssd_dt_bias, ssd_a_log, ssd_d, ssd_norm_w, final_norm_w):
    bsz, t_lat, d = x.shape
    depth = w_mod.shape[0]
    assert ctx.shape[1] == BLK and t_lat % BLK == 0 and d == D_MODEL and bsz + 1 <= 8
    nblk = 1 + t_lat // BLK

    cvec = jnp.concatenate([c, c_ctx[None], jnp.zeros((8 - bsz - 1, d), F32)], axis=0)
    mod = _mod_call(cvec, w_mod, b_mod)
    cos, sin = _rope_tables(t_lat)

    nw3 = norm_w[:, None, :]
    w_in_p = _w_in_layout_call(w_in)
    w_out_b = w_out.astype(BF16)
    pad_r = lambda w, lo: jnp.pad(w, ((0, 0), (lo, 128 - GLA_RANK - lo), (0, 0)))
    w2 = jnp.stack([pad_r(gla_w2[:, 0], 0), pad_r(gla_w2[:, 1], GLA_RANK)], axis=1).astype(BF16)
    b2 = gla_b2[:, :, None, :]
    gnw = jnp.tile(gla_norm_w, (1, GLA_HEADS))[:, None, :]
    wg = jnp.concatenate([_block_diag(lru_wa), _block_diag(lru_wx)], axis=-1).astype(BF16)
    bg = jnp.concatenate([lru_ba, lru_bx], axis=-1)[:, :, None, :]
    llam = lru_lam[:, :, None, :]
    lcb = lru_conv_b[:, None, :]
    sw = jnp.tile(diff_subln_w, (1, DIFF_HEADS))[:, None, :]
    pad_l = lambda v: jnp.pad(v.reshape(depth, -1), ((0, 0), (0, 128 - 2 * SSD_HEADS)))[:, None, :]
    dtb = pad_l(ssd_dt_bias)
    aneg = pad_l(-jnp.exp(ssd_a_log))
    dsk = jnp.repeat(ssd_d, SSD_P, axis=-1)[:, None, :]
    scb = ssd_conv_b[:, None, :]
    snw = ssd_norm_w[:, None, :]

    lat, lat_off, cx = x, 0, ctx
    proj = _inproj_call(lat, lat_off, cx, 0, nblk, nw3, mod, w_in_p, cos, sin)
    for l in range(depth):
        last = l == depth - 1
        lam_init = 0.8 - 0.6 * math.exp(-0.3 * l)
        ua, ub, ud, qt, kp, vt, gd = proj

        of = _gla_call(ua, l, w2, b2, False)
        ya = _gla_call(ua, l, w2, b2, True, of, gnw)

        hf = _lru_call(ub, l, lru_conv_w, lcb, wg, bg, llam, False)
        yb = _lru_call(ub, l, lru_conv_w, lcb, wg, bg, llam, True, hf)

        yc_lat = _attn_call(qt, kp, vt, gd, l, diff_lam, sw, lam_init, False, skew_rows=(8, 16)[l % 2])
        yc_ctx = yc_lat if last else _attn_call(qt, kp, vt, gd, l, diff_lam, sw, lam_init, True)

        yf = _ssd_call(ud, l, ssd_conv_w, scb, dtb, aneg, False)
        yd = _ssd_call(ud, l, ssd_conv_w, scb, dtb, aneg, True, yf, dsk, snw)

        if last:
            return _outproj_final_call(ya, yb, yc_lat, yd, lat, lat_off, cx, l, nblk, mod, w_out_b,
                                       final_norm_w[None])
        h, *proj = _outproj_next_call(ya, yb, yc_lat, yc_ctx, yd, lat, lat_off, cx, l, nblk, mod, w_out_b,
                                      nw3, w_in_p, cos, sin)
        lat, lat_off, cx = h, 1, h
```

```python
import functools
import math

import jax
import jax.numpy as jnp
from jax import lax
from jax.experimental import pallas as pl
from jax.experimental.pallas import tpu as pltpu

F32 = jnp.float32
BF16 = jnp.bfloat16

D_MODEL = 1024
BRANCH = D_MODEL // 4
GRID_W = 64
EPS = 1e-6
CONV_W = 4
BLK = 256
HALO = 8
GLA_HEADS, GLA_DK, GLA_DV = 4, 32, 64
GLA_RANK = 16
GLA_TAU = 16.0
GLA_CHUNK = 64
LRU_C = 8.0
DIFF_HEADS, DIFF_D, DIFF_DV = 4, 32, 64
ROPE_BASE = 10000.0
SSD_HEADS, SSD_P, SSD_N, SSD_GROUPS = 4, 64, 64, 2
SSD_CHUNK = 128
ATT_TK = 2816
W_GLA, W_LRU, W_DIFF, W_SSD = 896, 512, 1024, 896
COL_GLA, COL_LRU, COL_DIFF, COL_SSD = 0, 800, 1312, 2336
W_ALL = W_GLA + W_LRU + W_DIFF + W_SSD
VMEM_LIMIT = 56 * 1024 * 1024


def _dot(a, b):
    return jnp.dot(a, b, preferred_element_type=F32)


def _dot_nt(a, b):
    return lax.dot_general(a, b, (((1,), (1,)), ((), ())), preferred_element_type=F32)


def _dot_tn(a, b):
    return lax.dot_general(a, b, (((0,), (0,)), ((), ())), preferred_element_type=F32)


def _split(x, terms):
    out = []
    r = x
    for _ in range(terms):
        p = r.astype(BF16)
        out.append(p)
        r = r - p.astype(F32)
    return out


def _dot_exact_left(m, x, terms=3):
    return sum(_dot(m, p) for p in _split(x, terms))


def _dot_exact_right(x, m, terms=3):
    return sum(_dot(p, m) for p in _split(x, terms))


def _iota(shape, dim):
    return lax.broadcasted_iota(jnp.int32, shape, dim)


def _silu(x):
    return x * jax.nn.sigmoid(x)


def _params(n_axes):
    return pltpu.CompilerParams(dimension_semantics=("arbitrary",) * n_axes, vmem_limit_bytes=VMEM_LIMIT)


def _layer_spec(shape, layer, n_axes):
    block = (1,) + tuple(shape[1:])
    idx = (layer,) + (0,) * (len(shape) - 1)
    if n_axes == 1:
        return pl.BlockSpec(block, lambda i: idx)
    return pl.BlockSpec(block, lambda bb, i: idx)


def _bwd_block(i, nblk):
    return jnp.where(i == 0, 0, nblk - i)


def _mod_body(c_ref, w_ref, b_ref, o_ref):
    s = _silu(c_ref[...])
    o_ref[0] = _dot(s.astype(BF16), w_ref[0].astype(BF16)) + b_ref[0]


def _mod_call(cvec, w_mod, b_mod):
    depth, d, n = w_mod.shape
    tn = 1024
    return pl.pallas_call(
        _mod_body,
        grid=(depth, n // tn),
        in_specs=[pl.BlockSpec((8, d), lambda l, j: (0, 0)),
                  pl.BlockSpec((1, d, tn), lambda l, j: (l, 0, j)),
                  pl.BlockSpec((1, 1, tn), lambda l, j: (l, 0, j))],
        out_specs=pl.BlockSpec((1, 8, tn), lambda l, j: (l, 0, j)),
        out_shape=jax.ShapeDtypeStruct((depth, 8, n), F32),
        compiler_params=_params(2),
        name="adaln_mod",
    )(cvec, w_mod, b_mod.reshape(depth, 1, n))


def _mod_row(mod_ref, is_ctx, bb, n_batch, part):
    r = jnp.where(is_ctx, n_batch, bb)
    return mod_ref[0, pl.ds(r, 1), part * D_MODEL:(part + 1) * D_MODEL]


def _project_in(x, is_ctx, bb, n_batch, nw_ref, mod_ref, w_ref, cos_ref, sin_ref, outs):
    oa_ref, ob_ref, od_ref, qt_ref, k_ref, vt_ref, g_ref = outs
    ms = jnp.mean(x * x, axis=-1, keepdims=True)
    xn = x * lax.rsqrt(ms + EPS) * nw_ref[0]
    xm = xn * (1.0 + _mod_row(mod_ref, is_ctx, bb, n_batch, 1)) + _mod_row(mod_ref, is_ctx, bb, n_batch, 0)
    u = _dot(xm.astype(BF16), w_ref[0])
    oa_ref[0] = u[:, 0:W_GLA]
    ob_ref[0] = u[:, W_GLA:W_GLA + W_LRU]
    od_ref[0] = u[:, W_GLA + W_LRU + W_DIFF:W_ALL]

    ud = u[:, W_GLA + W_LRU:W_GLA + W_LRU + W_DIFF]
    lane_tiles = BRANCH // 128
    cs = jnp.concatenate([jnp.where(is_ctx, 1.0, cos_ref[...])] * lane_tiles, axis=1)
    sn = jnp.concatenate([jnp.where(is_ctx, 0.0, sin_ref[...])] * lane_tiles, axis=1)
    first = (_iota((BLK, BRANCH), 1) & 15) < 8

    def rope(t):
        rot = jnp.where(first, pltpu.roll(t, BRANCH - 8, 1), pltpu.roll(t, 8, 1))
        return t * cs + rot * sn

    qt_ref[0, 0] = (rope(ud[:, 0:256]) * (DIFF_D ** -0.5 * math.log2(math.e))).T.astype(BF16)
    k_ref[0] = rope(ud[:, 256:512]).astype(BF16)
    vt_ref[0, 0] = ud[:, 512:768].T.astype(BF16)
    g_ref[0] = ud[:, 768:1024]


def _inproj_body(lat_ref, ctx_ref, nw_ref, mod_ref, w_ref, cos_ref, sin_ref, *outs, n_batch):
    is_ctx = pl.program_id(1) == 0
    x = jnp.where(is_ctx, ctx_ref[0], lat_ref[0])
    _project_in(x, is_ctx, pl.program_id(0), n_batch, nw_ref, mod_ref, w_ref, cos_ref, sin_ref, outs)


def _inproj_specs(b, nblk, layer, nw, mod, w):
    rows = nblk * BLK
    row_out = lambda n: pl.BlockSpec((1, BLK, n), lambda bb, i: (bb, i, 0))
    t_out = pl.BlockSpec((1, 1, BRANCH, BLK), lambda bb, i: (bb, i, 0, 0))
    tab = pl.BlockSpec((BLK, 128), lambda bb, i: (jnp.maximum(i - 1, 0), 0))
    in_specs = [_layer_spec(nw.shape, layer, 2), _layer_spec(mod.shape, layer, 2), _layer_spec(w.shape, layer, 2),
                tab, tab]
    out_specs = [row_out(W_GLA), row_out(W_LRU), row_out(W_SSD), t_out, row_out(BRANCH), t_out, row_out(BRANCH)]
    out_shape = [jax.ShapeDtypeStruct((b, rows, W_GLA), F32),
                 jax.ShapeDtypeStruct((b, rows, W_LRU), F32),
                 jax.ShapeDtypeStruct((b, rows, W_SSD), F32),
                 jax.ShapeDtypeStruct((b, nblk, BRANCH, BLK), BF16),
                 jax.ShapeDtypeStruct((b, rows, BRANCH), BF16),
                 jax.ShapeDtypeStruct((b, nblk, BRANCH, BLK), BF16),
                 jax.ShapeDtypeStruct((b, rows, BRANCH), F32)]
    return in_specs, out_specs, out_shape


def _inproj_call(lat, lat_off, ctx, layer, nblk, nw, mod, w, cos, sin):
    b = lat.shape[0]
    d = D_MODEL
    p_specs, out_specs, out_shape = _inproj_specs(b, nblk, layer, nw, mod, w)
    return pl.pallas_call(
        functools.partial(_inproj_body, n_batch=b),
        grid=(b, nblk),
        in_specs=[pl.BlockSpec((1, BLK, d), lambda bb, i: (bb, jnp.maximum(i - 1, 0) + lat_off, 0)),
                  pl.BlockSpec((1, BLK, d), lambda bb, i: (bb, 0, 0))] + p_specs,
        out_specs=out_specs,
        out_shape=out_shape,
        compiler_params=_params(2),
        name="norm_inproj",
    )(lat, ctx, nw, mod, w, cos, sin)


def _residual_update(ya_ref, yb_ref, ycl_ref, ycc_ref, yd_ref, lat_ref, ctx_ref, mod_ref, w_ref, is_ctx, bb, n_batch):
    if is_ctx is False:
        yc, res = ycl_ref[0], lat_ref[0]
    else:
        yc = jnp.where(is_ctx, ycc_ref[0], ycl_ref[0])
        res = jnp.where(is_ctx, ctx_ref[0], lat_ref[0])
    y = jnp.concatenate([ya_ref[0], yb_ref[0], yc, yd_ref[0]], axis=-1)
    return res + _mod_row(mod_ref, is_ctx, bb, n_batch, 2) * _dot(y, w_ref[0])


def _outproj_final_body(ya_ref, yb_ref, ycl_ref, ycc_ref, yd_ref, lat_ref, ctx_ref, mod_ref, w_ref, fw_ref, o_ref,
                        *, n_batch):
    hn = _residual_update(ya_ref, yb_ref, ycl_ref, ycc_ref, yd_ref, lat_ref, ctx_ref, mod_ref, w_ref,
                          False, pl.program_id(0), n_batch)
    ms = jnp.mean(hn * hn, axis=-1, keepdims=True)
    o_ref[0] = hn * lax.rsqrt(ms + EPS) * fw_ref[...]


def _outproj_next_body(ya_ref, yb_ref, ycl_ref, ycc_ref, yd_ref, lat_ref, ctx_ref, modg_ref, wo_ref,
                       nw_ref, mod_ref, wi_ref, cos_ref, sin_ref, h_ref, *outs, n_batch):
    bb = pl.program_id(0)
    is_ctx = pl.program_id(1) == 0
    hn = _residual_update(ya_ref, yb_ref, ycl_ref, ycc_ref, yd_ref, lat_ref, ctx_ref, modg_ref, wo_ref,
                          is_ctx, bb, n_batch)
    h_ref[0] = hn
    _project_in(hn, is_ctx, bb, n_batch, nw_ref, mod_ref, wi_ref, cos_ref, sin_ref, outs)


def _outproj_in_specs(lat_off, layer, mod, w, final):
    d = D_MODEL
    off = 1 if final else 0
    y_spec = pl.BlockSpec((1, BLK, BRANCH), lambda bb, i: (bb, i + off, 0))
    lat_blk = (lambda i: i) if final else (lambda i: jnp.maximum(i - 1, 0))
    return [y_spec, y_spec,
            pl.BlockSpec((1, BLK, BRANCH), lambda bb, i: (bb, lat_blk(i), 0)),
            pl.BlockSpec((1, BLK, BRANCH), lambda bb, i: (bb, 0, 0)),
            y_spec,
            pl.BlockSpec((1, BLK, d), lambda bb, i: (bb, lat_blk(i) + lat_off, 0)),
            pl.BlockSpec((1, BLK, d), lambda bb, i: (bb, 0, 0)),
            _layer_spec(mod.shape, layer, 2), _layer_spec(w.shape, layer, 2)]


def _outproj_final_call(ya, yb, yc_lat, yd, lat, lat_off, ctx, layer, nblk, mod, w, fw):
    b = lat.shape[0]
    d = D_MODEL
    nout = nblk - 1
    return pl.pallas_call(
        functools.partial(_outproj_final_body, n_batch=b),
        grid=(b, nout),
        in_specs=_outproj_in_specs(lat_off, layer, mod, w, True) + [pl.BlockSpec((1, d), lambda bb, i: (0, 0))],
        out_specs=pl.BlockSpec((1, BLK, d), lambda bb, i: (bb, i, 0)),
        out_shape=jax.ShapeDtypeStruct((b, nout * BLK, d), F32),
        compiler_params=_params(2),
        name="outproj_final",
    )(ya, yb, yc_lat, yc_lat, yd, lat, ctx, mod, w, fw)


def _outproj_next_call(ya, yb, yc_lat, yc_ctx, yd, lat, lat_off, ctx, layer, nblk, mod, w_out, nw, w_in, cos, sin):
    b = lat.shape[0]
    d = D_MODEL
    p_specs, p_out_specs, p_out_shape = _inproj_specs(b, nblk, layer + 1, nw, mod, w_in)
    return pl.pallas_call(
        functools.partial(_outproj_next_body, n_batch=b),
        grid=(b, nblk),
        in_specs=_outproj_in_specs(lat_off, layer, mod, w_out, False) + p_specs,
        out_specs=[pl.BlockSpec((1, BLK, d), lambda bb, i: (bb, i, 0))] + p_out_specs,
        out_shape=[jax.ShapeDtypeStruct((b, nblk * BLK, d), F32)] + p_out_shape,
        compiler_params=_params(2),
        name="outproj_inproj",
    )(ya, yb, yc_lat, yc_ctx, yd, lat, ctx, mod, w_out, nw, mod, w_in, cos, sin)


def _pass_specs(nblk, reverse, n_batch, width):
    def blk_of(i):
        return _bwd_block(i, nblk) if reverse else i
    per_blk = BLK // HALO
    main = lambda i: (0, blk_of(i), 0)
    prev = pl.BlockSpec((n_batch, HALO, width), lambda i: (0, jnp.maximum(blk_of(i) * per_blk - 1, 0), 0))
    nxt = pl.BlockSpec((n_batch, HALO, width),
                       lambda i: (0, jnp.minimum((blk_of(i) + 1) * per_blk, nblk * per_blk - 1), 0))
    return main, prev, nxt


def _short_conv(xe_ref, x, prev, nxt, blk, nblk, cw, cb):
    prev_ok = blk >= 2
    next_ok = jnp.logical_and(blk >= 1, blk <= nblk - 2)
    xe_ref[0:HALO, :] = jnp.where(prev_ok, prev, 0.0)
    xe_ref[HALO:HALO + BLK, :] = x
    xe_ref[HALO + BLK:2 * HALO + BLK, :] = jnp.where(next_ok, nxt, 0.0)
    left = CONV_W // 2
    y = cb + xe_ref[HALO - left:HALO - left + BLK, :] * cw[0:1, :]
    for j in range(1, CONV_W):
        y = y + xe_ref[HALO - left + j:HALO - left + j + BLK, :] * cw[j:j + 1, :]
    return y


def _gla_blocks(ubs, w2, b2, sts, reverse):
    nb = len(ubs)
    r = _iota((BLK, BLK), 0)
    c = _iota((BLK, BLK), 1)
    same = (r >> 6) == (c >> 6)
    tri = jnp.logical_and(same, (c >= r) if reverse else (c <= r))
    sel = jnp.concatenate([tri, same], axis=0).astype(BF16)
    lane_k = _iota((GLA_CHUNK, 128), 1) >> 5
    rr = _iota((GLA_HEADS * GLA_CHUNK, GLA_CHUNK), 0) & (GLA_CHUNK - 1)
    cc = _iota((GLA_HEADS * GLA_CHUNK, GLA_CHUNK), 1)
    amask = (cc >= rr) if reverse else (cc <= rr)
    lane_v = _iota((GLA_CHUNK, BRANCH), 1) >> 6
    bd = (_iota((BRANCH, 128), 0) >> 6) == (_iota((BRANCH, 128), 1) >> 5)

    las = [jax.nn.log_sigmoid(_dot(ub[:, 768:896].astype(BF16), w2) + b2) * (1.0 / GLA_TAU) for ub in ubs]
    ggs = [_dot_exact_left(sel, la, terms=2) for la in las]
    gs = [gg[0:BLK] for gg in ggs]
    gls = [gg[BLK:2 * BLK] for gg in ggs]
    qgs = [ub[:, 0:128] * (GLA_DK ** -0.5) * jnp.exp(g) for ub, g in zip(ubs, gs)]
    kgs = [(ub[:, 128:256] * jnp.exp(-g)).astype(BF16) for ub, g in zip(ubs, gs)]
    kds = [(ub[:, 128:256] * jnp.exp(gl - g)).astype(BF16) for ub, g, gl in zip(ubs, gs, gls)]
    egls = [jnp.exp(gl) for gl in gls]
    vs = [ub[:, 256:512].astype(BF16) for ub in ubs]

    n_chunks = BLK // GLA_CHUNK
    order = list(reversed(range(n_chunks))) if reverse else list(range(n_chunks))
    rows = lambda a, ci: a[ci * GLA_CHUNK:(ci + 1) * GLA_CHUNK]
    cb = [(ci, bb) for ci in order for bb in range(nb)]

    def stacked_q(ci, bb):
        qg_c = rows(qgs[bb], ci)
        return jnp.concatenate([jnp.where(lane_k == h, qg_c, 0.0) for h in range(GLA_HEADS)], axis=0).astype(BF16)

    att = {k: jnp.where(amask, _dot_nt(stacked_q(*k), rows(kgs[k[1]], k[0])), 0.0).astype(BF16) for k in cb}
    res = {k: _dot(att[k], rows(vs[k[1]], k[0])) for k in cb}
    upd = {k: jnp.where(bd, _dot_tn(rows(vs[k[1]], k[0]), rows(kds[k[1]], k[0])), 0.0) for k in cb}
    sts = list(sts)
    outs = [[None] * n_chunks for _ in range(nb)]
    for ci, bb in cb:
        o_c = _dot_nt(rows(qgs[bb], ci).astype(BF16), sts[bb].astype(BF16))
        for h in range(GLA_HEADS):
            o_c = o_c + jnp.where(lane_v == h, res[(ci, bb)][h * GLA_CHUNK:(h + 1) * GLA_CHUNK], 0.0)
        sts[bb] = sts[bb] * egls[bb][ci * GLA_CHUNK:ci * GLA_CHUNK + 1, :] + upd[(ci, bb)]
        outs[bb][ci] = o_c
    return [jnp.concatenate(o, axis=0) for o in outs], sts


def _gla_body(*refs, reverse, n_batch):
    if reverse:
        u_ref, w2_ref, b2_ref, of_ref, nw_ref, o_ref, st_ref = refs
    else:
        u_ref, w2_ref, b2_ref, o_ref, st_ref = refs

    @pl.when(pl.program_id(0) == 0)
    def _():
        st_ref[...] = jnp.zeros_like(st_ref)

    ubs = [u_ref[bb] for bb in range(n_batch)]
    os_, sts = _gla_blocks(ubs, w2_ref[0, 0], b2_ref[0, 0], [st_ref[bb] for bb in range(n_batch)], reverse)
    if reverse:
        grp = ((_iota((BRANCH, BRANCH), 0) >> 6) == (_iota((BRANCH, BRANCH), 1) >> 6)).astype(BF16)
        os_ = [o + of_ref[bb] for bb, o in enumerate(os_)]
        mss = [_dot_exact_right(o * o, grp, terms=2) * (1.0 / GLA_DV) for o in os_]
        os_ = [(o * lax.rsqrt(ms + EPS) * nw_ref[0] * _silu(ub[:, 512:768])).astype(BF16)
               for o, ms, ub in zip(os_, mss, ubs)]
    for bb in range(n_batch):
        st_ref[bb] = sts[bb]
        o_ref[bb] = os_[bb]


def _gla_call(u, layer, w2, b2, reverse, o_fwd=None, nw=None):
    b, rows, _ = u.shape
    nblk = rows // BLK
    main, _, _ = _pass_specs(nblk, reverse, b, W_GLA)
    dr = 1 if reverse else 0
    in_specs = [pl.BlockSpec((b, BLK, W_GLA), main),
                pl.BlockSpec((1, 1, 128, 128), lambda i: (layer, dr, 0, 0)),
                pl.BlockSpec((1, 1, 1, 128), lambda i: (layer, dr, 0, 0))]
    args = [u, w2, b2]
    if reverse:
        in_specs += [pl.BlockSpec((b, BLK, BRANCH), main), _layer_spec(nw.shape, layer, 1)]
        args += [o_fwd, nw]
    return pl.pallas_call(
        functools.partial(_gla_body, reverse=reverse, n_batch=b),
        grid=(nblk,),
        in_specs=in_specs,
        out_specs=pl.BlockSpec((b, BLK, BRANCH), main),
        out_shape=jax.ShapeDtypeStruct((b, rows, BRANCH), BF16 if reverse else F32),
        scratch_shapes=[pltpu.VMEM((b, BRANCH, 128), F32)],
        compiler_params=_params(1),
        name="gla_bwd" if reverse else "gla_fwd",
    )(*args)


def _lru_body(*refs, reverse, nblk, n_batch):
    if reverse:
        (u_ref, prev_ref, next_ref, cw_ref, cb_ref, wg_ref, bg_ref, lam_ref, hf_ref,
         o_ref, xe_ref, a_ref, s_ref, p_ref, hb_ref, h_ref) = refs
    else:
        (u_ref, prev_ref, next_ref, cw_ref, cb_ref, wg_ref, bg_ref, lam_ref,
         o_ref, xe_ref, a_ref, s_ref, p_ref, h_ref) = refs
    i = pl.program_id(0)
    blk = _bwd_block(i, nblk) if reverse else i

    @pl.when(i == 0)
    def _():
        h_ref[...] = jnp.zeros_like(h_ref)

    for bb in range(n_batch):
        ub = u_ref[bb]
        xc = _short_conv(xe_ref.at[bb], ub[:, 0:BRANCH], prev_ref[bb][:, 0:BRANCH], next_ref[bb][:, 0:BRANCH],
                         blk, nblk, cw_ref[0], cb_ref[0])
        z = _dot(xc.astype(BF16), wg_ref[0, 0]) + bg_ref[0, 0]
        rg = jax.nn.sigmoid(z[:, 0:BRANCH])
        ig = jax.nn.sigmoid(z[:, BRANCH:2 * BRANCH])
        log_a = -LRU_C * rg * jax.nn.softplus(-lam_ref[0, 0])
        a = jnp.exp(log_a)
        a_ref[bb] = a
        s_ref[bb] = jnp.sqrt(jnp.tanh(-log_a) * (1.0 + a * a)) * (ig * xc)

    half = BLK // 2
    dst = hb_ref if reverse else o_ref
    if reverse:
        first, second, lo2 = range(BLK - 1, half - 1, -1), range(half - 1, -1, -1), 0
    else:
        first, second, lo2 = range(0, half), range(half, BLK), half
    hs = [h_ref[bb] for bb in range(n_batch)]
    ps = [None] * n_batch
    qs = [None] * n_batch
    for t1, t2 in zip(first, second):
        for bb in range(n_batch):
            hs[bb] = a_ref[bb, t1:t1 + 1, :] * hs[bb] + s_ref[bb, t1:t1 + 1, :]
            dst[bb, t1:t1 + 1, :] = hs[bb]
            a2 = a_ref[bb, t2:t2 + 1, :]
            s2 = s_ref[bb, t2:t2 + 1, :]
            ps[bb] = a2 if ps[bb] is None else a2 * ps[bb]
            qs[bb] = s2 if qs[bb] is None else a2 * qs[bb] + s2
            dst[bb, t2:t2 + 1, :] = qs[bb]
            p_ref[bb, t2 - lo2:t2 - lo2 + 1, :] = ps[bb]
    for bb in range(n_batch):
        dst[bb, lo2:lo2 + half, :] = dst[bb, lo2:lo2 + half, :] + p_ref[bb] * hs[bb]
        h_ref[bb] = qs[bb] + ps[bb] * hs[bb]
        if reverse:
            o_ref[bb] = ((hf_ref[bb] + hb_ref[bb]) * _silu(u_ref[bb][:, BRANCH:2 * BRANCH])).astype(BF16)


def _lru_call(u, layer, cw, cb, wg, bg, lam, reverse, h_fwd=None):
    b, rows, _ = u.shape
    nblk = rows // BLK
    main, prev, nxt = _pass_specs(nblk, reverse, b, W_LRU)
    dr = 1 if reverse else 0
    in_specs = [pl.BlockSpec((b, BLK, W_LRU), main), prev, nxt,
                _layer_spec(cw.shape, layer, 1), _layer_spec(cb.shape, layer, 1),
                pl.BlockSpec((1, 1, BRANCH, 2 * BRANCH), lambda i: (layer, dr, 0, 0)),
                pl.BlockSpec((1, 1, 1, 2 * BRANCH), lambda i: (layer, dr, 0, 0)),
                pl.BlockSpec((1, 1, 1, BRANCH), lambda i: (layer, dr, 0, 0))]
    args = [u, u, u, cw, cb, wg, bg, lam]
    scratch = [pltpu.VMEM((b, BLK + 2 * HALO, BRANCH), F32), pltpu.VMEM((b, BLK, BRANCH), F32),
               pltpu.VMEM((b, BLK, BRANCH), F32), pltpu.VMEM((b, BLK // 2, BRANCH), F32)]
    if reverse:
        in_specs.append(pl.BlockSpec((b, BLK, BRANCH), main))
        args.append(h_fwd)
        scratch.append(pltpu.VMEM((b, BLK, BRANCH), F32))
    scratch.append(pltpu.VMEM((b, 1, BRANCH), F32))
    return pl.pallas_call(
        functools.partial(_lru_body, reverse=reverse, nblk=nblk, n_batch=b),
        grid=(nblk,),
        in_specs=in_specs,
        out_specs=pl.BlockSpec((b, BLK, BRANCH), main),
        out_shape=jax.ShapeDtypeStruct((b, rows, BRANCH), BF16 if reverse else F32),
        scratch_shapes=scratch,
        compiler_params=_params(1),
        name="lru_bwd" if reverse else "lru_fwd",
    )(*args)


def _ssd_blocks(ubs, xcs, dtb, aneg, sts, reverse):
    nb = len(ubs)
    off = SSD_HEADS if reverse else 0
    ch = SSD_CHUNK
    hpg = SSD_HEADS // SSD_GROUPS
    r = _iota((ch, ch), 0)
    c = _iota((ch, ch), 1)
    keep = (c >= r) if reverse else (c <= r)
    tri = keep.astype(BF16)
    expand = (_iota((128, BRANCH), 0) == off + (_iota((128, BRANCH), 1) >> 6)).astype(BF16)
    lane_n = _iota((ch, 128), 1) >> 6
    lane_p = _iota((ch, BRANCH), 1) >> 6
    bd = (_iota((128, BRANCH), 0) >> 6) == (_iota((128, BRANCH), 1) >> 7)
    edge = 0 if reverse else ch - 1

    n_chunks = BLK // ch
    order = list(reversed(range(n_chunks))) if reverse else list(range(n_chunks))
    cb_keys = [(ci, bb) for ci in order for bb in range(nb)]
    rows = lambda a, ci: a[ci * ch:(ci + 1) * ch]
    xs = {k: rows(xcs[k[1]][:, 0:256], k[0]) for k in cb_keys}
    bm = {k: rows(xcs[k[1]][:, 256:384], k[0]).astype(BF16) for k in cb_keys}
    cm = {k: rows(xcs[k[1]][:, 384:512], k[0]) for k in cb_keys}
    dt = {k: jax.nn.softplus(rows(ubs[k[1]][:, 768:896], k[0]) + dtb) for k in cb_keys}

    cum = {k: _dot_exact_left(tri, dt[k] * aneg) for k in cb_keys}
    cum_x = {k: _dot_exact_right(cum[k], expand, terms=2) for k in cb_keys}
    dt_x = {k: _dot_exact_right(dt[k], expand, terms=2) for k in cb_keys}
    cum_t = {k: cum[k].T for k in cb_keys}
    dt_t = {k: dt[k].T for k in cb_keys}
    cbm = {(k, g): _dot_nt(jnp.where(lane_n == g, cm[k], 0.0).astype(BF16), bm[k])
           for k in cb_keys for g in range(SSD_GROUPS)}

    def decay_mix(k, h):
        seg = cum[k][:, off + h:off + h + 1] - cum_t[k][off + h:off + h + 1, :]
        lm = jnp.exp(jnp.where(keep, seg, -1e30))
        return (cbm[(k, h // hpg)] * lm * dt_t[k][off + h:off + h + 1, :]).astype(BF16)

    y_in = {}
    for k in cb_keys:
        m_cat = jnp.concatenate([decay_mix(k, h) for h in range(SSD_HEADS)], axis=1)
        x_bd = jnp.concatenate([jnp.where(lane_p == h, xs[k], 0.0) for h in range(SSD_HEADS)], axis=0)
        y_in[k] = _dot(m_cat, x_bd.astype(BF16))
    tot_x = {k: cum_x[k][edge:edge + 1, :] for k in cb_keys}
    upd = {k: jnp.where(bd, _dot_tn(bm[k], (xs[k] * (jnp.exp(tot_x[k] - cum_x[k]) * dt_x[k])).astype(BF16)), 0.0)
           for k in cb_keys}
    sts = list(sts)
    outs = [[None] * n_chunks for _ in range(nb)]
    for k in cb_keys:
        ci, bb = k
        outs[bb][ci] = y_in[k] + _dot(cm[k].astype(BF16), sts[bb].astype(BF16)) * jnp.exp(cum_x[k])
        sts[bb] = sts[bb] * jnp.exp(tot_x[k]) + upd[k]
    return [jnp.concatenate(o, axis=0) for o in outs], sts


def _ssd_body(*refs, reverse, nblk, n_batch):
    if reverse:
        (u_ref, prev_ref, next_ref, cw_ref, cb_ref, dtb_ref, an_ref, yf_ref, dsk_ref, nw_ref,
         o_ref, xe_ref, st_ref) = refs
    else:
        u_ref, prev_ref, next_ref, cw_ref, cb_ref, dtb_ref, an_ref, o_ref, xe_ref, st_ref = refs
    i = pl.program_id(0)
    blk = _bwd_block(i, nblk) if reverse else i

    @pl.when(i == 0)
    def _():
        st_ref[...] = jnp.zeros_like(st_ref)

    ubs = [u_ref[bb] for bb in range(n_batch)]
    xcs = [_silu(_short_conv(xe_ref.at[bb], ubs[bb][:, 0:512], prev_ref[bb][:, 0:512], next_ref[bb][:, 0:512],
                             blk, nblk, cw_ref[0], cb_ref[0])) for bb in range(n_batch)]
    ys, sts = _ssd_blocks(ubs, xcs, dtb_ref[0], an_ref[0], [st_ref[bb] for bb in range(n_batch)], reverse)
    for bb in range(n_batch):
        st_ref[bb] = sts[bb]
        if reverse:
            y = ys[bb] + yf_ref[bb] + dsk_ref[0] * xcs[bb][:, 0:256]
            yz = y * _silu(ubs[bb][:, 512:768])
            ms = jnp.mean(yz * yz, axis=-1, keepdims=True)
            o_ref[bb] = (yz * lax.rsqrt(ms + EPS) * nw_ref[0]).astype(BF16)
        else:
            o_ref[bb] = ys[bb]


def _ssd_call(u, layer, cw, cb, dtb, aneg, reverse, y_fwd=None, dskip=None, nw=None):
    b, rows, _ = u.shape
    nblk = rows // BLK
    main, prev, nxt = _pass_specs(nblk, reverse, b, W_SSD)
    in_specs = [pl.BlockSpec((b, BLK, W_SSD), main), prev, nxt,
                _layer_spec(cw.shape, layer, 1), _layer_spec(cb.shape, layer, 1),
                _layer_spec(dtb.shape, layer, 1), _layer_spec(aneg.shape, layer, 1)]
    args = [u, u, u, cw, cb, dtb, aneg]
    if reverse:
        in_specs += [pl.BlockSpec((b, BLK, BRANCH), main),
                     _layer_spec(dskip.shape, layer, 1), _layer_spec(nw.shape, layer, 1)]
        args += [y_fwd, dskip, nw]
    return pl.pallas_call(
        functools.partial(_ssd_body, reverse=reverse, nblk=nblk, n_batch=b),
        grid=(nblk,),
        in_specs=in_specs,
        out_specs=pl.BlockSpec((b, BLK, BRANCH), main),
        out_shape=jax.ShapeDtypeStruct((b, rows, BRANCH), BF16 if reverse else F32),
        scratch_shapes=[pltpu.VMEM((b, BLK + 2 * HALO, 512), F32), pltpu.VMEM((b, 128, BRANCH), F32)],
        compiler_params=_params(1),
        name="ssd_bwd" if reverse else "ssd_fwd",
    )(*args)


def _key_step(nk):
    return max(t for t in range(BLK, min(ATT_TK, nk) + 1, BLK) if nk % t == 0)


def _attn_body(*refs, nk, lam_init, n_qb):
    qt_refs, (k_ref, vt_ref), g_refs = refs[0:n_qb], refs[n_qb:n_qb + 2], refs[n_qb + 2:2 * n_qb + 2]
    lam_ref, sw_ref, o_ref, s0_ref, s1_ref = refs[2 * n_qb + 2:]
    tk = _key_step(nk)
    n_steps = nk // tk
    sub = tk // BLK
    fold = 32
    lv = lam_ref[0][0:4, 0:DIFF_D]
    lam = (jnp.exp(jnp.sum(lv[0:1] * lv[1:2], axis=-1, keepdims=True))
           - jnp.exp(jnp.sum(lv[2:3] * lv[3:4], axis=-1, keepdims=True)) + lam_init)
    row = _iota((128, BLK), 0)
    pairs = [(qb, h, half) for qb in range(n_qb) for h in range(DIFF_HEADS) for half in range(2)]
    s_bufs = (s0_ref, s1_ref)

    def masked_qt(p):
        qb, h, half = pairs[p]
        qt = qt_refs[qb][0, 0, (h // 2) * 128:(h // 2 + 1) * 128, :]
        lo = (h % 2) * 64 + half * DIFF_D
        return jnp.where(jnp.logical_and(row >= lo, row < lo + DIFF_D), qt, jnp.zeros_like(qt))

    def key_start(j):
        return j * tk if isinstance(j, int) else pl.multiple_of(j * tk, BLK)

    def stage_a(p, qm, j, jj, mx):
        tile = pairs[p][1] // 2
        rows = pl.ds(key_start(j) + jj * BLK, BLK)
        s = _dot(k_ref[0, rows, tile * 128:(tile + 1) * 128], qm)
        s_bufs[p % 2][rows, :] = s
        return jnp.maximum(mx, jnp.max(s.reshape(BLK // fold, fold, BLK), axis=0))

    def stage_bc(p, m, j, jj, ls, acc):
        h = pairs[p][1]
        e = jnp.exp2(s_bufs[p % 2][pl.ds(key_start(j) + jj * BLK, BLK), :] - m)
        ls = ls + jnp.sum(e.reshape(BLK // fold, fold, BLK), axis=0)
        acc = acc + _dot(vt_ref[0, j * sub + jj, h * DIFF_DV:(h + 1) * DIFF_DV, :], e.astype(BF16))
        return ls, acc

    results = []
    m_prev = None
    for p in range(len(pairs) + 1):
        do_a = p < len(pairs)
        do_bc = p >= 1
        qm = masked_qt(p) if do_a else None

        def step(j, carry, p=p, qm=qm, m_prev=m_prev, do_a=do_a, do_bc=do_bc):
            mx, ls, acc = carry
            for jj in range(sub):
                if do_bc:
                    ls, acc = stage_bc(p - 1, m_prev, j, jj, ls, acc)
                if do_a:
                    mx = stage_a(p, qm, j, jj, mx)
            return mx, ls, acc

        carry = (jnp.full((fold, BLK), -jnp.inf, F32), jnp.zeros((fold, BLK), F32),
                 jnp.zeros((DIFF_DV, BLK), F32))
        if n_steps > 1:
            mx, ls, acc = lax.fori_loop(0, n_steps, step, carry)
        else:
            mx, ls, acc = step(0, carry)
        if do_bc:
            results.append(acc / jnp.sum(ls, axis=0, keepdims=True))
        if do_a:
            m_prev = jnp.max(mx, axis=0, keepdims=True)

    grp = ((_iota((BRANCH, BRANCH), 0) >> 6) == (_iota((BRANCH, BRANCH), 1) >> 6)).astype(BF16)
    per_qb = 2 * DIFF_HEADS
    for qb in range(n_qb):
        res = results[qb * per_qb:(qb + 1) * per_qb]
        heads = [res[2 * h] - lam * res[2 * h + 1] for h in range(DIFF_HEADS)]
        o = jnp.concatenate(heads, axis=0).T
        ms = _dot_exact_right(o * o, grp, terms=2) * (1.0 / DIFF_DV)
        y = o * lax.rsqrt(ms + EPS) * sw_ref[0][0:1, :] * (1.0 - lam_init)
        o_ref[0, qb * BLK:(qb + 1) * BLK, :] = (y * _silu(g_refs[qb][0])).astype(BF16)


def _attn_call(qt, k, vt, gate, layer, lam_vecs, sw, lam_init, ctx_only):
    b, rows, _ = k.shape
    nblk = rows // BLK
    nk = BLK if ctx_only else rows
    nq = 1 if ctx_only else nblk - 1
    qoff = 0 if ctx_only else 1
    n_qb = max(t for t in (1, 2, 4) if nq % t == 0)
    q_specs = [pl.BlockSpec((1, 1, BRANCH, BLK), lambda bb, i, t=t: (bb, n_qb * i + t + qoff, 0, 0))
               for t in range(n_qb)]
    g_specs = [pl.BlockSpec((1, BLK, BRANCH), lambda bb, i, t=t: (bb, n_qb * i + t + qoff, 0))
               for t in range(n_qb)]
    return pl.pallas_call(
        functools.partial(_attn_body, nk=nk, lam_init=lam_init, n_qb=n_qb),
        grid=(b, nq // n_qb),
        in_specs=q_specs + [pl.BlockSpec((1, nk, BRANCH), lambda bb, i: (bb, 0, 0)),
                            pl.BlockSpec((1, nk // BLK, BRANCH, BLK), lambda bb, i: (bb, 0, 0, 0))]
        + g_specs + [_layer_spec(lam_vecs.shape, layer, 2), _layer_spec(sw.shape, layer, 2)],
        out_specs=pl.BlockSpec((1, n_qb * BLK, BRANCH), lambda bb, i: (bb, i, 0)),
        out_shape=jax.ShapeDtypeStruct((b, nq * BLK, BRANCH), BF16),
        scratch_shapes=[pltpu.VMEM((nk, BLK), F32), pltpu.VMEM((nk, BLK), F32)],
        compiler_params=_params(2),
        name="diff_attn_ctx" if ctx_only else "diff_attn",
    )(*([qt] * n_qb), k, vt, *([gate] * n_qb), lam_vecs, sw)


W_IN_ROWS = 128


def _w_in_layout_body(w_ref, o_ref):
    w = w_ref[0]

    def group(lo, pieces, width):
        parts = [w[:, lo + a:lo + b] for a, b in pieces]
        used = sum(b - a for a, b in pieces)
        if used < width:
            parts.append(jnp.zeros((W_IN_ROWS, width - used), F32))
        return parts

    parts = (group(COL_GLA, [(0, 512), (544, 800), (512, 544)], W_GLA)
             + group(COL_LRU, [(0, 512)], W_LRU)
             + group(COL_DIFF, [(0, 1024)], W_DIFF)
             + group(COL_SSD, [(0, 512), (520, 776), (512, 520)], W_SSD))
    o_ref[0] = jnp.concatenate(parts, axis=-1).astype(BF16)


def _w_in_layout_call(w_in):
    depth, d, n = w_in.shape
    return pl.pallas_call(
        _w_in_layout_body,
        grid=(depth, d // W_IN_ROWS),
        in_specs=[pl.BlockSpec((1, W_IN_ROWS, n), lambda l, r: (l, r, 0))],
        out_specs=pl.BlockSpec((1, W_IN_ROWS, W_ALL), lambda l, r: (l, r, 0)),
        out_shape=jax.ShapeDtypeStruct((depth, d, W_ALL), BF16),
        compiler_params=_params(2),
        name="w_in_layout",
    )(w_in)


def _block_diag(w):
    nb, n = w.shape[-3], w.shape[-1]
    eye = jnp.eye(nb, dtype=w.dtype)
    dense = eye[:, None, :, None] * w[..., :, :, None, :]
    return dense.reshape(w.shape[:-3] + (nb * n, nb * n))


def _rope_tables(t_lat):
    n_freq = DIFF_D // 4
    inv = ROPE_BASE ** (-jnp.arange(n_freq, dtype=F32) / n_freq)
    tpos = jnp.arange(t_lat)
    ang_r = (tpos // GRID_W).astype(F32)[:, None] * inv
    ang_c = (tpos % GRID_W).astype(F32)[:, None] * inv
    ang = jnp.concatenate([ang_r, ang_r, ang_c, ang_c], axis=-1)
    reps = 128 // DIFF_D
    sign = jnp.where((jnp.arange(DIFF_D) % 16) < 8, -1.0, 1.0).astype(F32)
    return jnp.tile(jnp.cos(ang), (1, reps)), jnp.tile(jnp.sin(ang) * sign, (1, reps))


def kernel(x, c, ctx, c_ctx, w_mod, b_mod, norm_w, w_in, w_out, gla_w2, gla_b2, gla_norm_w, lru_conv_w, lru_conv_b, lru_wa, lru_ba, lru_wx, lru_bx, lru_lam, diff_lam, diff_subln_w, ssd_conv_w, ssd_conv_b, ssd_dt_bias, ssd_a_log, ssd_d, ssd_norm_w, final_norm_w):
    bsz, t_lat, d = x.shape
    depth = w_mod.shape[0]
    assert ctx.shape[1] == BLK and t_lat % BLK == 0 and d == D_MODEL and bsz + 1 <= 8
    nblk = 1 + t_lat // BLK

    cvec = jnp.concatenate([c, c_ctx[None], jnp.zeros((8 - bsz - 1, d), F32)], axis=0)
    mod = _mod_call(cvec, w_mod, b_mod)
    cos, sin = _rope_tables(t_lat)

    nw3 = norm_w[:, None, :]
    w_in_p = _w_in_layout_call(w_in)
    w_out_b = w_out.astype(BF16)
    pad_r = lambda w, lo: jnp.pad(w, ((0, 0), (lo, 128 - GLA_RANK - lo), (0, 0)))
    w2 = jnp.stack([pad_r(gla_w2[:, 0], 0), pad_r(gla_w2[:, 1], GLA_RANK)], axis=1).astype(BF16)
    b2 = gla_b2[:, :, None, :]
    gnw = jnp.tile(gla_norm_w, (1, GLA_HEADS))[:, None, :]
    wg = jnp.concatenate([_block_diag(lru_wa), _block_diag(lru_wx)], axis=-1).astype(BF16)
    bg = jnp.concatenate([lru_ba, lru_bx], axis=-1)[:, :, None, :]
    llam = lru_lam[:, :, None, :]
    lcb = lru_conv_b[:, None, :]
    sw = jnp.pad(jnp.tile(diff_subln_w, (1, DIFF_HEADS))[:, None, :], ((0, 0), (0, 15), (0, 0)))
    lam_p = jnp.pad(diff_lam, ((0, 0), (0, 32 - diff_lam.shape[1]), (0, 128 - DIFF_D)))
    pad_l = lambda v: jnp.pad(v.reshape(depth, -1), ((0, 0), (0, 128 - 2 * SSD_HEADS)))[:, None, :]
    dtb = pad_l(ssd_dt_bias)
    aneg = pad_l(-jnp.exp(ssd_a_log))
    dsk = jnp.repeat(ssd_d, SSD_P, axis=-1)[:, None, :]
    scb = ssd_conv_b[:, None, :]
    snw = ssd_norm_w[:, None, :]

    lat, lat_off, cx = x, 0, ctx
    proj = _inproj_call(lat, lat_off, cx, 0, nblk, nw3, mod, w_in_p, cos, sin)
    for l in range(depth):
        last = l == depth - 1
        lam_init = 0.8 - 0.6 * math.exp(-0.3 * l)
        ua, ub, ud, qt, kp, vt, gd = proj

        of = _gla_call(ua, l, w2, b2, False)
        ya = _gla_call(ua, l, w2, b2, True, of, gnw)

        hf = _lru_call(ub, l, lru_conv_w, lcb, wg, bg, llam, False)
        yb = _lru_call(ub, l, lru_conv_w, lcb, wg, bg, llam, True, hf)

        yc_lat = _attn_call(qt, kp, vt, gd, l, lam_p, sw, lam_init, False)
        yc_ctx = yc_lat if last else _attn_call(qt, kp, vt, gd, l, lam_p, sw, lam_init, True)

        yf = _ssd_call(ud, l, ssd_conv_w, scb, dtb, aneg, False)
        yd = _ssd_call(ud, l, ssd_conv_w, scb, dtb, aneg, True, yf, dsk, snw)

        if last:
            return _outproj_final_call(ya, yb, yc_lat, yd, lat, lat_off, cx, l, nblk, mod, w_out_b,
                                       final_norm_w[None])
        h, *proj = _outproj_next_call(ya, yb, yc_lat, yc_ctx, yd, lat, lat_off, cx, l, nblk, mod, w_out_b,
                                      nw3, w_in_p, cos, sin)
        lat, lat_off, cx = h, 1, h
```

```python
import functools
import math

import jax
import jax.numpy as jnp
from jax import lax
from jax.experimental import pallas as pl
from jax.experimental.pallas import tpu as pltpu

F32 = jnp.float32
BF16 = jnp.bfloat16

D_MODEL = 1024
BRANCH = D_MODEL // 4
GRID_W = 64
EPS = 1e-6
CONV_W = 4
BLK = 256
HALO = 8
GLA_HEADS, GLA_DK, GLA_DV = 4, 32, 64
GLA_RANK = 16
GLA_TAU = 16.0
GLA_CHUNK = 64
LRU_C = 8.0
DIFF_HEADS, DIFF_D, DIFF_DV = 4, 32, 64
ROPE_BASE = 10000.0
SSD_HEADS, SSD_P, SSD_N, SSD_GROUPS = 4, 64, 64, 2
SSD_CHUNK = 128
ATT_TK = 2816
W_GLA, W_LRU, W_DIFF, W_SSD = 896, 512, 1024, 896
COL_GLA, COL_LRU, COL_DIFF, COL_SSD = 0, 800, 1312, 2336
W_ALL = W_GLA + W_LRU + W_DIFF + W_SSD
VMEM_LIMIT = 56 * 1024 * 1024


def _dot(a, b):
    return jnp.dot(a, b, preferred_element_type=F32)


def _dot_nt(a, b):
    return lax.dot_general(a, b, (((1,), (1,)), ((), ())), preferred_element_type=F32)


def _dot_tn(a, b):
    return lax.dot_general(a, b, (((0,), (0,)), ((), ())), preferred_element_type=F32)


def _split(x, terms):
    out = []
    r = x
    for _ in range(terms):
        p = r.astype(BF16)
        out.append(p)
        r = r - p.astype(F32)
    return out


def _dot_exact_left(m, x, terms=3):
    return sum(_dot(m, p) for p in _split(x, terms))


def _dot_exact_right(x, m, terms=3):
    return sum(_dot(p, m) for p in _split(x, terms))


def _iota(shape, dim):
    return lax.broadcasted_iota(jnp.int32, shape, dim)


def _silu(x):
    return x * jax.nn.sigmoid(x)


def _params(n_axes):
    return pltpu.CompilerParams(dimension_semantics=("arbitrary",) * n_axes, vmem_limit_bytes=VMEM_LIMIT)


def _layer_spec(shape, layer, n_axes):
    block = (1,) + tuple(shape[1:])
    idx = (layer,) + (0,) * (len(shape) - 1)
    if n_axes == 1:
        return pl.BlockSpec(block, lambda i: idx)
    return pl.BlockSpec(block, lambda bb, i: idx)


def _bwd_block(i, nblk):
    return jnp.where(i == 0, 0, nblk - i)


def _mod_body(c_ref, w_ref, b_ref, o_ref):
    s = _silu(c_ref[...])
    o_ref[0] = _dot(s.astype(BF16), w_ref[0].astype(BF16)) + b_ref[0]


def _mod_call(cvec, w_mod, b_mod):
    depth, d, n = w_mod.shape
    tn = 1024
    return pl.pallas_call(
        _mod_body,
        grid=(depth, n // tn),
        in_specs=[pl.BlockSpec((8, d), lambda l, j: (0, 0)),
                  pl.BlockSpec((1, d, tn), lambda l, j: (l, 0, j)),
                  pl.BlockSpec((1, 1, tn), lambda l, j: (l, 0, j))],
        out_specs=pl.BlockSpec((1, 8, tn), lambda l, j: (l, 0, j)),
        out_shape=jax.ShapeDtypeStruct((depth, 8, n), F32),
        compiler_params=_params(2),
        name="adaln_mod",
    )(cvec, w_mod, b_mod.reshape(depth, 1, n))


def _mod_row(mod_ref, is_ctx, bb, n_batch, part):
    r = jnp.where(is_ctx, n_batch, bb)
    return mod_ref[0, pl.ds(r, 1), part * D_MODEL:(part + 1) * D_MODEL]


def _project_in(x, is_ctx, bb, n_batch, nw_ref, mod_ref, w_ref, cos_ref, sin_ref, outs):
    oa_ref, ob_ref, od_ref, qt_ref, k_ref, vt_ref, g_ref = outs
    ms = jnp.mean(x * x, axis=-1, keepdims=True)
    xn = x * lax.rsqrt(ms + EPS) * nw_ref[0]
    xm = xn * (1.0 + _mod_row(mod_ref, is_ctx, bb, n_batch, 1)) + _mod_row(mod_ref, is_ctx, bb, n_batch, 0)
    u = _dot(xm.astype(BF16), w_ref[0])
    oa_ref[0] = u[:, 0:W_GLA]
    ob_ref[0] = u[:, W_GLA:W_GLA + W_LRU]
    od_ref[0] = u[:, W_GLA + W_LRU + W_DIFF:W_ALL]

    ud = u[:, W_GLA + W_LRU:W_GLA + W_LRU + W_DIFF]
    lane_tiles = BRANCH // 128
    cs = jnp.concatenate([jnp.where(is_ctx, 1.0, cos_ref[...])] * lane_tiles, axis=1)
    sn = jnp.concatenate([jnp.where(is_ctx, 0.0, sin_ref[...])] * lane_tiles, axis=1)
    first = (_iota((BLK, BRANCH), 1) & 15) < 8

    def rope(t):
        rot = jnp.where(first, pltpu.roll(t, BRANCH - 8, 1), pltpu.roll(t, 8, 1))
        return t * cs + rot * sn

    qt_ref[0, 0] = (rope(ud[:, 0:256]) * (DIFF_D ** -0.5 * math.log2(math.e))).T.astype(BF16)
    k_ref[0] = rope(ud[:, 256:512]).astype(BF16)
    vt_ref[0, 0] = ud[:, 512:768].T.astype(BF16)
    g_ref[0] = ud[:, 768:1024]


def _inproj_body(lat_ref, ctx_ref, nw_ref, mod_ref, w_ref, cos_ref, sin_ref, *outs, n_batch):
    is_ctx = pl.program_id(1) == 0
    x = jnp.where(is_ctx, ctx_ref[0], lat_ref[0])
    _project_in(x, is_ctx, pl.program_id(0), n_batch, nw_ref, mod_ref, w_ref, cos_ref, sin_ref, outs)


def _inproj_specs(b, nblk, layer, nw, mod, w):
    rows = nblk * BLK
    row_out = lambda n: pl.BlockSpec((1, BLK, n), lambda bb, i: (bb, i, 0))
    t_out = pl.BlockSpec((1, 1, BRANCH, BLK), lambda bb, i: (bb, i, 0, 0))
    tab = pl.BlockSpec((BLK, 128), lambda bb, i: (jnp.maximum(i - 1, 0), 0))
    in_specs = [_layer_spec(nw.shape, layer, 2), _layer_spec(mod.shape, layer, 2), _layer_spec(w.shape, layer, 2),
                tab, tab]
    out_specs = [row_out(W_GLA), row_out(W_LRU), row_out(W_SSD), t_out, row_out(BRANCH), t_out, row_out(BRANCH)]
    out_shape = [jax.ShapeDtypeStruct((b, rows, W_GLA), F32),
                 jax.ShapeDtypeStruct((b, rows, W_LRU), F32),
                 jax.ShapeDtypeStruct((b, rows, W_SSD), F32),
                 jax.ShapeDtypeStruct((b, nblk, BRANCH, BLK), BF16),
                 jax.ShapeDtypeStruct((b, rows, BRANCH), BF16),
                 jax.ShapeDtypeStruct((b, nblk, BRANCH, BLK), BF16),
                 jax.ShapeDtypeStruct((b, rows, BRANCH), F32)]
    return in_specs, out_specs, out_shape


def _inproj_call(lat, lat_off, ctx, layer, nblk, nw, mod, w, cos, sin):
    b = lat.shape[0]
    d = D_MODEL
    p_specs, out_specs, out_shape = _inproj_specs(b, nblk, layer, nw, mod, w)
    return pl.pallas_call(
        functools.partial(_inproj_body, n_batch=b),
        grid=(b, nblk),
        in_specs=[pl.BlockSpec((1, BLK, d), lambda bb, i: (bb, jnp.maximum(i - 1, 0) + lat_off, 0)),
                  pl.BlockSpec((1, BLK, d), lambda bb, i: (bb, 0, 0))] + p_specs,
        out_specs=out_specs,
        out_shape=out_shape,
        compiler_params=_params(2),
        name="norm_inproj",
    )(lat, ctx, nw, mod, w, cos, sin)


def _residual_update(ya_ref, yb_ref, ycl_ref, ycc_ref, yd_ref, lat_ref, ctx_ref, mod_ref, w_ref, is_ctx, bb, n_batch):
    if is_ctx is False:
        yc, res = ycl_ref[0], lat_ref[0]
    else:
        yc = jnp.where(is_ctx, ycc_ref[0], ycl_ref[0])
        res = jnp.where(is_ctx, ctx_ref[0], lat_ref[0])
    y = jnp.concatenate([ya_ref[0], yb_ref[0], yc, yd_ref[0]], axis=-1)
    return res + _mod_row(mod_ref, is_ctx, bb, n_batch, 2) * _dot(y, w_ref[0])


def _outproj_final_body(ya_ref, yb_ref, ycl_ref, ycc_ref, yd_ref, lat_ref, ctx_ref, mod_ref, w_ref, fw_ref, o_ref,
                        *, n_batch):
    hn = _residual_update(ya_ref, yb_ref, ycl_ref, ycc_ref, yd_ref, lat_ref, ctx_ref, mod_ref, w_ref,
                          False, pl.program_id(0), n_batch)
    ms = jnp.mean(hn * hn, axis=-1, keepdims=True)
    o_ref[0] = hn * lax.rsqrt(ms + EPS) * fw_ref[...]


def _outproj_next_body(ya_ref, yb_ref, ycl_ref, ycc_ref, yd_ref, lat_ref, ctx_ref, modg_ref, wo_ref,
                       nw_ref, mod_ref, wi_ref, cos_ref, sin_ref, h_ref, *outs, n_batch):
    bb = pl.program_id(0)
    is_ctx = pl.program_id(1) == 0
    hn = _residual_update(ya_ref, yb_ref, ycl_ref, ycc_ref, yd_ref, lat_ref, ctx_ref, modg_ref, wo_ref,
                          is_ctx, bb, n_batch)
    h_ref[0] = hn
    _project_in(hn, is_ctx, bb, n_batch, nw_ref, mod_ref, wi_ref, cos_ref, sin_ref, outs)


def _outproj_in_specs(lat_off, layer, mod, w, final):
    d = D_MODEL
    off = 1 if final else 0
    y_spec = pl.BlockSpec((1, BLK, BRANCH), lambda bb, i: (bb, i + off, 0))
    lat_blk = (lambda i: i) if final else (lambda i: jnp.maximum(i - 1, 0))
    return [y_spec, y_spec,
            pl.BlockSpec((1, BLK, BRANCH), lambda bb, i: (bb, lat_blk(i), 0)),
            pl.BlockSpec((1, BLK, BRANCH), lambda bb, i: (bb, 0, 0)),
            y_spec,
            pl.BlockSpec((1, BLK, d), lambda bb, i: (bb, lat_blk(i) + lat_off, 0)),
            pl.BlockSpec((1, BLK, d), lambda bb, i: (bb, 0, 0)),
            _layer_spec(mod.shape, layer, 2), _layer_spec(w.shape, layer, 2)]


def _outproj_final_call(ya, yb, yc_lat, yd, lat, lat_off, ctx, layer, nblk, mod, w, fw):
    b = lat.shape[0]
    d = D_MODEL
    nout = nblk - 1
    return pl.pallas_call(
        functools.partial(_outproj_final_body, n_batch=b),
        grid=(b, nout),
        in_specs=_outproj_in_specs(lat_off, layer, mod, w, True) + [pl.BlockSpec((1, d), lambda bb, i: (0, 0))],
        out_specs=pl.BlockSpec((1, BLK, d), lambda bb, i: (bb, i, 0)),
        out_shape=jax.ShapeDtypeStruct((b, nout * BLK, d), F32),
        compiler_params=_params(2),
        name="outproj_final",
    )(ya, yb, yc_lat, yc_lat, yd, lat, ctx, mod, w, fw)


def _outproj_next_call(ya, yb, yc_lat, yc_ctx, yd, lat, lat_off, ctx, layer, nblk, mod, w_out, nw, w_in, cos, sin):
    b = lat.shape[0]
    d = D_MODEL
    p_specs, p_out_specs, p_out_shape = _inproj_specs(b, nblk, layer + 1, nw, mod, w_in)
    return pl.pallas_call(
        functools.partial(_outproj_next_body, n_batch=b),
        grid=(b, nblk),
        in_specs=_outproj_in_specs(lat_off, layer, mod, w_out, False) + p_specs,
        out_specs=[pl.BlockSpec((1, BLK, d), lambda bb, i: (bb, i, 0))] + p_out_specs,
        out_shape=[jax.ShapeDtypeStruct((b, nblk * BLK, d), F32)] + p_out_shape,
        compiler_params=_params(2),
        name="outproj_inproj",
    )(ya, yb, yc_lat, yc_ctx, yd, lat, ctx, mod, w_out, nw, mod, w_in, cos, sin)


def _pass_specs(nblk, reverse, n_batch, width):
    def blk_of(i):
        return _bwd_block(i, nblk) if reverse else i
    per_blk = BLK // HALO
    main = lambda i: (0, blk_of(i), 0)
    prev = pl.BlockSpec((n_batch, HALO, width), lambda i: (0, jnp.maximum(blk_of(i) * per_blk - 1, 0), 0))
    nxt = pl.BlockSpec((n_batch, HALO, width),
                       lambda i: (0, jnp.minimum((blk_of(i) + 1) * per_blk, nblk * per_blk - 1), 0))
    return main, prev, nxt


def _short_conv(xe_ref, x, prev, nxt, blk, nblk, cw, cb):
    prev_ok = blk >= 2
    next_ok = jnp.logical_and(blk >= 1, blk <= nblk - 2)
    xe_ref[0:HALO, :] = jnp.where(prev_ok, prev, 0.0)
    xe_ref[HALO:HALO + BLK, :] = x
    xe_ref[HALO + BLK:2 * HALO + BLK, :] = jnp.where(next_ok, nxt, 0.0)
    left = CONV_W // 2
    y = cb + xe_ref[HALO - left:HALO - left + BLK, :] * cw[0:1, :]
    for j in range(1, CONV_W):
        y = y + xe_ref[HALO - left + j:HALO - left + j + BLK, :] * cw[j:j + 1, :]
    return y


def _gla_blocks(ubs, w2, b2, sts, reverse):
    nb = len(ubs)
    r = _iota((BLK, BLK), 0)
    c = _iota((BLK, BLK), 1)
    same = (r >> 6) == (c >> 6)
    tri = jnp.logical_and(same, (c >= r) if reverse else (c <= r))
    sel = jnp.concatenate([tri, same], axis=0).astype(BF16)
    lane_k = _iota((GLA_CHUNK, 128), 1) >> 5
    rr = _iota((GLA_HEADS * GLA_CHUNK, GLA_CHUNK), 0) & (GLA_CHUNK - 1)
    cc = _iota((GLA_HEADS * GLA_CHUNK, GLA_CHUNK), 1)
    amask = (cc >= rr) if reverse else (cc <= rr)
    lane_v = _iota((GLA_CHUNK, BRANCH), 1) >> 6
    bd = (_iota((BRANCH, 128), 0) >> 6) == (_iota((BRANCH, 128), 1) >> 5)

    las = [jax.nn.log_sigmoid(_dot(ub[:, 768:896].astype(BF16), w2) + b2) * (1.0 / GLA_TAU) for ub in ubs]
    ggs = [_dot_exact_left(sel, la, terms=2) for la in las]
    gs = [gg[0:BLK] for gg in ggs]
    gls = [gg[BLK:2 * BLK] for gg in ggs]
    qgs = [ub[:, 0:128] * (GLA_DK ** -0.5) * jnp.exp(g) for ub, g in zip(ubs, gs)]
    kgs = [(ub[:, 128:256] * jnp.exp(-g)).astype(BF16) for ub, g in zip(ubs, gs)]
    kds = [(ub[:, 128:256] * jnp.exp(gl - g)).astype(BF16) for ub, g, gl in zip(ubs, gs, gls)]
    egls = [jnp.exp(gl) for gl in gls]
    vs = [ub[:, 256:512].astype(BF16) for ub in ubs]

    n_chunks = BLK // GLA_CHUNK
    order = list(reversed(range(n_chunks))) if reverse else list(range(n_chunks))
    rows = lambda a, ci: a[ci * GLA_CHUNK:(ci + 1) * GLA_CHUNK]
    cb = [(ci, bb) for ci in order for bb in range(nb)]

    def stacked_q(ci, bb):
        qg_c = rows(qgs[bb], ci)
        return jnp.concatenate([jnp.where(lane_k == h, qg_c, 0.0) for h in range(GLA_HEADS)], axis=0).astype(BF16)

    att = {k: jnp.where(amask, _dot_nt(stacked_q(*k), rows(kgs[k[1]], k[0])), 0.0).astype(BF16) for k in cb}
    res = {k: _dot(att[k], rows(vs[k[1]], k[0])) for k in cb}
    upd = {k: jnp.where(bd, _dot_tn(rows(vs[k[1]], k[0]), rows(kds[k[1]], k[0])), 0.0) for k in cb}
    sts = list(sts)
    outs = [[None] * n_chunks for _ in range(nb)]
    for ci, bb in cb:
        o_c = _dot_nt(rows(qgs[bb], ci).astype(BF16), sts[bb].astype(BF16))
        for h in range(GLA_HEADS):
            o_c = o_c + jnp.where(lane_v == h, res[(ci, bb)][h * GLA_CHUNK:(h + 1) * GLA_CHUNK], 0.0)
        sts[bb] = sts[bb] * egls[bb][ci * GLA_CHUNK:ci * GLA_CHUNK + 1, :] + upd[(ci, bb)]
        outs[bb][ci] = o_c
    return [jnp.concatenate(o, axis=0) for o in outs], sts


def _gla_body(*refs, reverse, n_batch):
    if reverse:
        u_ref, w2_ref, b2_ref, of_ref, nw_ref, o_ref, st_ref = refs
    else:
        u_ref, w2_ref, b2_ref, o_ref, st_ref = refs

    @pl.when(pl.program_id(0) == 0)
    def _():
        st_ref[...] = jnp.zeros_like(st_ref)

    ubs = [u_ref[bb] for bb in range(n_batch)]
    os_, sts = _gla_blocks(ubs, w2_ref[0, 0], b2_ref[0, 0], [st_ref[bb] for bb in range(n_batch)], reverse)
    if reverse:
        grp = ((_iota((BRANCH, BRANCH), 0) >> 6) == (_iota((BRANCH, BRANCH), 1) >> 6)).astype(BF16)
        os_ = [o + of_ref[bb] for bb, o in enumerate(os_)]
        mss = [_dot_exact_right(o * o, grp, terms=2) * (1.0 / GLA_DV) for o in os_]
        os_ = [(o * lax.rsqrt(ms + EPS) * nw_ref[0] * _silu(ub[:, 512:768])).astype(BF16)
               for o, ms, ub in zip(os_, mss, ubs)]
    for bb in range(n_batch):
        st_ref[bb] = sts[bb]
        o_ref[bb] = os_[bb]


def _gla_call(u, layer, w2, b2, reverse, o_fwd=None, nw=None):
    b, rows, _ = u.shape
    nblk = rows // BLK
    main, _, _ = _pass_specs(nblk, reverse, b, W_GLA)
    dr = 1 if reverse else 0
    in_specs = [pl.BlockSpec((b, BLK, W_GLA), main),
                pl.BlockSpec((1, 1, 128, 128), lambda i: (layer, dr, 0, 0)),
                pl.BlockSpec((1, 1, 1, 128), lambda i: (layer, dr, 0, 0))]
    args = [u, w2, b2]
    if reverse:
        in_specs += [pl.BlockSpec((b, BLK, BRANCH), main), _layer_spec(nw.shape, layer, 1)]
        args += [o_fwd, nw]
    return pl.pallas_call(
        functools.partial(_gla_body, reverse=reverse, n_batch=b),
        grid=(nblk,),
        in_specs=in_specs,
        out_specs=pl.BlockSpec((b, BLK, BRANCH), main),
        out_shape=jax.ShapeDtypeStruct((b, rows, BRANCH), BF16 if reverse else F32),
        scratch_shapes=[pltpu.VMEM((b, BRANCH, 128), F32)],
        compiler_params=_params(1),
        name="gla_bwd" if reverse else "gla_fwd",
    )(*args)


def _lru_body(*refs, reverse, nblk, n_batch):
    if reverse:
        (u_ref, prev_ref, next_ref, cw_ref, cb_ref, wg_ref, bg_ref, lam_ref, hf_ref,
         o_ref, xe_ref, a_ref, s_ref, p_ref, hb_ref, h_ref) = refs
    else:
        (u_ref, prev_ref, next_ref, cw_ref, cb_ref, wg_ref, bg_ref, lam_ref,
         o_ref, xe_ref, a_ref, s_ref, p_ref, h_ref) = refs
    i = pl.program_id(0)
    blk = _bwd_block(i, nblk) if reverse else i

    @pl.when(i == 0)
    def _():
        h_ref[...] = jnp.zeros_like(h_ref)

    for bb in range(n_batch):
        ub = u_ref[bb]
        xc = _short_conv(xe_ref.at[bb], ub[:, 0:BRANCH], prev_ref[bb][:, 0:BRANCH], next_ref[bb][:, 0:BRANCH],
                         blk, nblk, cw_ref[0], cb_ref[0])
        z = _dot(xc.astype(BF16), wg_ref[0, 0]) + bg_ref[0, 0]
        rg = jax.nn.sigmoid(z[:, 0:BRANCH])
        ig = jax.nn.sigmoid(z[:, BRANCH:2 * BRANCH])
        log_a = -LRU_C * rg * jax.nn.softplus(-lam_ref[0, 0])
        a = jnp.exp(log_a)
        a_ref[bb] = a
        s_ref[bb] = jnp.sqrt(jnp.tanh(-log_a) * (1.0 + a * a)) * (ig * xc)

    half = BLK // 2
    dst = hb_ref if reverse else o_ref
    if reverse:
        first, second, lo2 = range(BLK - 1, half - 1, -1), range(half - 1, -1, -1), 0
    else:
        first, second, lo2 = range(0, half), range(half, BLK), half
    hs = [h_ref[bb] for bb in range(n_batch)]
    ps = [None] * n_batch
    qs = [None] * n_batch
    for t1, t2 in zip(first, second):
        for bb in range(n_batch):
            hs[bb] = a_ref[bb, t1:t1 + 1, :] * hs[bb] + s_ref[bb, t1:t1 + 1, :]
            dst[bb, t1:t1 + 1, :] = hs[bb]
            a2 = a_ref[bb, t2:t2 + 1, :]
            s2 = s_ref[bb, t2:t2 + 1, :]
            ps[bb] = a2 if ps[bb] is None else a2 * ps[bb]
            qs[bb] = s2 if qs[bb] is None else a2 * qs[bb] + s2
            dst[bb, t2:t2 + 1, :] = qs[bb]
            p_ref[bb, t2 - lo2:t2 - lo2 + 1, :] = ps[bb]
    for bb in range(n_batch):
        dst[bb, lo2:lo2 + half, :] = dst[bb, lo2:lo2 + half, :] + p_ref[bb] * hs[bb]
        h_ref[bb] = qs[bb] + ps[bb] * hs[bb]
        if reverse:
            o_ref[bb] = ((hf_ref[bb] + hb_ref[bb]) * _silu(u_ref[bb][:, BRANCH:2 * BRANCH])).astype(BF16)


def _lru_call(u, layer, cw, cb, wg, bg, lam, reverse, h_fwd=None):
    b, rows, _ = u.shape
    nblk = rows // BLK
    main, prev, nxt = _pass_specs(nblk, reverse, b, W_LRU)
    dr = 1 if reverse else 0
    in_specs = [pl.BlockSpec((b, BLK, W_LRU), main), prev, nxt,
                _layer_spec(cw.shape, layer, 1), _layer_spec(cb.shape, layer, 1),
                pl.BlockSpec((1, 1, BRANCH, 2 * BRANCH), lambda i: (layer, dr, 0, 0)),
                pl.BlockSpec((1, 1, 1, 2 * BRANCH), lambda i: (layer, dr, 0, 0)),
                pl.BlockSpec((1, 1, 1, BRANCH), lambda i: (layer, dr, 0, 0))]
    args = [u, u, u, cw, cb, wg, bg, lam]
    scratch = [pltpu.VMEM((b, BLK + 2 * HALO, BRANCH), F32), pltpu.VMEM((b, BLK, BRANCH), F32),
               pltpu.VMEM((b, BLK, BRANCH), F32), pltpu.VMEM((b, BLK // 2, BRANCH), F32)]
    if reverse:
        in_specs.append(pl.BlockSpec((b, BLK, BRANCH), main))
        args.append(h_fwd)
        scratch.append(pltpu.VMEM((b, BLK, BRANCH), F32))
    scratch.append(pltpu.VMEM((b, 1, BRANCH), F32))
    return pl.pallas_call(
        functools.partial(_lru_body, reverse=reverse, nblk=nblk, n_batch=b),
        grid=(nblk,),
        in_specs=in_specs,
        out_specs=pl.BlockSpec((b, BLK, BRANCH), main),
        out_shape=jax.ShapeDtypeStruct((b, rows, BRANCH), BF16 if reverse else F32),
        scratch_shapes=scratch,
        compiler_params=_params(1),
        name="lru_bwd" if reverse else "lru_fwd",
    )(*args)


def _ssd_blocks(ubs, xcs, dtb, aneg, sts, reverse):
    nb = len(ubs)
    off = SSD_HEADS if reverse else 0
    ch = SSD_CHUNK
    hpg = SSD_HEADS // SSD_GROUPS
    r = _iota((ch, ch), 0)
    c = _iota((ch, ch), 1)
    keep = (c >= r) if reverse else (c <= r)
    tri = keep.astype(BF16)
    expand = (_iota((128, BRANCH), 0) == off + (_iota((128, BRANCH), 1) >> 6)).astype(BF16)
    lane_n = _iota((ch, 128), 1) >> 6
    lane_p = _iota((ch, BRANCH), 1) >> 6
    bd = (_iota((128, BRANCH), 0) >> 6) == (_iota((128, BRANCH), 1) >> 7)
    edge = 0 if reverse else ch - 1

    n_chunks = BLK // ch
    order = list(reversed(range(n_chunks))) if reverse else list(range(n_chunks))
    cb_keys = [(ci, bb) for ci in order for bb in range(nb)]
    rows = lambda a, ci: a[ci * ch:(ci + 1) * ch]
    xs = {k: rows(xcs[k[1]][:, 0:256], k[0]) for k in cb_keys}
    bm = {k: rows(xcs[k[1]][:, 256:384], k[0]).astype(BF16) for k in cb_keys}
    cm = {k: rows(xcs[k[1]][:, 384:512], k[0]) for k in cb_keys}
    dt = {k: jax.nn.softplus(rows(ubs[k[1]][:, 768:896], k[0]) + dtb) for k in cb_keys}

    cum = {k: _dot_exact_left(tri, dt[k] * aneg) for k in cb_keys}
    cum_x = {k: _dot_exact_right(cum[k], expand, terms=2) for k in cb_keys}
    dt_x = {k: _dot_exact_right(dt[k], expand, terms=2) for k in cb_keys}
    cum_t = {k: cum[k].T for k in cb_keys}
    dt_t = {k: dt[k].T for k in cb_keys}
    cbm = {(k, g): _dot_nt(jnp.where(lane_n == g, cm[k], 0.0).astype(BF16), bm[k])
           for k in cb_keys for g in range(SSD_GROUPS)}

    def decay_mix(k, h):
        seg = cum[k][:, off + h:off + h + 1] - cum_t[k][off + h:off + h + 1, :]
        lm = jnp.exp(jnp.where(keep, seg, -1e30))
        return (cbm[(k, h // hpg)] * lm * dt_t[k][off + h:off + h + 1, :]).astype(BF16)

    y_in = {}
    for k in cb_keys:
        m_cat = jnp.concatenate([decay_mix(k, h) for h in range(SSD_HEADS)], axis=1)
        x_bd = jnp.concatenate([jnp.where(lane_p == h, xs[k], 0.0) for h in range(SSD_HEADS)], axis=0)
        y_in[k] = _dot(m_cat, x_bd.astype(BF16))
    tot_x = {k: cum_x[k][edge:edge + 1, :] for k in cb_keys}
    upd = {k: jnp.where(bd, _dot_tn(bm[k], (xs[k] * (jnp.exp(tot_x[k] - cum_x[k]) * dt_x[k])).astype(BF16)), 0.0)
           for k in cb_keys}
    sts = list(sts)
    outs = [[None] * n_chunks for _ in range(nb)]
    for k in cb_keys:
        ci, bb = k
        outs[bb][ci] = y_in[k] + _dot(cm[k].astype(BF16), sts[bb].astype(BF16)) * jnp.exp(cum_x[k])
        sts[bb] = sts[bb] * jnp.exp(tot_x[k]) + upd[k]
    return [jnp.concatenate(o, axis=0) for o in outs], sts


def _ssd_body(*refs, reverse, nblk, n_batch):
    if reverse:
        (u_ref, prev_ref, next_ref, cw_ref, cb_ref, dtb_ref, an_ref, yf_ref, dsk_ref, nw_ref,
         o_ref, xe_ref, st_ref) = refs
    else:
        u_ref, prev_ref, next_ref, cw_ref, cb_ref, dtb_ref, an_ref, o_ref, xe_ref, st_ref = refs
    i = pl.program_id(0)
    blk = _bwd_block(i, nblk) if reverse else i

    @pl.when(i == 0)
    def _():
        st_ref[...] = jnp.zeros_like(st_ref)

    ubs = [u_ref[bb] for bb in range(n_batch)]
    xcs = [_silu(_short_conv(xe_ref.at[bb], ubs[bb][:, 0:512], prev_ref[bb][:, 0:512], next_ref[bb][:, 0:512],
                             blk, nblk, cw_ref[0], cb_ref[0])) for bb in range(n_batch)]
    ys, sts = _ssd_blocks(ubs, xcs, dtb_ref[0], an_ref[0], [st_ref[bb] for bb in range(n_batch)], reverse)
    for bb in range(n_batch):
        st_ref[bb] = sts[bb]
        if reverse:
            y = ys[bb] + yf_ref[bb] + dsk_ref[0] * xcs[bb][:, 0:256]
            yz = y * _silu(ubs[bb][:, 512:768])
            ms = jnp.mean(yz * yz, axis=-1, keepdims=True)
            o_ref[bb] = (yz * lax.rsqrt(ms + EPS) * nw_ref[0]).astype(BF16)
        else:
            o_ref[bb] = ys[bb]


def _ssd_call(u, layer, cw, cb, dtb, aneg, reverse, y_fwd=None, dskip=None, nw=None):
    b, rows, _ = u.shape
    nblk = rows // BLK
    main, prev, nxt = _pass_specs(nblk, reverse, b, W_SSD)
    in_specs = [pl.BlockSpec((b, BLK, W_SSD), main), prev, nxt,
                _layer_spec(cw.shape, layer, 1), _layer_spec(cb.shape, layer, 1),
                _layer_spec(dtb.shape, layer, 1), _layer_spec(aneg.shape, layer, 1)]
    args = [u, u, u, cw, cb, dtb, aneg]
    if reverse:
        in_specs += [pl.BlockSpec((b, BLK, BRANCH), main),
                     _layer_spec(dskip.shape, layer, 1), _layer_spec(nw.shape, layer, 1)]
        args += [y_fwd, dskip, nw]
    return pl.pallas_call(
        functools.partial(_ssd_body, reverse=reverse, nblk=nblk, n_batch=b),
        grid=(nblk,),
        in_specs=in_specs,
        out_specs=pl.BlockSpec((b, BLK, BRANCH), main),
        out_shape=jax.ShapeDtypeStruct((b, rows, BRANCH), BF16 if reverse else F32),
        scratch_shapes=[pltpu.VMEM((b, BLK + 2 * HALO, 512), F32), pltpu.VMEM((b, 128, BRANCH), F32)],
        compiler_params=_params(1),
        name="ssd_bwd" if reverse else "ssd_fwd",
    )(*args)


def _key_step(nk):
    return max(t for t in range(BLK, min(ATT_TK, nk) + 1, BLK) if nk % t == 0)


def _attn_body(*refs, nk, lam_init, n_qb):
    qt_refs, (k_ref, vt_ref), g_refs = refs[0:n_qb], refs[n_qb:n_qb + 2], refs[n_qb + 2:2 * n_qb + 2]
    lam_ref, sw_ref, o_ref, s0_ref, s1_ref = refs[2 * n_qb + 2:]
    tk = _key_step(nk)
    n_steps = nk // tk
    sub = tk // BLK
    fold = 32
    lv = lam_ref[0][0:4, 0:DIFF_D]
    lam = (jnp.exp(jnp.sum(lv[0:1] * lv[1:2], axis=-1, keepdims=True))
           - jnp.exp(jnp.sum(lv[2:3] * lv[3:4], axis=-1, keepdims=True)) + lam_init)
    row = _iota((128, BLK), 0)
    pairs = [(qb, h, half) for qb in range(n_qb) for h in range(DIFF_HEADS) for half in range(2)]
    s_bufs = (s0_ref, s1_ref)

    def masked_qt(p):
        qb, h, half = pairs[p]
        qt = qt_refs[qb][0, 0, (h // 2) * 128:(h // 2 + 1) * 128, :]
        lo = (h % 2) * 64 + half * DIFF_D
        return jnp.where(jnp.logical_and(row >= lo, row < lo + DIFF_D), qt, jnp.zeros_like(qt))

    def key_start(j):
        return j * tk if isinstance(j, int) else pl.multiple_of(j * tk, BLK)

    def stage_a(p, qm, j, jj, mx):
        tile = pairs[p][1] // 2
        rows = pl.ds(key_start(j) + jj * BLK, BLK)
        s = _dot(k_ref[0, rows, tile * 128:(tile + 1) * 128], qm)
        s_bufs[p % 2][rows, :] = s
        return jnp.maximum(mx, jnp.max(s.reshape(BLK // fold, fold, BLK), axis=0))

    def stage_bc(p, m, j, jj, ls, acc):
        h = pairs[p][1]
        e = jnp.exp2(s_bufs[p % 2][pl.ds(key_start(j) + jj * BLK, BLK), :] - m)
        ls = ls + jnp.sum(e.reshape(BLK // fold, fold, BLK), axis=0)
        acc = acc + _dot(vt_ref[0, j * sub + jj, h * DIFF_DV:(h + 1) * DIFF_DV, :], e.astype(BF16))
        return ls, acc

    results = []
    m_prev = None
    for p in range(len(pairs) + 1):
        do_a = p < len(pairs)
        do_bc = p >= 1
        qm = masked_qt(p) if do_a else None

        def step(j, carry, p=p, qm=qm, m_prev=m_prev, do_a=do_a, do_bc=do_bc):
            mx, ls, acc = carry
            for jj in range(sub):
                if do_bc:
                    ls, acc = stage_bc(p - 1, m_prev, j, jj, ls, acc)
                if do_a:
                    mx = stage_a(p, qm, j, jj, mx)
            return mx, ls, acc

        carry = (jnp.full((fold, BLK), -jnp.inf, F32), jnp.zeros((fold, BLK), F32),
                 jnp.zeros((DIFF_DV, BLK), F32))
        if n_steps > 1:
            mx, ls, acc = lax.fori_loop(0, n_steps, step, carry)
        else:
            mx, ls, acc = step(0, carry)
        if do_bc:
            results.append(acc / jnp.sum(ls, axis=0, keepdims=True))
        if do_a:
            m_prev = jnp.max(mx, axis=0, keepdims=True)

    grp = ((_iota((BRANCH, BRANCH), 0) >> 6) == (_iota((BRANCH, BRANCH), 1) >> 6)).astype(BF16)
    per_qb = 2 * DIFF_HEADS
    for qb in range(n_qb):
        res = results[qb * per_qb:(qb + 1) * per_qb]
        heads = [res[2 * h] - lam * res[2 * h + 1] for h in range(DIFF_HEADS)]
        o = jnp.concatenate(heads, axis=0).T
        ms = _dot_exact_right(o * o, grp, terms=2) * (1.0 / DIFF_DV)
        y = o * lax.rsqrt(ms + EPS) * sw_ref[0][0:1, :] * (1.0 - lam_init)
        o_ref[0, qb * BLK:(qb + 1) * BLK, :] = (y * _silu(g_refs[qb][0])).astype(BF16)


def _attn_call(qt, k, vt, gate, layer, lam_vecs, sw, lam_init, ctx_only):
    b, rows, _ = k.shape
    nblk = rows // BLK
    nk = BLK if ctx_only else rows
    nq = 1 if ctx_only else nblk - 1
    qoff = 0 if ctx_only else 1
    n_qb = max(t for t in (1, 2, 4, 8) if nq % t == 0)
    q_specs = [pl.BlockSpec((1, 1, BRANCH, BLK), lambda bb, i, t=t: (bb, n_qb * i + t + qoff, 0, 0))
               for t in range(n_qb)]
    g_specs = [pl.BlockSpec((1, BLK, BRANCH), lambda bb, i, t=t: (bb, n_qb * i + t + qoff, 0))
               for t in range(n_qb)]
    return pl.pallas_call(
        functools.partial(_attn_body, nk=nk, lam_init=lam_init, n_qb=n_qb),
        grid=(b, nq // n_qb),
        in_specs=q_specs + [pl.BlockSpec((1, nk, BRANCH), lambda bb, i: (bb, 0, 0)),
                            pl.BlockSpec((1, nk // BLK, BRANCH, BLK), lambda bb, i: (bb, 0, 0, 0))]
        + g_specs + [_layer_spec(lam_vecs.shape, layer, 2), _layer_spec(sw.shape, layer, 2)],
        out_specs=pl.BlockSpec((1, n_qb * BLK, BRANCH), lambda bb, i: (bb, i, 0)),
        out_shape=jax.ShapeDtypeStruct((b, nq * BLK, BRANCH), BF16),
        scratch_shapes=[pltpu.VMEM((nk, BLK), F32), pltpu.VMEM((nk, BLK), F32)],
        compiler_params=_params(2),
        name="diff_attn_ctx" if ctx_only else "diff_attn",
    )(*([qt] * n_qb), k, vt, *([gate] * n_qb), lam_vecs, sw)


W_IN_ROWS = 128


def _w_in_layout_body(w_ref, o_ref):
    w = w_ref[0]

    def group(lo, pieces, width):
        parts = [w[:, lo + a:lo + b] for a, b in pieces]
        used = sum(b - a for a, b in pieces)
        if used < width:
            parts.append(jnp.zeros((W_IN_ROWS, width - used), F32))
        return parts

    parts = (group(COL_GLA, [(0, 512), (544, 800), (512, 544)], W_GLA)
             + group(COL_LRU, [(0, 512)], W_LRU)
             + group(COL_DIFF, [(0, 1024)], W_DIFF)
             + group(COL_SSD, [(0, 512), (520, 776), (512, 520)], W_SSD))
    o_ref[0] = jnp.concatenate(parts, axis=-1).astype(BF16)


def _w_in_layout_call(w_in):
    depth, d, n = w_in.shape
    return pl.pallas_call(
        _w_in_layout_body,
        grid=(depth, d // W_IN_ROWS),
        in_specs=[pl.BlockSpec((1, W_IN_ROWS, n), lambda l, r: (l, r, 0))],
        out_specs=pl.BlockSpec((1, W_IN_ROWS, W_ALL), lambda l, r: (l, r, 0)),
        out_shape=jax.ShapeDtypeStruct((depth, d, W_ALL), BF16),
        compiler_params=_params(2),
        name="w_in_layout",
    )(w_in)


def _block_diag(w):
    nb, n = w.shape[-3], w.shape[-1]
    eye = jnp.eye(nb, dtype=w.dtype)
    dense = eye[:, None, :, None] * w[..., :, :, None, :]
    return dense.reshape(w.shape[:-3] + (nb * n, nb * n))


def _rope_tables(t_lat):
    n_freq = DIFF_D // 4
    inv = ROPE_BASE ** (-jnp.arange(n_freq, dtype=F32) / n_freq)
    tpos = jnp.arange(t_lat)
    ang_r = (tpos // GRID_W).astype(F32)[:, None] * inv
    ang_c = (tpos % GRID_W).astype(F32)[:, None] * inv
    ang = jnp.concatenate([ang_r, ang_r, ang_c, ang_c], axis=-1)
    reps = 128 // DIFF_D
    sign = jnp.where((jnp.arange(DIFF_D) % 16) < 8, -1.0, 1.0).astype(F32)
    return jnp.tile(jnp.cos(ang), (1, reps)), jnp.tile(jnp.sin(ang) * sign, (1, reps))


def kernel(x, c, ctx, c_ctx, w_mod, b_mod, norm_w, w_in, w_out, gla_w2, gla_b2, gla_norm_w, lru_conv_w, lru_conv_b, lru_wa, lru_ba, lru_wx, lru_bx, lru_lam, diff_lam, diff_subln_w, ssd_conv_w, ssd_conv_b, ssd_dt_bias, ssd_a_log, ssd_d, ssd_norm_w, final_norm_w):
    bsz, t_lat, d = x.shape
    depth = w_mod.shape[0]
    assert ctx.shape[1] == BLK and t_lat % BLK == 0 and d == D_MODEL and bsz + 1 <= 8
    nblk = 1 + t_lat // BLK

    cvec = jnp.concatenate([c, c_ctx[None], jnp.zeros((8 - bsz - 1, d), F32)], axis=0)
    mod = _mod_call(cvec, w_mod, b_mod)
    cos, sin = _rope_tables(t_lat)

    nw3 = norm_w[:, None, :]
    w_in_p = _w_in_layout_call(w_in)
    w_out_b = w_out.astype(BF16)
    pad_r = lambda w, lo: jnp.pad(w, ((0, 0), (lo, 128 - GLA_RANK - lo), (0, 0)))
    w2 = jnp.stack([pad_r(gla_w2[:, 0], 0), pad_r(gla_w2[:, 1], GLA_RANK)], axis=1).astype(BF16)
    b2 = gla_b2[:, :, None, :]
    gnw = jnp.tile(gla_norm_w, (1, GLA_HEADS))[:, None, :]
    wg = jnp.concatenate([_block_diag(lru_wa), _block_diag(lru_wx)], axis=-1).astype(BF16)
    bg = jnp.concatenate([lru_ba, lru_bx], axis=-1)[:, :, None, :]
    llam = lru_lam[:, :, None, :]
    lcb = lru_conv_b[:, None, :]
    sw = jnp.pad(jnp.tile(diff_subln_w, (1, DIFF_HEADS))[:, None, :], ((0, 0), (0, 15), (0, 0)))
    lam_p = jnp.pad(diff_lam, ((0, 0), (0, 32 - diff_lam.shape[1]), (0, 128 - DIFF_D)))
    pad_l = lambda v: jnp.pad(v.reshape(depth, -1), ((0, 0), (0, 128 - 2 * SSD_HEADS)))[:, None, :]
    dtb = pad_l(ssd_dt_bias)
    aneg = pad_l(-jnp.exp(ssd_a_log))
    dsk = jnp.repeat(ssd_d, SSD_P, axis=-1)[:, None, :]
    scb = ssd_conv_b[:, None, :]
    snw = ssd_norm_w[:, None, :]

    lat, lat_off, cx = x, 0, ctx
    proj = _inproj_call(lat, lat_off, cx, 0, nblk, nw3, mod, w_in_p, cos, sin)
    for l in range(depth):
        last = l == depth - 1
        lam_init = 0.8 - 0.6 * math.exp(-0.3 * l)
        ua, ub, ud, qt, kp, vt, gd = proj

        of = _gla_call(ua, l, w2, b2, False)
        ya = _gla_call(ua, l, w2, b2, True, of, gnw)

        hf = _lru_call(ub, l, lru_conv_w, lcb, wg, bg, llam, False)
        yb = _lru_call(ub, l, lru_conv_w, lcb, wg, bg, llam, True, hf)

        yc_lat = _attn_call(qt, kp, vt, gd, l, lam_p, sw, lam_init, False)
        yc_ctx = yc_lat if last else _attn_call(qt, kp, vt, gd, l, lam_p, sw, lam_init, True)

        yf = _ssd_call(ud, l, ssd_conv_w, scb, dtb, aneg, False)
        yd = _ssd_call(ud, l, ssd_conv_w, scb, dtb, aneg, True, yf, dsk, snw)

        if last:
            return _outproj_final_call(ya, yb, yc_lat, yd, lat, lat_off, cx, l, nblk, mod, w_out_b,
                                       final_norm_w[None])
        h, *proj = _outproj_next_call(ya, yb, yc_lat, yc_ctx, yd, lat, lat_off, cx, l, nblk, mod, w_out_b,
                                      nw3, w_in_p, cos, sin)
        lat, lat_off, cx = h, 1, h
```
